```python
import math
import jax
import jax.numpy as jnp
from jax import lax
import numpy as np

D_MODEL = 1024
BATCH = 8
SEQ = 2048
DEPTH = 4
DEC_BATCH = 32
DEC_SEQ = 4
PAST_LEN = 8192
PAGE_SIZE = 128

N_MIXERS = 3
N_A_LAYERS = len(range(0, DEPTH, N_MIXERS))
N_B_LAYERS = len(range(1, DEPTH, N_MIXERS))
N_C_LAYERS = len(range(2, DEPTH, N_MIXERS))
NORM_EPS = 1e-6
NEG_BIG = -1e30

A_HEADS = 8
A_DK = 128
A_DV = D_MODEL // A_HEADS
A_WIDTH = A_HEADS * A_DV
A_CHUNK = 32
A_IN = 2 * A_HEADS * A_DK + 2 * A_WIDTH
A_EXP_CLIP = 60.0

B_HEADS = 8
B_HEAD_DIM = D_MODEL // B_HEADS
B_WIDTH = B_HEADS * B_HEAD_DIM
B_GROUPS = ((128, 1), (512, 4), (2048, 16))
B_BLOCK = 128
B_IN = 3 * len(B_GROUPS) * B_WIDTH + B_WIDTH
ROPE_THETA = 10000.0

C_WIDTH = D_MODEL
C_GROUP_CH = 16
C_GROUPS = C_WIDTH // C_GROUP_CH
C_STATE = 64
C_IN = 2 * C_WIDTH
C_MAX_RE = -1e-4

kernel_name = 'hybrid_hgrn2_dilswa_s5_decode_step'


def rms_norm(x, w):
    xf = x.astype(jnp.float32)
    y = xf * lax.rsqrt(jnp.mean(xf * xf, axis=-1, keepdims=True) + NORM_EPS)
    return (y * w.astype(jnp.float32)).astype(x.dtype)


def apply_rope(x, pos):
    half = x.shape[-1] // 2
    inv_freq = ROPE_THETA ** (-jnp.arange(half, dtype=jnp.float32) / half)
    ang = pos[:, None] * inv_freq[None, :]
    cos = jnp.cos(ang)[None, :, None, :]
    sin = jnp.sin(ang)[None, :, None, :]
    xf = x.astype(jnp.float32)
    x1, x2 = xf[..., :half], xf[..., half:]
    return jnp.concatenate([x1 * cos - x2 * sin, x2 * cos + x1 * sin], axis=-1).astype(x.dtype)


def hgrn2_recurrence(q, k, v, log_f, state0):
    bn, t, h, dk = q.shape
    dv = v.shape[-1]
    c = A_CHUNK if t % A_CHUNK == 0 else t
    nc = t // c

    def chunks(a):
        return a.reshape(bn, nc, c, h, a.shape[-1]).transpose(1, 0, 3, 2, 4)

    causal = jnp.tril(jnp.ones((c, c), dtype=bool))

    def step(s, inp):
        qc, kc, vc, gc = inp
        b = jnp.cumsum(gc, axis=-2)
        diff = jnp.where(causal[:, :, None], b[..., :, None, :] - b[..., None, :, :], NEG_BIG)
        scores = jnp.einsum('bhtk,bhsk,bhtsk->bhts', qc, kc, jnp.exp(diff))
        o = scores @ vc + jnp.einsum('bhtk,bhkv->bhtv', qc * jnp.exp(b), s)
        b_last = b[..., -1:, :]
        s = jnp.exp(b_last[..., 0, :])[..., None] * s + jnp.einsum('bhsk,bhsv->bhkv', kc * jnp.exp(b_last - b), vc)
        return s, o

    s0 = jnp.zeros((bn, h, dk, dv), jnp.float32) if state0 is None else state0.astype(jnp.float32)
    s_last, o = lax.scan(step, s0, (chunks(q), chunks(k), chunks(v), chunks(log_f)))
    o = o.transpose(1, 0, 3, 2, 4).reshape(bn, t, h, dv)
    return o, s_last


def hgrn2_mixer(h, w_in, lower_bound, onorm_w, w_out, state0):
    bn, t, _ = h.shape
    f32 = jnp.float32
    proj = h @ w_in
    nq = A_HEADS * A_DK
    q = jax.nn.silu(proj[..., :nq].astype(f32)).reshape(bn, t, A_HEADS, A_DK)
    zf = proj[..., nq:2 * nq].astype(f32).reshape(bn, t, A_HEADS, A_DK)
    v = proj[..., 2 * nq:2 * nq + A_WIDTH].astype(f32).reshape(bn, t, A_HEADS, A_DV)
    gate = proj[..., 2 * nq + A_WIDTH:]
    lb = lower_bound.astype(f32).reshape(A_HEADS, A_DK)
    log_f = jax.nn.log_sigmoid(zf) + jnp.log1p(lb * jnp.exp(jnp.minimum(-zf, A_EXP_CLIP)))
    k = (1.0 - lb) * jax.nn.sigmoid(-zf)
    o, s_last = hgrn2_recurrence(q, k, v, log_f, state0)
    o = rms_norm(o, onorm_w.reshape(A_HEADS, A_DV))
    o = o.reshape(bn, t, A_WIDTH).astype(h.dtype) * jax.nn.silu(gate)
    return o @ w_out, s_last


def dilated_group_prompt(q, k, v, dil, n_keys):
    bn, s, h, hd = q.shape
    n = s // dil
    nb = -(-n // B_BLOCK)
    npad = nb * B_BLOCK

    def to_res(a):
        a = a.reshape(bn, n, dil, h, hd).transpose(0, 2, 1, 3, 4).reshape(bn * dil, n, h, hd)
        a = jnp.pad(a, ((0, 0), (0, npad - n), (0, 0), (0, 0)))
        return a.reshape(bn * dil, nb, B_BLOCK, h, hd)

    def with_prev(a):
        prev = jnp.pad(a[:, :-1], ((0, 0), (1, 0), (0, 0), (0, 0), (0, 0)))
        return jnp.concatenate([prev, a], axis=2)

    def from_res(a):
        rest = a.shape[3:]
        a = a.reshape((bn, dil, npad) + rest)[:, :, :n]
        return jnp.swapaxes(a, 1, 2).reshape((bn, s) + rest)

    qb = to_res(q)
    kk = with_prev(to_res(k))
    vv = with_prev(to_res(v))
    a_idx = jnp.arange(B_BLOCK)[:, None]
    c_idx = jnp.arange(2 * B_BLOCK)[None, :]
    rel = a_idx - c_idx + B_BLOCK
    blk = jnp.arange(nb)[:, None, None]
    valid = (rel >= 0) & (rel <= n_keys) & (blk * B_BLOCK + c_idx - B_BLOCK >= 0)
    sc = jnp.einsum('zjqhd,zjkhd->zjhqk', qb, kk).astype(jnp.float32) * (hd ** -0.5)
    sc = jnp.where(valid[None, :, None], sc, NEG_BIG)
    lse = jax.nn.logsumexp(sc, axis=-1)
    p = jnp.exp(sc - lse[..., None])
    o = jnp.einsum('zjhqk,zjkhd->zjqhd', p.astype(vv.dtype), vv).astype(jnp.float32)
    return from_res(o), from_res(jnp.swapaxes(lse, 2, 3))


def dilated_group_sample(q, k, v, buf, dil, n_keys, window):
    db, t, h, hd = q.shape
    lb = buf.shape[1]
    k_all = jnp.concatenate([buf[:, :, 0], k], axis=1)
    v_all = jnp.concatenate([buf[:, :, 1], v], axis=1)
    idx = lb + jnp.arange(t)[:, None] - dil * jnp.arange(n_keys + 1)[None, :]
    valid = idx >= 0
    flat = jnp.maximum(idx, 0).reshape(-1)
    kg = jnp.take(k_all, flat, axis=1).reshape(db, t, n_keys + 1, h, hd)
    vg = jnp.take(v_all, flat, axis=1).reshape(db, t, n_keys + 1, h, hd)
    sc = jnp.einsum('bthd,btkhd->bthk', q, kg).astype(jnp.float32) * (hd ** -0.5)
    sc = jnp.where(valid[None, :, None, :], sc, NEG_BIG)
    lse = jax.nn.logsumexp(sc, axis=-1)
    p = jnp.exp(sc - lse[..., None])
    o = jnp.einsum('bthk,btkhd->bthd', p.astype(vg.dtype), vg).astype(jnp.float32)
    keep = min(window, k_all.shape[1])
    new_buf = jnp.stack([k_all[:, -keep:], v_all[:, -keep:]], axis=2)
    return o, lse, new_buf


def dilated_mixer(h, w_in, w_out, pos, bufs):
    bn, t, _ = h.shape
    proj = h @ w_in
    outs, lses, new_bufs = [], [], []
    for gi, (window, dil) in enumerate(B_GROUPS):
        base = gi * 3 * B_WIDTH
        q, k, v = [proj[..., base + m * B_WIDTH: base + (m + 1) * B_WIDTH].reshape(bn, t, B_HEADS, B_HEAD_DIM)
                   for m in range(3)]
        q = apply_rope(q, pos)
        k = apply_rope(k, pos)
        n_keys = window // dil
        if bufs is None:
            o, lse = dilated_group_prompt(q, k, v, dil, n_keys)
            keep = min(window, t)
            nbuf = jnp.stack([k[:, t - keep:], v[:, t - keep:]], axis=2)
        else:
            o, lse, nbuf = dilated_group_sample(q, k, v, bufs[gi], dil, n_keys, window)
        outs.append(o)
        lses.append(lse)
        new_bufs.append(nbuf)
    wts = jax.nn.softmax(jnp.stack(lses, axis=0), axis=0)
    o = jnp.sum(wts[..., None] * jnp.stack(outs, axis=0), axis=0)
    gate = proj[..., 3 * len(B_GROUPS) * B_WIDTH:]
    o = o.reshape(bn, t, B_WIDTH).astype(h.dtype) * jax.nn.silu(gate)
    return o @ w_out, new_bufs


def s5_mixer(h, w_in, a_re, a_im, b_re, b_im, c_re, c_im, d_skip, log_dt, w_glu, b_glu, w_out, state0):
    bn, t, _ = h.shape
    f32 = jnp.float32
    proj = h @ w_in
    u = proj[..., :C_WIDTH].astype(f32)
    gate = proj[..., C_WIDTH:]
    ug = u.reshape(bn, t, C_GROUPS, C_GROUP_CH)
    lam_re = jnp.minimum(a_re.astype(f32), C_MAX_RE)
    lam_im = a_im.astype(f32)
    dt = jnp.exp(log_dt.astype(f32))[:, None]
    mag = jnp.exp(lam_re * dt)
    bar_re = mag * jnp.cos(lam_im * dt)
    bar_im = mag * jnp.sin(lam_im * dt)
    den = lam_re * lam_re + lam_im * lam_im
    xr = bar_re - 1.0
    coef_re = (xr * lam_re + bar_im * lam_im) / den
    coef_im = (bar_im * lam_re - xr * lam_im) / den
    br, bi = b_re.astype(f32), b_im.astype(f32)
    bbar_re = coef_re[..., None] * br - coef_im[..., None] * bi
    bbar_im = coef_re[..., None] * bi + coef_im[..., None] * br
    bu_re = jnp.einsum('btgc,gpc->btgp', ug, bbar_re)
    bu_im = jnp.einsum('btgc,gpc->btgp', ug, bbar_im)
    if state0 is not None:
        s_re = state0[..., 0].astype(f32)
        s_im = state0[..., 1].astype(f32)
        bu_re = bu_re.at[:, 0].add(bar_re * s_re - bar_im * s_im)
        bu_im = bu_im.at[:, 0].add(bar_re * s_im + bar_im * s_re)
    a_re_t = jnp.broadcast_to(bar_re, (1, t) + bar_re.shape)
    a_im_t = jnp.broadcast_to(bar_im, (1, t) + bar_im.shape)

    def combine(e1, e2):
        a1r, a1i, b1r, b1i = e1
        a2r, a2i, b2r, b2i = e2
        return (a2r * a1r - a2i * a1i, a2r * a1i + a2i * a1r,
                a2r * b1r - a2i * b1i + b2r, a2r * b1i + a2i * b1r + b2i)

    _, _, xs_re, xs_im = lax.associative_scan(combine, (a_re_t, a_im_t, bu_re, bu_im), axis=1)
    y = (jnp.einsum('gcp,btgp->btgc', c_re.astype(f32), xs_re)
         - jnp.einsum('gcp,btgp->btgc', c_im.astype(f32), xs_im))
    y = y.reshape(bn, t, C_WIDTH) + d_skip.astype(f32) * u
    y = jax.nn.gelu(y)
    y = y * jax.nn.sigmoid(y @ w_glu.astype(f32) + b_glu.astype(f32))
    y = y.astype(h.dtype) * jax.nn.silu(gate)
    new_state = jnp.stack([xs_re[:, -1], xs_im[:, -1]], axis=-1)
    return y @ w_out, new_state


def setup_inputs(seed: int = 0) -> dict:
    key = jax.random.key(seed)
    ks = iter(jax.random.split(key, 40))
    f32 = jnp.float32

    def nrm(shape, scale=1.0):
        return scale * jax.random.normal(next(ks), shape, f32)

    def buf_len(w):
        return min(w, PAST_LEN)

    kv_shape = lambda w: (N_B_LAYERS, DEC_BATCH, buf_len(w), 2, B_HEADS, B_HEAD_DIM)
    return {
        'x_prompt': nrm((BATCH, SEQ, D_MODEL)),
        'x_sample': nrm((DEC_BATCH, DEC_SEQ, D_MODEL)),
        'state_hgrn': nrm((N_A_LAYERS, DEC_BATCH, A_HEADS, A_DK, A_DV), 0.5),
        'cache_kv_w128': nrm(kv_shape(B_GROUPS[0][0])),
        'cache_kv_w512': nrm(kv_shape(B_GROUPS[1][0])),
        'cache_kv_w2048': nrm(kv_shape(B_GROUPS[2][0])),
        'state_s5': nrm((N_C_LAYERS, DEC_BATCH, C_GROUPS, C_STATE, 2), 0.2),
        'norm_w': 1.0 + nrm((DEPTH, D_MODEL), 0.02),
        'final_norm_w': 1.0 + nrm((D_MODEL,), 0.02),
        'a_w_in': nrm((N_A_LAYERS, D_MODEL, A_IN), D_MODEL ** -0.5),
        'a_lb_logits': nrm((N_A_LAYERS, A_HEADS * A_DK), 1.0),
        'a_onorm_w': 1.0 + nrm((N_A_LAYERS, A_WIDTH), 0.02),
        'a_w_out': nrm((N_A_LAYERS, A_WIDTH, D_MODEL), A_WIDTH ** -0.5),
        'b_w_in': nrm((N_B_LAYERS, D_MODEL, B_IN), D_MODEL ** -0.5),
        'b_w_out': nrm((N_B_LAYERS, B_WIDTH, D_MODEL), B_WIDTH ** -0.5),
        'c_w_in': nrm((N_C_LAYERS, D_MODEL, C_IN), D_MODEL ** -0.5),
        'c_a_re': -0.5 + nrm((N_C_LAYERS, C_GROUPS, C_STATE), 0.01),
        'c_a_im': jnp.pi * jnp.arange(C_STATE, dtype=f32) + nrm((N_C_LAYERS, C_GROUPS, C_STATE), 0.01),
        'c_b_re': nrm((N_C_LAYERS, C_GROUPS, C_STATE, C_GROUP_CH), (2.0 * C_GROUP_CH) ** -0.5),
        'c_b_im': nrm((N_C_LAYERS, C_GROUPS, C_STATE, C_GROUP_CH), (2.0 * C_GROUP_CH) ** -0.5),
        'c_c_re': nrm((N_C_LAYERS, C_GROUPS, C_GROUP_CH, C_STATE), (2.0 * C_STATE) ** -0.5),
        'c_c_im': nrm((N_C_LAYERS, C_GROUPS, C_GROUP_CH, C_STATE), (2.0 * C_STATE) ** -0.5),
        'c_d': nrm((N_C_LAYERS, C_WIDTH), 1.0),
        'c_log_dt': jax.random.uniform(next(ks), (N_C_LAYERS, C_GROUPS), f32, math.log(0.001), math.log(0.1)),
        'c_w_glu': nrm((N_C_LAYERS, C_WIDTH, C_WIDTH), C_WIDTH ** -0.5),
        'c_b_glu': nrm((N_C_LAYERS, C_WIDTH), 0.01),
        'c_w_out': nrm((N_C_LAYERS, C_WIDTH, D_MODEL), C_WIDTH ** -0.5),
    }


def reference(x_prompt, x_sample, state_hgrn, cache_kv_w128, cache_kv_w512, cache_kv_w2048, state_s5,
              norm_w, final_norm_w, a_w_in, a_lb_logits, a_onorm_w, a_w_out, b_w_in, b_w_out,
              c_w_in, c_a_re, c_a_im, c_b_re, c_b_im, c_c_re, c_c_im, c_d, c_log_dt, c_w_glu, c_b_glu, c_w_out):
    f32 = jnp.float32
    pos_p = jnp.arange(x_prompt.shape[1], dtype=f32)
    pos_s = PAST_LEN + jnp.arange(x_sample.shape[1], dtype=f32)
    p_lb = jax.nn.softmax(a_lb_logits.astype(f32), axis=0)
    lower_bounds = jnp.cumsum(p_lb, axis=0) - p_lb[0:1]
    caches = (cache_kv_w128, cache_kv_w512, cache_kv_w2048)
    xp, xs = x_prompt, x_sample
    hgrn_p, hgrn_s, s5_p, s5_s = [], [], [], []
    kv_p = [[] for _ in B_GROUPS]
    kv_s = [[] for _ in B_GROUPS]
    for layer in range(DEPTH):
        kind, j = layer % N_MIXERS, layer // N_MIXERS
        hp = rms_norm(xp, norm_w[layer])
        hs = rms_norm(xs, norm_w[layer])
        if kind == 0:
            wts = (a_w_in[j], lower_bounds[j], a_onorm_w[j], a_w_out[j])
            dp, st = hgrn2_mixer(hp, *wts, None)
            hgrn_p.append(st)
            ds, st = hgrn2_mixer(hs, *wts, state_hgrn[j])
            hgrn_s.append(st)
        elif kind == 1:
            dp, bufs = dilated_mixer(hp, b_w_in[j], b_w_out[j], pos_p, None)
            for g in range(len(B_GROUPS)):
                kv_p[g].append(bufs[g])
            ds, bufs = dilated_mixer(hs, b_w_in[j], b_w_out[j], pos_s, [cc[j] for cc in caches])
            for g in range(len(B_GROUPS)):
                kv_s[g].append(bufs[g])
        else:
            wts = (c_w_in[j], c_a_re[j], c_a_im[j], c_b_re[j], c_b_im[j], c_c_re[j], c_c_im[j],
                   c_d[j], c_log_dt[j], c_w_glu[j], c_b_glu[j], c_w_out[j])
            dp, st = s5_mixer(hp, *wts, None)
            s5_p.append(st)
            ds, st = s5_mixer(hs, *wts, state_s5[j])
            s5_s.append(st)
        xp = xp + dp
        xs = xs + ds
    y_prompt = rms_norm(xp, final_norm_w)
    y_sample = rms_norm(xs, final_norm_w)
    new_hgrn_prompt = jnp.stack(hgrn_p, axis=0)
    new_hgrn_sample = jnp.stack(hgrn_s, axis=0)
    new_kv128_prompt = jnp.stack(kv_p[0], axis=0)
    new_kv128_sample = jnp.stack(kv_s[0], axis=0)
    new_kv512_prompt = jnp.stack(kv_p[1], axis=0)
    new_kv512_sample = jnp.stack(kv_s[1], axis=0)
    new_kv2048_prompt = jnp.stack(kv_p[2], axis=0)
    new_kv2048_sample = jnp.stack(kv_s[2], axis=0)
    new_s5_prompt = jnp.stack(s5_p, axis=0)
    new_s5_sample = jnp.stack(s5_s, axis=0)
    return (y_prompt, y_sample, new_hgrn_prompt, new_hgrn_sample, new_kv128_prompt, new_kv128_sample,
            new_kv512_prompt, new_kv512_sample, new_kv2048_prompt, new_kv2048_sample,
            new_s5_prompt, new_s5_sample)
```

```python
import functools
import math

import numpy as np
import jax
import jax.numpy as jnp
from jax import lax
from jax.experimental import pallas as pl
from jax.experimental.pallas import tpu as pltpu

F32 = jnp.float32
BF16 = jnp.bfloat16

D_MODEL = 1024
N_HEADS = 8
HEAD_DIM = 128
NORM_EPS = 1e-6
NEG_BIG = -1e30
PAST_LEN = 8192
ROPE_THETA = 10000.0
A_EXP_CLIP = 60.0
ATTN_GROUPS = ((128, 1), (512, 4), (2048, 16))
ATTN_BLOCK = 128
S5_GROUPS = 64
S5_STATE = 64
S5_GROUP_CH = 16
S5_MAX_RE = -1e-4
S5_KTILES = 4
S5_NSTATE = S5_GROUPS * S5_STATE
SAMPLE_PAD = 8
VMEM_LIMIT_V7X = 56 * 1024 * 1024


def _cparams(*sem):
    return pltpu.CompilerParams(dimension_semantics=sem, vmem_limit_bytes=VMEM_LIMIT_V7X)


def _const_spec(shape):
    nd = len(shape)
    return pl.BlockSpec(shape, lambda *_: (0,) * nd, pipeline_mode=pl.Buffered(1))


def _dot(a, b):
    return jnp.dot(a, b, preferred_element_type=F32)


def _dot_nt(a, b):
    return lax.dot_general(a, b, (((1,), (1,)), ((), ())), preferred_element_type=F32)


def _dot_tn(a, b):
    return lax.dot_general(a, b, (((0,), (0,)), ((), ())), preferred_element_type=F32)


def _rms(x, w):
    ms = jnp.mean(x * x, axis=-1, keepdims=True)
    return x * lax.rsqrt(ms + NORM_EPS) * w


def _silu(x):
    return x * jax.nn.sigmoid(x)


def _hgrn_gates(zf, lb):
    ez = jnp.exp(-jnp.abs(zf))
    log_sig = jnp.minimum(zf, 0.0) - jnp.log1p(ez)
    log_f = log_sig + jnp.log1p(lb * jnp.exp(jnp.minimum(-zf, A_EXP_CLIP)))
    r = 1.0 / (1.0 + ez)
    key = (1.0 - lb) * jnp.where(zf >= 0.0, ez * r, r)
    return log_f, key


def _hgrn_decay_matrix(c):
    levels = int(math.log2(c))
    t = np.arange(c)[:, None]
    u = np.arange(c)[None, :]
    blocks = [u <= t, u > t]
    for lev in range(1, levels + 1):
        size = 2 ** lev
        half = size // 2
        mid = (t // size) * size + half
        second = (t % size) >= half
        blocks.append(np.where(second, (u >= mid) & (u <= t), (u > t) & (u <= mid - 1)))
    return np.concatenate(blocks, axis=0).astype(np.float32)


def _hgrn_level_matrix(c):
    t = np.arange(c)[:, None]
    s = np.arange(c)[None, :]
    x = t ^ s
    lev = np.zeros((c, c), np.int32)
    nz = x > 0
    lev[nz] = np.floor(np.log2(x[nz])).astype(np.int32) + 1
    return np.where(s <= t, lev, -1).astype(np.int32)


def _hgrn_prompt_kernel(x_ref, nw_ref, win_ref, lb_ref, onw_ref, wout_ref, fnw_ref, dmat_ref, lvl_ref,
                        xo_ref, st_ref,
                        q_s, k_s, v_s, g_s, gate_s, o_s, st_s, *, chunk, final_norm):
    ti = pl.program_id(1)
    tb = x_ref.shape[0]
    levels = int(math.log2(chunk))

    @pl.when(ti == 0)
    def _():
        st_s[...] = jnp.zeros_like(st_s)

    x = x_ref[...]
    h = _rms(x, nw_ref[...]).astype(BF16)
    q_s[...] = _silu(_dot(h, win_ref[:, 0:D_MODEL]))
    log_f, key = _hgrn_gates(_dot(h, win_ref[:, D_MODEL:2 * D_MODEL]), lb_ref[...])
    g_s[...] = log_f
    k_s[...] = key
    v_s[...] = _dot(h, win_ref[:, 2 * D_MODEL:3 * D_MODEL])
    gate_s[...] = _silu(_dot(h, win_ref[:, 3 * D_MODEL:4 * D_MODEL]))

    lvl = lvl_ref[...]
    row = lax.broadcasted_iota(jnp.int32, (chunk, HEAD_DIM), 0)

    def chunk_body(ci, carry):
        r0 = pl.multiple_of(ci * chunk, chunk)
        rows = pl.ds(r0, chunk)
        for hh in range(N_HEADS):
            cs = slice(hh * HEAD_DIM, (hh + 1) * HEAD_DIM)
            qh = q_s[rows, cs]
            kh = k_s[rows, cs]
            vh = v_s[rows, cs].astype(BF16)
            gh = g_s[rows, cs]
            g_hi = gh.astype(BF16)
            g_lo = (gh - g_hi.astype(F32)).astype(BF16)
            ex = _dot(dmat_ref[...], jnp.concatenate([g_hi, g_lo], axis=1))
            ex = ex[:, :HEAD_DIM] + ex[:, HEAD_DIM:]
            b_read = ex[0:chunk]
            b_write = ex[chunk:2 * chunk]
            st = st_s[hh]
            o = _dot_nt((qh * jnp.exp(b_read)).astype(BF16), st.astype(BF16))
            k_dec = (kh * jnp.exp(b_write)).astype(BF16)
            st_s[hh] = st * jnp.exp(b_read[chunk - 1:chunk, :]) + _dot_tn(vh, k_dec)
            p = jnp.where(lvl == 0, _dot_nt(qh.astype(BF16), kh.astype(BF16)), 0.0)
            for lev in range(1, levels + 1):
                e_l = jnp.exp(ex[(lev + 1) * chunk:(lev + 2) * chunk])
                second = ((row >> (lev - 1)) & 1) == 1
                a_l = (jnp.where(second, qh, kh) * e_l).astype(BF16)
                p = jnp.where(lvl == lev, _dot_nt(a_l, a_l), p)
            o = o + _dot(p.astype(BF16), vh)
            ms = jnp.mean(o * o, axis=-1, keepdims=True)
            o_s[rows, cs] = o * lax.rsqrt(ms + NORM_EPS) * onw_ref[:, cs]
        return carry

    lax.fori_loop(0, tb // chunk, chunk_body, 0)

    y = (o_s[...] * gate_s[...]).astype(BF16)
    xo = x + _dot(y, wout_ref[...])
    if final_norm:
        xo = _rms(xo, fnw_ref[...])
    xo_ref[...] = xo

    @pl.when(ti == pl.num_programs(1) - 1)
    def _():
        for hh in range(N_HEADS):
            st_ref[hh] = st_s[hh].T


def _hgrn_prompt(x, nw, win, lb, onw, wout, fnw, *, batch, seq, time_major_in, final_norm, tb=256, chunk=128):
    if time_major_in:
        x_spec = pl.BlockSpec((tb, D_MODEL), lambda b, t: (t, b))
    else:
        x_spec = pl.BlockSpec((None, tb, D_MODEL), lambda b, t: (b, t, 0))
    dmat = jnp.asarray(_hgrn_decay_matrix(chunk), BF16)
    lvl = jnp.asarray(_hgrn_level_matrix(chunk))
    row_scratch = pltpu.VMEM((tb, D_MODEL), F32)
    return pl.pallas_call(
        functools.partial(_hgrn_prompt_kernel, chunk=chunk, final_norm=final_norm),
        grid=(batch, seq // tb),
        in_specs=[x_spec, _const_spec((1, D_MODEL)), _const_spec((D_MODEL, 4 * D_MODEL)),
                  _const_spec((1, D_MODEL)), _const_spec((1, D_MODEL)), _const_spec((D_MODEL, D_MODEL)),
                  _const_spec((1, D_MODEL)), _const_spec(dmat.shape), _const_spec(lvl.shape)],
        out_specs=[pl.BlockSpec((None, tb, D_MODEL), lambda b, t: (b, t, 0)),
                   pl.BlockSpec((None, N_HEADS, HEAD_DIM, HEAD_DIM), lambda b, t: (b, 0, 0, 0))],
        out_shape=[jax.ShapeDtypeStruct((batch, seq, D_MODEL), F32),
                   jax.ShapeDtypeStruct((batch, N_HEADS, HEAD_DIM, HEAD_DIM), F32)],
        scratch_shapes=[row_scratch] * 6 + [pltpu.VMEM((N_HEADS, HEAD_DIM, HEAD_DIM), F32)],
        compiler_params=_cparams("arbitrary", "arbitrary"),
        name="hgrn_prompt",
    )(x, nw, win, lb, onw, wout, fnw, dmat, lvl)


def _hgrn_sample_kernel(x_ref, s0_ref, nw_ref, win_ref, lb_ref, onw_ref, wout_ref, fnw_ref,
                        xo_ref, s_ref, q_s, k_s, v_s, f_s, o_s, *, n_tok, final_norm):
    nb = s0_ref.shape[0]
    x = x_ref[...]
    h = _rms(x, nw_ref[...]).astype(BF16)
    q_s[...] = _silu(_dot(h, win_ref[:, 0:D_MODEL]))
    log_f, key = _hgrn_gates(_dot(h, win_ref[:, D_MODEL:2 * D_MODEL]), lb_ref[...])
    f_s[...] = jnp.exp(log_f)
    k_s[...] = key
    v_s[...] = _dot(h, win_ref[:, 2 * D_MODEL:3 * D_MODEL])
    gate = _silu(_dot(h, win_ref[:, 3 * D_MODEL:4 * D_MODEL]))
    row = lax.broadcasted_iota(jnp.int32, (SAMPLE_PAD, HEAD_DIM), 0)

    def seq_body(bi, carry):
        rows = pl.ds(pl.multiple_of(bi * SAMPLE_PAD, SAMPLE_PAD), SAMPLE_PAD)
        for hh in range(N_HEADS):
            cs = slice(hh * HEAD_DIM, (hh + 1) * HEAD_DIM)
            q_t = q_s[rows, cs].T
            k_t = k_s[rows, cs].T
            f_t = f_s[rows, cs].T
            vh = v_s[rows, cs]
            st = s0_ref[bi, hh]
            o = jnp.zeros((SAMPLE_PAD, HEAD_DIM), F32)
            for t in range(n_tok):
                st = st * f_t[:, t:t + 1] + k_t[:, t:t + 1] * vh[t:t + 1, :]
                o_t = jnp.sum(st * q_t[:, t:t + 1], axis=0, keepdims=True)
                o = jnp.where(row == t, o_t, o)
            s_ref[bi, hh] = st
            ms = jnp.mean(o * o, axis=-1, keepdims=True)
            o_s[rows, cs] = o * lax.rsqrt(ms + NORM_EPS) * onw_ref[:, cs]
        return carry

    lax.fori_loop(0, nb, seq_body, 0)
    y = (o_s[...] * gate).astype(BF16)
    xo = x + _dot(y, wout_ref[...])
    if final_norm:
        xo = _rms(xo, fnw_ref[...])
    xo_ref[...] = xo


def _hgrn_sample(x, s0, nw, win, lb, onw, wout, fnw, *, n_tok, final_norm, nb=8):
    n_seq = s0.shape[0]
    rows = nb * SAMPLE_PAD
    row_scratch = pltpu.VMEM((rows, D_MODEL), F32)
    st_spec = pl.BlockSpec((nb, N_HEADS, HEAD_DIM, HEAD_DIM), lambda i: (i, 0, 0, 0))
    return pl.pallas_call(
        functools.partial(_hgrn_sample_kernel, n_tok=n_tok, final_norm=final_norm),
        grid=(n_seq // nb,),
        in_specs=[pl.BlockSpec((rows, D_MODEL), lambda i: (i, 0)), st_spec,
                  _const_spec((1, D_MODEL)), _const_spec((D_MODEL, 4 * D_MODEL)), _const_spec((1, D_MODEL)),
                  _const_spec((1, D_MODEL)), _const_spec((D_MODEL, D_MODEL)), _const_spec((1, D_MODEL))],
        out_specs=[pl.BlockSpec((rows, D_MODEL), lambda i: (i, 0)), st_spec],
        out_shape=[jax.ShapeDtypeStruct(x.shape, F32), jax.ShapeDtypeStruct(s0.shape, F32)],
        scratch_shapes=[row_scratch] * 5,
        compiler_params=_cparams("arbitrary"),
        name="hgrn_sample",
    )(x, s0, nw, win, lb, onw, wout, fnw)


def _attn_proj_kernel(x_ref, cos_ref, sin_ref, nw_ref, win_ref, qkv_ref, gate_ref, kv0_ref, kv1_ref, kv2_ref):
    tb = x_ref.shape[0]
    h = _rms(x_ref[...], nw_ref[...]).astype(BF16)
    cos = cos_ref[...]
    sin = sin_ref[...]

    def rope(a):
        cols = []
        for hh in range(N_HEADS):
            ah = a[:, hh * HEAD_DIM:(hh + 1) * HEAD_DIM]
            cols.append(ah * cos + pltpu.roll(ah, HEAD_DIM // 2, 1) * sin)
        return jnp.concatenate(cols, axis=1)

    for gi, kv_ref in enumerate((kv0_ref, kv1_ref, kv2_ref)):
        base = 3 * gi * D_MODEL
        q = rope(_dot(h, win_ref[:, base:base + D_MODEL])) * (HEAD_DIM ** -0.5)
        k = rope(_dot(h, win_ref[:, base + D_MODEL:base + 2 * D_MODEL]))
        v = _dot(h, win_ref[:, base + 2 * D_MODEL:base + 3 * D_MODEL])
        qkv_ref[:, base:base + D_MODEL] = q.astype(BF16)
        qkv_ref[:, base + D_MODEL:base + 2 * D_MODEL] = k.astype(BF16)
        qkv_ref[:, base + 2 * D_MODEL:base + 3 * D_MODEL] = v.astype(BF16)
        keep = kv_ref.shape[0]
        kv_ref[:, 0:D_MODEL] = k[tb - keep:, :]
        kv_ref[:, D_MODEL:2 * D_MODEL] = v[tb - keep:, :]
    gate_ref[...] = _silu(_dot(h, win_ref[:, 9 * D_MODEL:10 * D_MODEL]))


def _attn_proj(x, cos, sin, nw, win, *, batch, seq, keeps, tb):
    nt = seq // tb
    kv_specs, kv_shapes = [], []
    for keep in keeps:
        kb = min(tb, keep)
        first = (seq - keep) // tb if keep >= tb else 0
        kv_specs.append(pl.BlockSpec((None, kb, 2 * D_MODEL),
                                     lambda b, t, first=first, kb=kb: (b, jnp.maximum(t - first, 0) if kb == tb else 0, 0)))
        kv_shapes.append(jax.ShapeDtypeStruct((batch, keep, 2 * D_MODEL), F32))
    return pl.pallas_call(
        _attn_proj_kernel,
        grid=(batch, nt),
        in_specs=[pl.BlockSpec((None, tb, D_MODEL), lambda b, t: (b, t, 0)),
                  pl.BlockSpec((tb, HEAD_DIM), lambda b, t: (t, 0)),
                  pl.BlockSpec((tb, HEAD_DIM), lambda b, t: (t, 0)),
                  _const_spec((1, D_MODEL)), _const_spec((D_MODEL, 10 * D_MODEL))],
        out_specs=[pl.BlockSpec((None, tb, 9 * D_MODEL), lambda b, t: (b, t, 0)),
                   pl.BlockSpec((None, tb, D_MODEL), lambda b, t: (b, t, 0))] + kv_specs,
        out_shape=[jax.ShapeDtypeStruct((batch, seq, 9 * D_MODEL), BF16),
                   jax.ShapeDtypeStruct((batch, seq, D_MODEL), F32)] + kv_shapes,
        compiler_params=_cparams("arbitrary", "arbitrary"),
        name="attn_proj",
    )(x, cos, sin, nw, win)


def _attn_group_kernel(q_ref, kc_ref, kp_ref, vc_ref, vp_ref, o_ref, lse_ref):
    j = pl.program_id(2)
    a = lax.broadcasted_iota(jnp.int32, (ATTN_BLOCK, ATTN_BLOCK), 0)
    c = lax.broadcasted_iota(jnp.int32, (ATTN_BLOCK, ATTN_BLOCK), 1)
    mask_cur = c <= a
    mask_prev = (c >= a) & (j > 0)
    lane = lax.broadcasted_iota(jnp.int32, (ATTN_BLOCK, HEAD_DIM), 1)
    lse_tile = jnp.zeros((ATTN_BLOCK, HEAD_DIM), F32)
    for hh in range(N_HEADS):
        cs = slice(hh * HEAD_DIM, (hh + 1) * HEAD_DIM)
        qh = q_ref[:, cs]
        s_c = jnp.where(mask_cur, _dot_nt(qh, kc_ref[:, cs]), NEG_BIG)
        s_p = jnp.where(mask_prev, _dot_nt(qh, kp_ref[:, cs]), NEG_BIG)
        m = jnp.maximum(jnp.max(s_c, axis=1, keepdims=True), jnp.max(s_p, axis=1, keepdims=True))
        p_c = jnp.exp(s_c - m)
        p_p = jnp.exp(s_p - m)
        den = jnp.sum(p_c, axis=1, keepdims=True) + jnp.sum(p_p, axis=1, keepdims=True)
        o = _dot(p_c.astype(BF16), vc_ref[:, cs]) + _dot(p_p.astype(BF16), vp_ref[:, cs])
        o_ref[:, cs] = o / den
        lse_tile = jnp.where(lane == hh, m + jnp.log(den), lse_tile)
    lse_ref[...] = lse_tile


def _attn_group(qkv, gi, *, batch, seq):
    dil = ATTN_GROUPS[gi][1]
    n = seq // dil
    qkv_r = qkv.reshape(batch, n, dil * 9 * D_MODEL)

    def spec(col, prev):
        if prev:
            return pl.BlockSpec((None, ATTN_BLOCK, D_MODEL),
                                lambda b, r, j: (b, jnp.maximum(j - 1, 0), r * 9 + 3 * gi + col))
        return pl.BlockSpec((None, ATTN_BLOCK, D_MODEL), lambda b, r, j: (b, j, r * 9 + 3 * gi + col))

    o, lse = pl.pallas_call(
        _attn_group_kernel,
        grid=(batch, dil, n // ATTN_BLOCK),
        in_specs=[spec(0, False), spec(1, False), spec(1, True), spec(2, False), spec(2, True)],
        out_specs=[pl.BlockSpec((None, ATTN_BLOCK, D_MODEL), lambda b, r, j: (b, j, r)),
                   pl.BlockSpec((None, ATTN_BLOCK, HEAD_DIM), lambda b, r, j: (b, j, r))],
        out_shape=[jax.ShapeDtypeStruct((batch, n, dil * D_MODEL), F32),
                   jax.ShapeDtypeStruct((batch, n, dil * HEAD_DIM), F32)],
        compiler_params=_cparams("arbitrary", "arbitrary", "arbitrary"),
        name=f"attn_group{gi}",
    )(qkv_r, qkv_r, qkv_r, qkv_r, qkv_r)
    return o.reshape(batch, seq, D_MODEL), lse.reshape(batch, seq, HEAD_DIM)


def _merge_heads(outs, lses):
    m = functools.reduce(jnp.maximum, lses)
    es = [jnp.exp(l - m) for l in lses]
    den = functools.reduce(jnp.add, es)
    ws = [e / den for e in es]
    cols = []
    for hh in range(N_HEADS):
        cs = slice(hh * HEAD_DIM, (hh + 1) * HEAD_DIM)
        cols.append(functools.reduce(jnp.add, [w[:, hh:hh + 1] * o[:, cs] for w, o in zip(ws, outs)]))
    return jnp.concatenate(cols, axis=1)


def _attn_out_kernel(o0_ref, o1_ref, o2_ref, l0_ref, l1_ref, l2_ref, gate_ref, x_ref, wout_ref, xo_ref):
    o = _merge_heads([o0_ref[...], o1_ref[...], o2_ref[...]], [l0_ref[...], l1_ref[...], l2_ref[...]])
    y = (o * gate_ref[...]).astype(BF16)
    xo_ref[...] = x_ref[...] + _dot(y, wout_ref[...])


def _attn_out(outs, lses, gate, x, wout, *, batch, seq, tb):
    row = lambda w: pl.BlockSpec((None, tb, w), lambda b, t: (b, t, 0))
    return pl.pallas_call(
        _attn_out_kernel,
        grid=(batch, seq // tb),
        in_specs=[row(D_MODEL)] * 3 + [row(HEAD_DIM)] * 3 + [row(D_MODEL), row(D_MODEL),
                                                            _const_spec((D_MODEL, D_MODEL))],
        out_specs=pl.BlockSpec((tb, D_MODEL), lambda b, t: (t, b)),
        out_shape=jax.ShapeDtypeStruct((seq, batch * D_MODEL), F32),
        compiler_params=_cparams("arbitrary", "arbitrary"),
        name="attn_out",
    )(*outs, *lses, gate, x, wout)


def _attn_sample_kernel(qkv_ref, kvn_ref, c0_ref, c1_ref, c2_ref, gate_ref, x_ref, wout_ref, xo_ref, o_s,
                        *, n_tok):
    nb = c0_ref.shape[0]
    row_k = lax.broadcasted_iota(jnp.int32, (ATTN_BLOCK, 1), 0)
    row_t = lax.broadcasted_iota(jnp.int32, (SAMPLE_PAD, HEAD_DIM), 0)
    lane = lax.broadcasted_iota(jnp.int32, (SAMPLE_PAD, HEAD_DIM), 1)
    cache_refs = (c0_ref, c1_ref, c2_ref)

    def seq_body(bi, carry):
        rows = pl.ds(pl.multiple_of(bi * SAMPLE_PAD, SAMPLE_PAD), SAMPLE_PAD)
        outs, lses = [], []
        for gi, (_, dil) in enumerate(ATTN_GROUPS):
            cref = cache_refs[gi]
            width = 2 * D_MODEL
            o_cols = []
            lse_tile = jnp.zeros((SAMPLE_PAD, HEAD_DIM), F32)
            for hh in range(N_HEADS):
                cs = slice(hh * HEAD_DIM, (hh + 1) * HEAD_DIM)
                qh = qkv_ref[rows, 3 * gi * D_MODEL + hh * HEAD_DIM:3 * gi * D_MODEL + (hh + 1) * HEAD_DIM].astype(F32)
                k_new = kvn_ref[rows, gi * width + hh * HEAD_DIM:gi * width + (hh + 1) * HEAD_DIM]
                v_new = kvn_ref[rows, gi * width + D_MODEL + hh * HEAD_DIM:gi * width + D_MODEL + (hh + 1) * HEAD_DIM]
                o_h = jnp.zeros((SAMPLE_PAD, HEAD_DIM), F32)
                lse_h = jnp.zeros((SAMPLE_PAD, 1), F32)
                for t in range(n_tok):
                    res = t % dil
                    k_c = cref[bi, :, res * width + hh * HEAD_DIM:res * width + (hh + 1) * HEAD_DIM]
                    v_c = cref[bi, :, res * width + D_MODEL + hh * HEAD_DIM:res * width + D_MODEL + (hh + 1) * HEAD_DIM]
                    q_t = qh[t:t + 1, :]
                    s_c = jnp.sum(k_c * q_t, axis=1, keepdims=True)
                    s_c = jnp.where(row_k * dil + res >= t, s_c, NEG_BIG)
                    new_t = [u for u in range(t + 1) if (t - u) % dil == 0]
                    s_n = [jnp.sum(k_new[u:u + 1, :] * q_t, axis=1, keepdims=True) for u in new_t]
                    m = functools.reduce(jnp.maximum, s_n + [jnp.max(s_c, axis=0, keepdims=True)])
                    p_c = jnp.exp(s_c - m)
                    p_n = [jnp.exp(s - m) for s in s_n]
                    den = functools.reduce(jnp.add, p_n + [jnp.sum(p_c, axis=0, keepdims=True)])
                    acc = jnp.sum(p_c * v_c, axis=0, keepdims=True)
                    for u, p in zip(new_t, p_n):
                        acc = acc + p * v_new[u:u + 1, :]
                    o_h = jnp.where(row_t == t, acc / den, o_h)
                    lse_h = jnp.where(row_t[:, 0:1] == t, m + jnp.log(den), lse_h)
                o_cols.append(o_h)
                lse_tile = jnp.where(lane == hh, lse_h, lse_tile)
            outs.append(jnp.concatenate(o_cols, axis=1))
            lses.append(lse_tile)
        o_s[rows, :] = _merge_heads(outs, lses)
        return carry

    lax.fori_loop(0, nb, seq_body, 0)
    y = (o_s[...] * gate_ref[...]).astype(BF16)
    xo_ref[...] = x_ref[...] + _dot(y, wout_ref[...])


def _attn_sample(qkv, kvn, caches, gate, x, wout, *, n_tok, nb=2):
    n_seq = caches[0].shape[0]
    rows = nb * SAMPLE_PAD
    c_views, c_specs = [], []
    for (window, dil), cache in zip(ATTN_GROUPS, caches):
        assert cache.shape[1] == window == ATTN_BLOCK * dil and (n_tok <= dil or dil == 1)
        c_views.append(cache.reshape(n_seq, ATTN_BLOCK, dil * 2 * D_MODEL))
        n_res = min(dil, n_tok)
        c_specs.append(pl.BlockSpec((nb, ATTN_BLOCK, n_res * 2 * D_MODEL), lambda i: (i, 0, 0)))
    row = lambda w: pl.BlockSpec((rows, w), lambda i: (i, 0))
    return pl.pallas_call(
        functools.partial(_attn_sample_kernel, n_tok=n_tok),
        grid=(n_seq // nb,),
        in_specs=[row(9 * D_MODEL), row(6 * D_MODEL)] + c_specs + [row(D_MODEL), row(D_MODEL),
                                                                  _const_spec((D_MODEL, D_MODEL))],
        out_specs=row(D_MODEL),
        out_shape=jax.ShapeDtypeStruct(x.shape, F32),
        scratch_shapes=[pltpu.VMEM((rows, D_MODEL), F32)],
        compiler_params=_cparams("arbitrary"),
        name="attn_sample",
    )(qkv, kvn, *c_views, gate, x, wout)


def _cache_roll_kernel(c0, c1, c2, n0, n1, n2, o0, o1, o2, sem, *, n_tok):
    copies = []
    for i, (c, n, o) in enumerate(((c0, n0, o0), (c1, n1, o1), (c2, n2, o2))):
        length = c.shape[1]
        copies.append(pltpu.make_async_copy(c.at[:, pl.ds(n_tok, length - n_tok)],
                                            o.at[:, pl.ds(0, length - n_tok)], sem.at[2 * i]))
        copies.append(pltpu.make_async_copy(n, o.at[:, pl.ds(length - n_tok, n_tok)], sem.at[2 * i + 1]))
    for cp in copies:
        cp.start()
    for cp in copies:
        cp.wait()


def _cache_roll(caches, news, *, n_tok):
    any_spec = pl.BlockSpec(memory_space=pl.ANY)
    return pl.pallas_call(
        functools.partial(_cache_roll_kernel, n_tok=n_tok),
        in_specs=[any_spec] * 6,
        out_specs=[any_spec] * 3,
        out_shape=[jax.ShapeDtypeStruct(c.shape, c.dtype) for c in caches],
        scratch_shapes=[pltpu.SemaphoreType.DMA((6,))],
        name="cache_roll",
    )(*caches, *news)


def _gelu_tanh(y):
    return 0.5 * y * (1.0 + jnp.tanh(math.sqrt(2.0 / math.pi) * (y + 0.044715 * (y * y * y))))


def _s5_kernel(x_ref, s0_ref, nw_ref, win_ref, wb_ref, are_ref, aim_ref, wc_ref, dsk_ref, wglu_ref, bglu_ref,
               wout_ref, xo_ref, sfin_ref, bu_s, st_s, *, n_seq, tb):
    ti = pl.program_id(0)
    tile = 2 * S5_NSTATE // S5_KTILES
    half = tile // 2
    lanes = 512

    @pl.when(ti == 0)
    def _():
        st_s[...] = s0_ref[...]

    x = x_ref[...]
    h = _rms(x, nw_ref[...]).astype(BF16)
    u = _dot(h, win_ref[:, 0:D_MODEL])
    gate = _silu(_dot(h, win_ref[:, D_MODEL:2 * D_MODEL]))
    ub = u.astype(BF16)
    kw = D_MODEL // S5_KTILES
    for kt in range(S5_KTILES):
        bu_s[:, kt * tile:(kt + 1) * tile] = _dot(ub[:, kt * kw:(kt + 1) * kw], wb_ref[kt])

    for sg in range(n_seq // 8):
        srow = slice(sg * 8, (sg + 1) * 8)
        for kt in range(S5_KTILES):
            for part in range(half // lanes):
                c_re = slice(kt * tile + part * lanes, kt * tile + (part + 1) * lanes)
                c_im = slice(kt * tile + half + part * lanes, kt * tile + half + (part + 1) * lanes)
                a_re = are_ref[:, c_re]
                a_im = aim_ref[:, c_re]

                def step(t, carry):
                    s_re, s_im = carry
                    rows = pl.ds(pl.multiple_of(t * n_seq + sg * 8, 8), 8)
                    n_re = a_re * s_re - a_im * s_im + bu_s[rows, c_re]
                    n_im = a_re * s_im + a_im * s_re + bu_s[rows, c_im]
                    bu_s[rows, c_re] = n_re
                    bu_s[rows, c_im] = n_im
                    return n_re, n_im

                s_re, s_im = lax.fori_loop(0, tb, step, (st_s[srow, c_re], st_s[srow, c_im]))
                st_s[srow, c_re] = s_re
                st_s[srow, c_im] = s_im

    ys = []
    for kt in range(S5_KTILES):
        ys.append(_dot(bu_s[:, kt * tile:(kt + 1) * tile].astype(BF16), wc_ref[kt]))
    y = jnp.concatenate(ys, axis=1) + dsk_ref[...] * u
    y = _gelu_tanh(y)
    y = y * jax.nn.sigmoid(_dot(y.astype(BF16), wglu_ref[...]) + bglu_ref[...])
    y = (y * gate).astype(BF16)
    xo_ref[...] = x + _dot(y, wout_ref[...])

    @pl.when(ti == pl.num_programs(0) - 1)
    def _():
        sfin_ref[...] = st_s[...]


def _s5_layer(x, s0, nw, win, wb, a_re, a_im, wc, dsk, wglu, bglu, wout, *, n_seq, seq, tb):
    rows = tb * n_seq
    ncol = 2 * S5_NSTATE
    return pl.pallas_call(
        functools.partial(_s5_kernel, n_seq=n_seq, tb=tb),
        grid=(seq // tb,),
        in_specs=[pl.BlockSpec((rows, D_MODEL), lambda t: (t, 0)), _const_spec((n_seq, ncol)),
                  _const_spec((1, D_MODEL)), _const_spec((D_MODEL, 2 * D_MODEL)), _const_spec(wb.shape),
                  _const_spec((8, ncol)), _const_spec((8, ncol)), _const_spec(wc.shape),
                  _const_spec((1, D_MODEL)), _const_spec((D_MODEL, D_MODEL)), _const_spec((1, D_MODEL)),
                  _const_spec((D_MODEL, D_MODEL))],
        out_specs=[pl.BlockSpec((rows, D_MODEL), lambda t: (t, 0)),
                   pl.BlockSpec((n_seq, ncol), lambda t: (0, 0))],
        out_shape=[jax.ShapeDtypeStruct(x.shape, F32), jax.ShapeDtypeStruct((n_seq, ncol), F32)],
        scratch_shapes=[pltpu.VMEM((rows, ncol), F32), pltpu.VMEM((n_seq, ncol), F32)],
        compiler_params=_cparams("arbitrary"),
        name="s5_layer",
    )(x, s0, nw, win, wb, a_re, a_im, wc, dsk, wglu, bglu, wout)


def _s5_params(a_re, a_im, b_re, b_im, c_re, c_im, log_dt):
    lam_re = jnp.minimum(a_re, S5_MAX_RE)
    lam_im = a_im
    dt = jnp.exp(log_dt)[:, None]
    mag = jnp.exp(lam_re * dt)
    bar_re = mag * jnp.cos(lam_im * dt)
    bar_im = mag * jnp.sin(lam_im * dt)
    den = lam_re * lam_re + lam_im * lam_im
    xr = bar_re - 1.0
    coef_re = (xr * lam_re + bar_im * lam_im) / den
    coef_im = (bar_im * lam_re - xr * lam_im) / den
    bbar_re = coef_re[..., None] * b_re - coef_im[..., None] * b_im
    bbar_im = coef_re[..., None] * b_im + coef_im[..., None] * b_re
    gl = S5_GROUPS // S5_KTILES
    eye = jnp.eye(gl, dtype=F32)

    def to_cols(a):
        return a.reshape(S5_KTILES, gl * S5_STATE)

    def b_tile(bb):
        bb = bb.reshape(S5_KTILES, gl, S5_STATE, S5_GROUP_CH)
        return jnp.einsum('kgpc,gh->kgchp', bb, eye).reshape(S5_KTILES, gl * S5_GROUP_CH, gl * S5_STATE)

    def c_tile(cc):
        cc = cc.reshape(S5_KTILES, gl, S5_GROUP_CH, S5_STATE)
        return jnp.einsum('kgcp,gh->kgphc', cc, eye).reshape(S5_KTILES, gl * S5_STATE, gl * S5_GROUP_CH)

    wb = jnp.concatenate([b_tile(bbar_re), b_tile(bbar_im)], axis=2).astype(BF16)
    wc = jnp.concatenate([c_tile(c_re), -c_tile(c_im)], axis=1).astype(BF16)
    cols = lambda a: jnp.concatenate([to_cols(a), to_cols(a)], axis=1).reshape(1, -1)
    a_re_cols = jnp.broadcast_to(cols(bar_re), (8, 2 * S5_NSTATE))
    a_im_cols = jnp.broadcast_to(cols(bar_im), (8, 2 * S5_NSTATE))
    return wb, wc, a_re_cols, a_im_cols


def _s5_state_to_cols(s):
    n = s.shape[0]
    gl = S5_GROUPS // S5_KTILES
    s = s.reshape(n, S5_KTILES, gl * S5_STATE, 2)
    return jnp.moveaxis(s, 3, 2).reshape(n, 2 * S5_NSTATE)


def _s5_cols_to_state(c):
    n = c.shape[0]
    gl = S5_GROUPS // S5_KTILES
    c = c.reshape(n, S5_KTILES, 2, gl * S5_STATE)
    return jnp.moveaxis(c, 2, 3).reshape(n, S5_GROUPS, S5_STATE, 2)


def _rope_tables(pos):
    half = HEAD_DIM // 2
    inv_freq = ROPE_THETA ** (-jnp.arange(half, dtype=F32) / half)
    ang = pos[:, None] * inv_freq[None, :]
    cos, sin = jnp.cos(ang), jnp.sin(ang)
    return jnp.concatenate([cos, cos], axis=1), jnp.concatenate([-sin, sin], axis=1)


def kernel(x_prompt, x_sample, state_hgrn, cache_kv_w128, cache_kv_w512, cache_kv_w2048, state_s5,
           norm_w, final_norm_w, a_w_in, a_lb_logits, a_onorm_w, a_w_out, b_w_in, b_w_out,
           c_w_in, c_a_re, c_a_im, c_b_re, c_b_im, c_c_re, c_c_im, c_d, c_log_dt, c_w_glu, c_b_glu, c_w_out):
    batch, seq, _ = x_prompt.shape
    n_seq, n_tok, _ = x_sample.shape
    depth = norm_w.shape[0]
    caches = (cache_kv_w128, cache_kv_w512, cache_kv_w2048)
    row = lambda a: a.reshape(1, -1)

    p_lb = jax.nn.softmax(a_lb_logits.astype(F32), axis=0)
    lower_bounds = jnp.cumsum(p_lb, axis=0) - p_lb[0:1]
    fnw = row(final_norm_w)

    xs = jnp.pad(x_sample, ((0, 0), (0, SAMPLE_PAD - n_tok), (0, 0))).reshape(n_seq * SAMPLE_PAD, D_MODEL)
    xp = x_prompt
    xp_time_major = False
    pos_p = jnp.arange(seq, dtype=F32)
    pos_s = jnp.tile(jnp.pad(PAST_LEN + jnp.arange(n_tok, dtype=F32), (0, SAMPLE_PAD - n_tok)), n_seq)

    hgrn_p, hgrn_s, s5_p, s5_s = [], [], [], []
    kv_p = [[] for _ in ATTN_GROUPS]
    kv_s = [[] for _ in ATTN_GROUPS]
    for layer in range(depth):
        kind, j = layer % 3, layer // 3
        last = layer == depth - 1
        nw = row(norm_w[layer])
        if kind == 0:
            win, wout = a_w_in[j].astype(BF16), a_w_out[j].astype(BF16)
            lb, onw = row(lower_bounds[j]), row(a_onorm_w[j])
            xp, st = _hgrn_prompt(xp, nw, win, lb, onw, wout, fnw, batch=batch, seq=seq,
                                  time_major_in=xp_time_major, final_norm=last)
            xp_time_major = False
            hgrn_p.append(st)
            xs, st = _hgrn_sample(xs, state_hgrn[j], nw, win, lb, onw, wout, fnw, n_tok=n_tok, final_norm=last)
            hgrn_s.append(st)
        elif kind == 1:
            assert not xp_time_major
            win, wout = b_w_in[j].astype(BF16), b_w_out[j].astype(BF16)
            cos, sin = _rope_tables(pos_p)
            qkv, gate, *kvs = _attn_proj(xp, cos, sin, nw, win, batch=batch, seq=seq,
                                         keeps=[min(w, seq) for w, _ in ATTN_GROUPS], tb=256)
            for g, kv in enumerate(kvs):
                kv_p[g].append(kv.reshape(batch, kv.shape[1], 2, N_HEADS, HEAD_DIM))
            outs, lses = zip(*[_attn_group(qkv, g, batch=batch, seq=seq) for g in range(len(ATTN_GROUPS))])
            xp = _attn_out(outs, lses, gate, xp, wout, batch=batch, seq=seq, tb=256)
            xp_time_major = True

            cos, sin = _rope_tables(pos_s)
            rows = n_seq * SAMPLE_PAD
            qkv, gate, *kvs = _attn_proj(xs.reshape(1, rows, D_MODEL), cos, sin, nw, win, batch=1, seq=rows,
                                         keeps=[rows] * len(ATTN_GROUPS), tb=rows)
            kvn = jnp.concatenate([kv.reshape(rows, 2 * D_MODEL) for kv in kvs], axis=1)
            layer_caches = [c[j] for c in caches]
            xs = _attn_sample(qkv.reshape(rows, 9 * D_MODEL), kvn, layer_caches, gate.reshape(rows, D_MODEL),
                              xs, wout, n_tok=n_tok)
            news = [kv.reshape(n_seq, SAMPLE_PAD, 2, N_HEADS, HEAD_DIM)[:, :n_tok] for kv in kvs]
            for g, nb in enumerate(_cache_roll(layer_caches, news, n_tok=n_tok)):
                kv_s[g].append(nb)
        else:
            wb, wc, a_re_cols, a_im_cols = _s5_params(c_a_re[j], c_a_im[j], c_b_re[j], c_b_im[j],
                                                      c_c_re[j], c_c_im[j], c_log_dt[j])
            wts = (nw, c_w_in[j].astype(BF16), wb, a_re_cols, a_im_cols, wc, row(c_d[j]),
                   c_w_glu[j].astype(BF16), row(c_b_glu[j]), c_w_out[j].astype(BF16))
            if not xp_time_major:
                xp = jnp.swapaxes(xp, 0, 1).reshape(seq, batch * D_MODEL)
                xp_time_major = True
            xp2, sfin = _s5_layer(xp.reshape(seq * batch, D_MODEL), jnp.zeros((batch, 2 * S5_NSTATE), F32), *wts,
                                  n_seq=batch, seq=seq, tb=32)
            xp = xp2.reshape(seq, batch * D_MODEL)
            s5_p.append(_s5_cols_to_state(sfin))
            xs_tm = jnp.swapaxes(xs.reshape(n_seq, SAMPLE_PAD, D_MODEL)[:, :n_tok], 0, 1)
            xs_tm, sfin = _s5_layer(xs_tm.reshape(n_tok * n_seq, D_MODEL), _s5_state_to_cols(state_s5[j]), *wts,
                                    n_seq=n_seq, seq=n_tok, tb=n_tok)
            s5_s.append(_s5_cols_to_state(sfin))
            xs = jnp.pad(jnp.swapaxes(xs_tm.reshape(n_tok, n_seq, D_MODEL), 0, 1),
                         ((0, 0), (0, SAMPLE_PAD - n_tok), (0, 0))).reshape(n_seq * SAMPLE_PAD, D_MODEL)

    if depth % 3 != 1:
        raise NotImplementedError("final norm is fused into the last HGRN2 layer")
    if xp_time_major:
        xp = jnp.swapaxes(xp.reshape(seq, batch, D_MODEL), 0, 1)
    y_prompt = xp
    y_sample = xs.reshape(n_seq, SAMPLE_PAD, D_MODEL)[:, :n_tok]
    stack = lambda parts: jnp.stack(parts, axis=0)
    return (y_prompt, y_sample, stack(hgrn_p), stack(hgrn_s),
            stack(kv_p[0]), stack(kv_s[0]), stack(kv_p[1]), stack(kv_s[1]), stack(kv_p[2]), stack(kv_s[2]),
            stack(s5_p), stack(s5_s))
```

```python
import functools
import math

import numpy as np
import jax
import jax.numpy as jnp
from jax import lax
from jax.experimental import pallas as pl
from jax.experimental.pallas import tpu as pltpu

F32 = jnp.float32
BF16 = jnp.bfloat16

D_MODEL = 1024
N_HEADS = 8
HEAD_DIM = 128
NORM_EPS = 1e-6
NEG_BIG = -1e30
PAST_LEN = 8192
ROPE_THETA = 10000.0
A_EXP_CLIP = 60.0
ATTN_GROUPS = ((128, 1), (512, 4), (2048, 16))
ATTN_BLOCK = 128
S5_GROUPS = 64
S5_STATE = 64
S5_GROUP_CH = 16
S5_MAX_RE = -1e-4
S5_KTILES = 4
S5_NSTATE = S5_GROUPS * S5_STATE
SAMPLE_PAD = 8
KV_ROWS = 2 * N_HEADS
VMEM_LIMIT_V7X = 56 * 1024 * 1024


def _cparams(*sem):
    return pltpu.CompilerParams(dimension_semantics=sem, vmem_limit_bytes=VMEM_LIMIT_V7X)


def _const_spec(shape):
    nd = len(shape)
    return pl.BlockSpec(shape, lambda *_: (0,) * nd, pipeline_mode=pl.Buffered(1))


def _dot(a, b):
    return jnp.dot(a, b, preferred_element_type=F32)


def _dot_nt(a, b):
    return lax.dot_general(a, b, (((1,), (1,)), ((), ())), preferred_element_type=F32)


def _dot_tn(a, b):
    return lax.dot_general(a, b, (((0,), (0,)), ((), ())), preferred_element_type=F32)


def _rms(x, w):
    ms = jnp.mean(x * x, axis=-1, keepdims=True)
    return x * lax.rsqrt(ms + NORM_EPS) * w


def _silu(x):
    return x * jax.nn.sigmoid(x)


def _hgrn_gates(zf, lb):
    ez = jnp.exp(-jnp.abs(zf))
    log_sig = jnp.minimum(zf, 0.0) - jnp.log1p(ez)
    log_f = log_sig + jnp.log1p(lb * jnp.exp(jnp.minimum(-zf, A_EXP_CLIP)))
    r = 1.0 / (1.0 + ez)
    key = (1.0 - lb) * jnp.where(zf >= 0.0, ez * r, r)
    return log_f, key


def _hgrn_decay_matrix(c):
    levels = int(math.log2(c))
    t = np.arange(c)[:, None]
    u = np.arange(c)[None, :]
    blocks = [u <= t, u > t]
    for lev in range(1, levels + 1):
        size = 2 ** lev
        half = size // 2
        mid = (t // size) * size + half
        second = (t % size) >= half
        blocks.append(np.where(second, (u >= mid) & (u <= t), (u > t) & (u <= mid - 1)))
    return np.concatenate(blocks, axis=0).astype(np.float32)


def _hgrn_level_matrix(c):
    t = np.arange(c)[:, None]
    s = np.arange(c)[None, :]
    x = t ^ s
    lev = np.zeros((c, c), np.int32)
    nz = x > 0
    lev[nz] = np.floor(np.log2(x[nz])).astype(np.int32) + 1
    return np.where(s <= t, lev, -1).astype(np.int32)


def _hgrn_prompt_kernel(x_ref, nw_ref, win_ref, lb_ref, onw_ref, wout_ref, fnw_ref, dmat_ref, lvl_ref,
                        xo_ref, st_ref,
                        q_s, k_s, v_s, g_s, gate_s, o_s, st_s, *, chunk, final_norm):
    ti = pl.program_id(1)
    tb = x_ref.shape[0]
    levels = int(math.log2(chunk))

    @pl.when(ti == 0)
    def _():
        st_s[...] = jnp.zeros_like(st_s)

    x = x_ref[...]
    h = _rms(x, nw_ref[...]).astype(BF16)
    q_s[...] = _silu(_dot(h, win_ref[:, 0:D_MODEL]))
    log_f, key = _hgrn_gates(_dot(h, win_ref[:, D_MODEL:2 * D_MODEL]), lb_ref[...])
    g_s[...] = log_f
    k_s[...] = key
    v_s[...] = _dot(h, win_ref[:, 2 * D_MODEL:3 * D_MODEL])
    gate_s[...] = _silu(_dot(h, win_ref[:, 3 * D_MODEL:4 * D_MODEL]))

    lvl = lvl_ref[...]
    row = lax.broadcasted_iota(jnp.int32, (chunk, HEAD_DIM), 0)

    def chunk_body(ci, carry):
        r0 = pl.multiple_of(ci * chunk, chunk)
        rows = pl.ds(r0, chunk)
        for hh in range(N_HEADS):
            cs = slice(hh * HEAD_DIM, (hh + 1) * HEAD_DIM)
            qh = q_s[rows, cs]
            kh = k_s[rows, cs]
            vh = v_s[rows, cs].astype(BF16)
            gh = g_s[rows, cs]
            g_hi = gh.astype(BF16)
            g_lo = (gh - g_hi.astype(F32)).astype(BF16)
            ex = _dot(dmat_ref[...], jnp.concatenate([g_hi, g_lo], axis=1))
            ex = ex[:, :HEAD_DIM] + ex[:, HEAD_DIM:]
            b_read = ex[0:chunk]
            b_write = ex[chunk:2 * chunk]
            st = st_s[hh]
            o = _dot_nt((qh * jnp.exp(b_read)).astype(BF16), st.astype(BF16))
            k_dec = (kh * jnp.exp(b_write)).astype(BF16)
            st_s[hh] = st * jnp.exp(b_read[chunk - 1:chunk, :]) + _dot_tn(vh, k_dec)
            p = jnp.where(lvl == 0, _dot_nt(qh.astype(BF16), kh.astype(BF16)), 0.0)
            for lev in range(1, levels + 1):
                e_l = jnp.exp(ex[(lev + 1) * chunk:(lev + 2) * chunk])
                second = ((row >> (lev - 1)) & 1) == 1
                a_l = (jnp.where(second, qh, kh) * e_l).astype(BF16)
                p = jnp.where(lvl == lev, _dot_nt(a_l, a_l), p)
            o = o + _dot(p.astype(BF16), vh)
            ms = jnp.mean(o * o, axis=-1, keepdims=True)
            o_s[rows, cs] = o * lax.rsqrt(ms + NORM_EPS) * onw_ref[:, cs]
        return carry

    lax.fori_loop(0, tb // chunk, chunk_body, 0)

    y = (o_s[...] * gate_s[...]).astype(BF16)
    xo = x + _dot(y, wout_ref[...])
    if final_norm:
        xo = _rms(xo, fnw_ref[...])
    xo_ref[...] = xo

    @pl.when(ti == pl.num_programs(1) - 1)
    def _():
        for hh in range(N_HEADS):
            st_ref[hh] = st_s[hh].T


def _hgrn_prompt(x, nw, win, lb, onw, wout, fnw, *, final_norm, tb=256, chunk=128):
    batch, seq, _ = x.shape
    dmat = jnp.asarray(_hgrn_decay_matrix(chunk), BF16)
    lvl = jnp.asarray(_hgrn_level_matrix(chunk))
    row_scratch = pltpu.VMEM((tb, D_MODEL), F32)
    x_spec = pl.BlockSpec((None, tb, D_MODEL), lambda b, t: (b, t, 0))
    return pl.pallas_call(
        functools.partial(_hgrn_prompt_kernel, chunk=chunk, final_norm=final_norm),
        grid=(batch, seq // tb),
        in_specs=[x_spec, _const_spec((1, D_MODEL)), _const_spec((D_MODEL, 4 * D_MODEL)),
                  _const_spec((1, D_MODEL)), _const_spec((1, D_MODEL)), _const_spec((D_MODEL, D_MODEL)),
                  _const_spec((1, D_MODEL)), _const_spec(dmat.shape), _const_spec(lvl.shape)],
        out_specs=[x_spec, pl.BlockSpec((None, N_HEADS, HEAD_DIM, HEAD_DIM), lambda b, t: (b, 0, 0, 0))],
        out_shape=[jax.ShapeDtypeStruct((batch, seq, D_MODEL), F32),
                   jax.ShapeDtypeStruct((batch, N_HEADS, HEAD_DIM, HEAD_DIM), F32)],
        scratch_shapes=[row_scratch] * 6 + [pltpu.VMEM((N_HEADS, HEAD_DIM, HEAD_DIM), F32)],
        compiler_params=_cparams("arbitrary", "arbitrary"),
        name="hgrn_prompt",
    )(x, nw, win, lb, onw, wout, fnw, dmat, lvl)


def _hgrn_sample_kernel(x_ref, s0_ref, nw_ref, win_ref, lb_ref, onw_ref, wout_ref, fnw_ref,
                        xo_ref, s_ref, q_s, k_s, v_s, f_s, o_s, *, n_tok, final_norm):
    nb = s0_ref.shape[0]
    x = x_ref[...]
    h = _rms(x, nw_ref[...]).astype(BF16)
    q_s[...] = _silu(_dot(h, win_ref[:, 0:D_MODEL]))
    log_f, key = _hgrn_gates(_dot(h, win_ref[:, D_MODEL:2 * D_MODEL]), lb_ref[...])
    f_s[...] = jnp.exp(log_f)
    k_s[...] = key
    v_s[...] = _dot(h, win_ref[:, 2 * D_MODEL:3 * D_MODEL])
    gate = _silu(_dot(h, win_ref[:, 3 * D_MODEL:4 * D_MODEL]))
    row = lax.broadcasted_iota(jnp.int32, (SAMPLE_PAD, HEAD_DIM), 0)

    def seq_body(bi, carry):
        rows = pl.ds(pl.multiple_of(bi * SAMPLE_PAD, SAMPLE_PAD), SAMPLE_PAD)
        for hh in range(N_HEADS):
            cs = slice(hh * HEAD_DIM, (hh + 1) * HEAD_DIM)
            q_t = q_s[rows, cs].T
            k_t = k_s[rows, cs].T
            f_t = f_s[rows, cs].T
            vh = v_s[rows, cs]
            st = s0_ref[bi, hh]
            o = jnp.zeros((SAMPLE_PAD, HEAD_DIM), F32)
            for t in range(n_tok):
                st = st * f_t[:, t:t + 1] + k_t[:, t:t + 1] * vh[t:t + 1, :]
                o_t = jnp.sum(st * q_t[:, t:t + 1], axis=0, keepdims=True)
                o = jnp.where(row == t, o_t, o)
            s_ref[bi, hh] = st
            ms = jnp.mean(o * o, axis=-1, keepdims=True)
            o_s[rows, cs] = o * lax.rsqrt(ms + NORM_EPS) * onw_ref[:, cs]
        return carry

    lax.fori_loop(0, nb, seq_body, 0)
    y = (o_s[...] * gate).astype(BF16)
    xo = x + _dot(y, wout_ref[...])
    if final_norm:
        xo = _rms(xo, fnw_ref[...])
    xo_ref[...] = xo


def _hgrn_sample(x, s0, layer, nw, win, lb, onw, wout, fnw, *, n_tok, final_norm, nb=8):
    n_seq = s0.shape[1]
    rows = nb * SAMPLE_PAD
    row_scratch = pltpu.VMEM((rows, D_MODEL), F32)
    return pl.pallas_call(
        functools.partial(_hgrn_sample_kernel, n_tok=n_tok, final_norm=final_norm),
        grid=(n_seq // nb,),
        in_specs=[pl.BlockSpec((rows, D_MODEL), lambda i: (i, 0)),
                  pl.BlockSpec((None, nb, N_HEADS, HEAD_DIM, HEAD_DIM), lambda i: (layer, i, 0, 0, 0)),
                  _const_spec((1, D_MODEL)), _const_spec((D_MODEL, 4 * D_MODEL)), _const_spec((1, D_MODEL)),
                  _const_spec((1, D_MODEL)), _const_spec((D_MODEL, D_MODEL)), _const_spec((1, D_MODEL))],
        out_specs=[pl.BlockSpec((rows, D_MODEL), lambda i: (i, 0)),
                   pl.BlockSpec((nb, N_HEADS, HEAD_DIM, HEAD_DIM), lambda i: (i, 0, 0, 0))],
        out_shape=[jax.ShapeDtypeStruct(x.shape, F32), jax.ShapeDtypeStruct(s0.shape[1:], F32)],
        scratch_shapes=[row_scratch] * 5,
        compiler_params=_cparams("arbitrary"),
        name="hgrn_sample",
    )(x, s0, nw, win, lb, onw, wout, fnw)


def _attn_proj_kernel(x_ref, cos_ref, sin_ref, nw_ref, win_ref,
                      g0_ref, g1_ref, g2_ref, gate_ref, kv0_ref, kv1_ref, kv2_ref):
    tb = x_ref.shape[0]
    h = _rms(x_ref[...], nw_ref[...]).astype(BF16)
    cos = cos_ref[...]
    sin = sin_ref[...]

    def rope(a):
        cols = []
        for hh in range(N_HEADS):
            ah = a[:, hh * HEAD_DIM:(hh + 1) * HEAD_DIM]
            cols.append(ah * cos + pltpu.roll(ah, HEAD_DIM // 2, 1) * sin)
        return jnp.concatenate(cols, axis=1)

    for gi, (qkv_ref, kv_ref) in enumerate(((g0_ref, kv0_ref), (g1_ref, kv1_ref), (g2_ref, kv2_ref))):
        base = 3 * gi * D_MODEL
        dil = qkv_ref.shape[0]
        q = rope(_dot(h, win_ref[:, base:base + D_MODEL])) * (HEAD_DIM ** -0.5)
        k = rope(_dot(h, win_ref[:, base + D_MODEL:base + 2 * D_MODEL]))
        v = _dot(h, win_ref[:, base + 2 * D_MODEL:base + 3 * D_MODEL])
        for ci, val in enumerate((q, k, v)):
            if dil == 1:
                val = val.reshape(1, tb, D_MODEL)
            else:
                val = pltpu.einshape("(id)c->dic", val, d=dil)
            qkv_ref[:, :, ci * D_MODEL:(ci + 1) * D_MODEL] = val.astype(BF16)
        keep = kv_ref.shape[0]
        kv_ref[:, 0] = k[tb - keep:, :].reshape(keep, N_HEADS, HEAD_DIM)
        kv_ref[:, 1] = v[tb - keep:, :].reshape(keep, N_HEADS, HEAD_DIM)
    gate_ref[...] = _silu(_dot(h, win_ref[:, 9 * D_MODEL:10 * D_MODEL]))


def _attn_proj(x, cos, sin, nw, win, *, dils, keeps, tb):
    batch, seq, _ = x.shape
    qkv_specs, qkv_shapes, kv_specs, kv_shapes = [], [], [], []
    for dil, keep in zip(dils, keeps):
        qkv_specs.append(pl.BlockSpec((None, dil, tb // dil, 3 * D_MODEL), lambda b, t: (b, 0, t, 0)))
        qkv_shapes.append(jax.ShapeDtypeStruct((batch, dil, seq // dil, 3 * D_MODEL), BF16))
        kb = min(tb, keep)
        first = (seq - keep) // tb if keep >= tb else 0
        kv_specs.append(pl.BlockSpec(
            (None, kb, 2, N_HEADS, HEAD_DIM),
            lambda b, t, first=first, kb=kb: (b, jnp.maximum(t - first, 0) if kb == tb else 0, 0, 0, 0)))
        kv_shapes.append(jax.ShapeDtypeStruct((batch, keep, 2, N_HEADS, HEAD_DIM), F32))
    return pl.pallas_call(
        _attn_proj_kernel,
        grid=(batch, seq // tb),
        in_specs=[pl.BlockSpec((None, tb, D_MODEL), lambda b, t: (b, t, 0)),
                  pl.BlockSpec((tb, HEAD_DIM), lambda b, t: (t, 0)),
                  pl.BlockSpec((tb, HEAD_DIM), lambda b, t: (t, 0)),
                  _const_spec((1, D_MODEL)), _const_spec((D_MODEL, 10 * D_MODEL))],
        out_specs=qkv_specs + [pl.BlockSpec((None, tb, D_MODEL), lambda b, t: (b, t, 0))] + kv_specs,
        out_shape=qkv_shapes + [jax.ShapeDtypeStruct((batch, seq, D_MODEL), F32)] + kv_shapes,
        compiler_params=_cparams("arbitrary", "arbitrary"),
        name="attn_proj",
    )(x, cos, sin, nw, win)


def _attn_group_kernel(q_ref, kc_ref, kp_ref, vc_ref, vp_ref, o_ref, lse_ref):
    j = pl.program_id(2)
    a = lax.broadcasted_iota(jnp.int32, (ATTN_BLOCK, ATTN_BLOCK), 0)
    c = lax.broadcasted_iota(jnp.int32, (ATTN_BLOCK, ATTN_BLOCK), 1)
    mask_cur = c <= a
    mask_prev = (c >= a) & (j > 0)
    lane = lax.broadcasted_iota(jnp.int32, (ATTN_BLOCK, HEAD_DIM), 1)
    lse_tile = jnp.zeros((ATTN_BLOCK, HEAD_DIM), F32)
    for hh in range(N_HEADS):
        cs = slice(hh * HEAD_DIM, (hh + 1) * HEAD_DIM)
        qh = q_ref[:, cs]
        s_c = jnp.where(mask_cur, _dot_nt(qh, kc_ref[:, cs]), NEG_BIG)
        s_p = jnp.where(mask_prev, _dot_nt(qh, kp_ref[:, cs]), NEG_BIG)
        m = jnp.maximum(jnp.max(s_c, axis=1, keepdims=True), jnp.max(s_p, axis=1, keepdims=True))
        p_c = jnp.exp(s_c - m)
        p_p = jnp.exp(s_p - m)
        den = jnp.sum(p_c, axis=1, keepdims=True) + jnp.sum(p_p, axis=1, keepdims=True)
        o = _dot(p_c.astype(BF16), vc_ref[:, cs]) + _dot(p_p.astype(BF16), vp_ref[:, cs])
        o_ref[:, cs] = o / den
        lse_tile = jnp.where(lane == hh, m + jnp.log(den), lse_tile)
    lse_ref[...] = lse_tile


def _attn_group(qkv):
    batch, dil, n, _ = qkv.shape

    def spec(col, prev):
        if prev:
            return pl.BlockSpec((None, None, ATTN_BLOCK, D_MODEL), lambda b, r, j: (b, r, jnp.maximum(j - 1, 0), col))
        return pl.BlockSpec((None, None, ATTN_BLOCK, D_MODEL), lambda b, r, j: (b, r, j, col))

    return pl.pallas_call(
        _attn_group_kernel,
        grid=(batch, dil, n // ATTN_BLOCK),
        in_specs=[spec(0, False), spec(1, False), spec(1, True), spec(2, False), spec(2, True)],
        out_specs=[pl.BlockSpec((None, None, ATTN_BLOCK, D_MODEL), lambda b, r, j: (b, r, j, 0)),
                   pl.BlockSpec((None, None, ATTN_BLOCK, HEAD_DIM), lambda b, r, j: (b, r, j, 0))],
        out_shape=[jax.ShapeDtypeStruct((batch, dil, n, D_MODEL), F32),
                   jax.ShapeDtypeStruct((batch, dil, n, HEAD_DIM), F32)],
        compiler_params=_cparams("arbitrary", "arbitrary", "arbitrary"),
        name=f"attn_group_d{dil}",
    )(qkv, qkv, qkv, qkv, qkv)


def _merge_heads(outs, lses):
    m = functools.reduce(jnp.maximum, lses)
    es = [jnp.exp(l - m) for l in lses]
    den = functools.reduce(jnp.add, es)
    ws = [e / den for e in es]
    cols = []
    for hh in range(N_HEADS):
        cs = slice(hh * HEAD_DIM, (hh + 1) * HEAD_DIM)
        cols.append(functools.reduce(jnp.add, [w[:, hh:hh + 1] * o[:, cs] for w, o in zip(ws, outs)]))
    return jnp.concatenate(cols, axis=1)


def _attn_out_kernel(o0_ref, o1_ref, o2_ref, l0_ref, l1_ref, l2_ref, gate_ref, x_ref, wout_ref, xo_ref):
    def natural(ref):
        return ref[0] if ref.shape[0] == 1 else pltpu.einshape("dic->(id)c", ref[...])

    o = _merge_heads([natural(r) for r in (o0_ref, o1_ref, o2_ref)], [natural(r) for r in (l0_ref, l1_ref, l2_ref)])
    y = (o * gate_ref[...]).astype(BF16)
    xo_ref[...] = x_ref[...] + _dot(y, wout_ref[...])


def _attn_out(outs, lses, gate, x, wout, *, tb):
    batch, seq, _ = x.shape
    regrouped = lambda a: pl.BlockSpec((None, a.shape[1], tb // a.shape[1], a.shape[3]), lambda b, t: (b, 0, t, 0))
    row = pl.BlockSpec((None, tb, D_MODEL), lambda b, t: (b, t, 0))
    return pl.pallas_call(
        _attn_out_kernel,
        grid=(batch, seq // tb),
        in_specs=[regrouped(a) for a in outs] + [regrouped(a) for a in lses] + [row, row, _const_spec((D_MODEL, D_MODEL))],
        out_specs=row,
        out_shape=jax.ShapeDtypeStruct((batch, seq, D_MODEL), F32),
        compiler_params=_cparams("arbitrary", "arbitrary"),
        name="attn_out",
    )(*outs, *lses, gate, x, wout)


def _attn_sample_kernel(q0_ref, q1_ref, q2_ref, n0_ref, n1_ref, n2_ref, c0_ref, c1_ref, c2_ref,
                        gate_ref, x_ref, wout_ref, xo_ref, o_s, *, n_tok):
    nb = c0_ref.shape[0]
    pos_i = lax.broadcasted_iota(jnp.int32, (ATTN_BLOCK, N_HEADS, 1), 0)
    row_t = lax.broadcasted_iota(jnp.int32, (SAMPLE_PAD, D_MODEL), 0)
    q_refs, new_refs, cache_refs = (q0_ref, q1_ref, q2_ref), (n0_ref, n1_ref, n2_ref), (c0_ref, c1_ref, c2_ref)

    def seq_body(bi, carry):
        r0 = pl.multiple_of(bi * SAMPLE_PAD, SAMPLE_PAD)
        q_rows = [q_ref[pl.ds(r0, SAMPLE_PAD), :].astype(F32) for q_ref in q_refs]
        o_tile = jnp.zeros((SAMPLE_PAD, D_MODEL), F32)
        for t in range(n_tok):
            outs, lses = [], []
            for gi, (_, dil) in enumerate(ATTN_GROUPS):
                q_t = jnp.concatenate([q_rows[gi][t:t + 1, hh * HEAD_DIM:(hh + 1) * HEAD_DIM]
                                       for hh in range(N_HEADS)], axis=0)
                res = t % dil
                k_c = cache_refs[gi][bi, :, res, 0]
                v_c = cache_refs[gi][bi, :, res, 1]
                s_c = jnp.sum(k_c * q_t[None], axis=-1, keepdims=True)
                s_c = jnp.where(pos_i * dil + res >= t, s_c, NEG_BIG)
                new_t = [u for u in range(t + 1) if (t - u) % dil == 0]
                k_n = [new_refs[gi][r0 + u, 0] for u in new_t]
                v_n = [new_refs[gi][r0 + u, 1] for u in new_t]
                s_n = [jnp.sum(kk * q_t, axis=-1, keepdims=True) for kk in k_n]
                m = functools.reduce(jnp.maximum, s_n + [jnp.max(s_c, axis=0)])
                p_c = jnp.exp(s_c - m[None])
                p_n = [jnp.exp(s - m) for s in s_n]
                den = functools.reduce(jnp.add, p_n + [jnp.sum(p_c, axis=0)])
                acc = functools.reduce(jnp.add, [p * vv for p, vv in zip(p_n, v_n)] + [jnp.sum(p_c * v_c, axis=0)])
                outs.append(acc / den)
                lses.append(m + jnp.log(den))
            m_g = functools.reduce(jnp.maximum, lses)
            e_g = [jnp.exp(l - m_g) for l in lses]
            den_g = functools.reduce(jnp.add, e_g)
            o_t = functools.reduce(jnp.add, [e / den_g * o for e, o in zip(e_g, outs)])
            o_row = jnp.concatenate([o_t[hh:hh + 1, :] for hh in range(N_HEADS)], axis=1)
            o_tile = jnp.where(row_t == t, o_row, o_tile)
        o_s[pl.ds(r0, SAMPLE_PAD), :] = o_tile
        return carry

    lax.fori_loop(0, nb, seq_body, 0)
    y = (o_s[...] * gate_ref[...]).astype(BF16)
    xo_ref[...] = x_ref[...] + _dot(y, wout_ref[...])


def _attn_sample(qkvs, news, caches, layer, gate, x, wout, *, n_tok, nb=2):
    n_seq = caches[0].shape[1]
    rows = nb * SAMPLE_PAD
    c_views, c_specs = [], []
    for (window, dil), cache in zip(ATTN_GROUPS, caches):
        assert cache.shape[2] == window == ATTN_BLOCK * dil and (n_tok <= dil or dil == 1)
        c_views.append(cache.reshape(cache.shape[0], n_seq, ATTN_BLOCK, dil, 2, N_HEADS, HEAD_DIM))
        n_res = min(dil, n_tok)
        c_specs.append(pl.BlockSpec((None, nb, ATTN_BLOCK, n_res, 2, N_HEADS, HEAD_DIM),
                                    lambda i: (layer, i, 0, 0, 0, 0, 0)))
    row = lambda w: pl.BlockSpec((rows, w), lambda i: (i, 0))
    new_spec = pl.BlockSpec((rows, 2, N_HEADS, HEAD_DIM), lambda i: (i, 0, 0, 0))
    return pl.pallas_call(
        functools.partial(_attn_sample_kernel, n_tok=n_tok),
        grid=(n_seq // nb,),
        in_specs=[row(D_MODEL)] * 3 + [new_spec] * 3 + c_specs + [row(D_MODEL), row(D_MODEL),
                                                                  _const_spec((D_MODEL, D_MODEL))],
        out_specs=row(D_MODEL),
        out_shape=jax.ShapeDtypeStruct(x.shape, F32),
        scratch_shapes=[pltpu.VMEM((rows, D_MODEL), F32)],
        compiler_params=_cparams("arbitrary"),
        name="attn_sample",
    )(*qkvs, *news, *c_views, gate, x, wout)


def _cache_roll_copies(c_refs, n_refs, o_refs, sem, *, layer, n_tok):
    copies = []
    shift = n_tok * KV_ROWS
    for gi, (c, n, o) in enumerate(zip(c_refs, n_refs, o_refs)):
        n_seq, rows, _ = o.shape
        for b in range(n_seq):
            copies.append(pltpu.make_async_copy(c.at[layer, b, pl.ds(shift, rows - shift)],
                                                o.at[b, pl.ds(0, rows - shift)], sem.at[gi, b]))
        copies.append(pltpu.make_async_copy(n, o.at[:, pl.ds(rows - shift, shift)], sem.at[gi, n_seq]))
    return copies


def _cache_roll_kernel(c0, c1, c2, n0, n1, n2, o0, o1, o2, sem, *, layer, n_tok):
    copies = _cache_roll_copies((c0, c1, c2), (n0, n1, n2), (o0, o1, o2), sem, layer=layer, n_tok=n_tok)
    for cp in copies:
        cp.start()
    for cp in copies:
        cp.wait()


def _cache_roll(caches, news, layer, *, n_tok):
    n_seq = caches[0].shape[1]
    c_rows = [c.reshape(c.shape[0], n_seq, c.shape[2] * KV_ROWS, HEAD_DIM) for c in caches]
    n_rows = [n.reshape(n_seq, n_tok * KV_ROWS, HEAD_DIM) for n in news]
    any_spec = pl.BlockSpec(memory_space=pl.ANY)
    outs = pl.pallas_call(
        functools.partial(_cache_roll_kernel, layer=layer, n_tok=n_tok),
        in_specs=[any_spec] * 6,
        out_specs=[any_spec] * 3,
        out_shape=[jax.ShapeDtypeStruct(c.shape[1:], c.dtype) for c in c_rows],
        scratch_shapes=[pltpu.SemaphoreType.DMA((len(caches), n_seq + 1))],
        name="cache_roll",
    )(*c_rows, *n_rows)
    return [o.reshape(c.shape[1:]) for o, c in zip(outs, caches)]


def _gelu_tanh(y):
    return 0.5 * y * (1.0 + jnp.tanh(math.sqrt(2.0 / math.pi) * (y + 0.044715 * (y * y * y))))


def _s5_kernel(x_ref, s0_ref, nw_ref, win_ref, wb_ref, are_ref, aim_ref, wc_ref, dsk_ref, wglu_ref, bglu_ref,
               wout_ref, xo_ref, sfin_ref, bu_s, st_s, *, n_seq, tb, seq_major_io):
    ti = pl.program_id(0)
    tile = 2 * S5_NSTATE // S5_KTILES
    half = tile // 2
    lanes = 512

    @pl.when(ti == 0)
    def _():
        st_s[...] = s0_ref[...]

    if seq_major_io:
        x = jnp.swapaxes(x_ref[...], 0, 1).reshape(tb * n_seq, D_MODEL)
    else:
        x = x_ref[...]
    h = _rms(x, nw_ref[...]).astype(BF16)
    u = _dot(h, win_ref[:, 0:D_MODEL])
    gate = _silu(_dot(h, win_ref[:, D_MODEL:2 * D_MODEL]))
    ub = u.astype(BF16)
    kw = D_MODEL // S5_KTILES
    for kt in range(S5_KTILES):
        bu_s[:, kt * tile:(kt + 1) * tile] = _dot(ub[:, kt * kw:(kt + 1) * kw], wb_ref[kt])

    for sg in range(n_seq // 8):
        srow = slice(sg * 8, (sg + 1) * 8)
        for kt in range(S5_KTILES):
            for part in range(half // lanes):
                c_re = slice(kt * tile + part * lanes, kt * tile + (part + 1) * lanes)
                c_im = slice(kt * tile + half + part * lanes, kt * tile + half + (part + 1) * lanes)
                a_re = are_ref[:, c_re]
                a_im = aim_ref[:, c_re]

                def step(t, carry):
                    s_re, s_im = carry
                    rows = pl.ds(pl.multiple_of(t * n_seq + sg * 8, 8), 8)
                    n_re = a_re * s_re - a_im * s_im + bu_s[rows, c_re]
                    n_im = a_re * s_im + a_im * s_re + bu_s[rows, c_im]
                    bu_s[rows, c_re] = n_re
                    bu_s[rows, c_im] = n_im
                    return n_re, n_im

                s_re, s_im = lax.fori_loop(0, tb, step, (st_s[srow, c_re], st_s[srow, c_im]))
                st_s[srow, c_re] = s_re
                st_s[srow, c_im] = s_im

    ys = []
    for kt in range(S5_KTILES):
        ys.append(_dot(bu_s[:, kt * tile:(kt + 1) * tile].astype(BF16), wc_ref[kt]))
    y = jnp.concatenate(ys, axis=1) + dsk_ref[...] * u
    y = _gelu_tanh(y)
    y = y * jax.nn.sigmoid(_dot(y.astype(BF16), wglu_ref[...]) + bglu_ref[...])
    y = (y * gate).astype(BF16)
    xo = x + _dot(y, wout_ref[...])
    if seq_major_io:
        xo_ref[...] = jnp.swapaxes(xo.reshape(tb, n_seq, D_MODEL), 0, 1)
    else:
        xo_ref[...] = xo

    @pl.when(ti == pl.num_programs(0) - 1)
    def _():
        sfin_ref[...] = st_s[...]


def _s5_layer(x, s0, nw, win, wb, a_re, a_im, wc, dsk, wglu, bglu, wout, *, n_seq, seq, tb, seq_major_io):
    rows = tb * n_seq
    ncol = 2 * S5_NSTATE
    if seq_major_io:
        x_spec = pl.BlockSpec((n_seq, tb, D_MODEL), lambda t: (0, t, 0))
    else:
        x_spec = pl.BlockSpec((rows, D_MODEL), lambda t: (t, 0))
    return pl.pallas_call(
        functools.partial(_s5_kernel, n_seq=n_seq, tb=tb, seq_major_io=seq_major_io),
        grid=(seq // tb,),
        in_specs=[x_spec, _const_spec((n_seq, ncol)),
                  _const_spec((1, D_MODEL)), _const_spec((D_MODEL, 2 * D_MODEL)), _const_spec(wb.shape),
                  _const_spec((8, ncol)), _const_spec((8, ncol)), _const_spec(wc.shape),
                  _const_spec((1, D_MODEL)), _const_spec((D_MODEL, D_MODEL)), _const_spec((1, D_MODEL)),
                  _const_spec((D_MODEL, D_MODEL))],
        out_specs=[x_spec, pl.BlockSpec((n_seq, ncol), lambda t: (0, 0))],
        out_shape=[jax.ShapeDtypeStruct(x.shape, F32), jax.ShapeDtypeStruct((n_seq, ncol), F32)],
        scratch_shapes=[pltpu.VMEM((rows, ncol), F32), pltpu.VMEM((n_seq, ncol), F32)],
        compiler_params=_cparams("arbitrary"),
        name="s5_layer",
    )(x, s0, nw, win, wb, a_re, a_im, wc, dsk, wglu, bglu, wout)


def _s5_params(a_re, a_im, b_re, b_im, c_re, c_im, log_dt):
    lam_re = jnp.minimum(a_re, S5_MAX_RE)
    lam_im = a_im
    dt = jnp.exp(log_dt)[:, None]
    mag = jnp.exp(lam_re * dt)
    bar_re = mag * jnp.cos(lam_im * dt)
    bar_im = mag * jnp.sin(lam_im * dt)
    den = lam_re * lam_re + lam_im * lam_im
    xr = bar_re - 1.0
    coef_re = (xr * lam_re + bar_im * lam_im) / den
    coef_im = (bar_im * lam_re - xr * lam_im) / den
    bbar_re = coef_re[..., None] * b_re - coef_im[..., None] * b_im
    bbar_im = coef_re[..., None] * b_im + coef_im[..., None] * b_re
    gl = S5_GROUPS // S5_KTILES
    eye = jnp.eye(gl, dtype=F32)

    def to_cols(a):
        return a.reshape(S5_KTILES, gl * S5_STATE)

    def b_tile(bb):
        bb = bb.reshape(S5_KTILES, gl, S5_STATE, S5_GROUP_CH)
        return jnp.einsum('kgpc,gh->kgchp', bb, eye).reshape(S5_KTILES, gl * S5_GROUP_CH, gl * S5_STATE)

    def c_tile(cc):
        cc = cc.reshape(S5_KTILES, gl, S5_GROUP_CH, S5_STATE)
        return jnp.einsum('kgcp,gh->kgphc', cc, eye).reshape(S5_KTILES, gl * S5_STATE, gl * S5_GROUP_CH)

    wb = jnp.concatenate([b_tile(bbar_re), b_tile(bbar_im)], axis=2).astype(BF16)
    wc = jnp.concatenate([c_tile(c_re), -c_tile(c_im)], axis=1).astype(BF16)
    cols = lambda a: jnp.concatenate([to_cols(a), to_cols(a)], axis=1).reshape(1, -1)
    a_re_cols = jnp.broadcast_to(cols(bar_re), (8, 2 * S5_NSTATE))
    a_im_cols = jnp.broadcast_to(cols(bar_im), (8, 2 * S5_NSTATE))
    return wb, wc, a_re_cols, a_im_cols


def _s5_state_to_cols(s):
    n = s.shape[0]
    gl = S5_GROUPS // S5_KTILES
    s = s.reshape(n, S5_KTILES, gl * S5_STATE, 2)
    return jnp.moveaxis(s, 3, 2).reshape(n, 2 * S5_NSTATE)


def _s5_cols_to_state(c):
    n = c.shape[0]
    gl = S5_GROUPS // S5_KTILES
    c = c.reshape(n, S5_KTILES, 2, gl * S5_STATE)
    return jnp.moveaxis(c, 2, 3).reshape(n, S5_GROUPS, S5_STATE, 2)


def _rope_tables(pos):
    half = HEAD_DIM // 2
    inv_freq = ROPE_THETA ** (-jnp.arange(half, dtype=F32) / half)
    ang = pos[:, None] * inv_freq[None, :]
    cos, sin = jnp.cos(ang), jnp.sin(ang)
    return jnp.concatenate([cos, cos], axis=1), jnp.concatenate([-sin, sin], axis=1)


def kernel(x_prompt, x_sample, state_hgrn, cache_kv_w128, cache_kv_w512, cache_kv_w2048, state_s5,
           norm_w, final_norm_w, a_w_in, a_lb_logits, a_onorm_w, a_w_out, b_w_in, b_w_out,
           c_w_in, c_a_re, c_a_im, c_b_re, c_b_im, c_c_re, c_c_im, c_d, c_log_dt, c_w_glu, c_b_glu, c_w_out):
    batch, seq, _ = x_prompt.shape
    n_seq, n_tok, _ = x_sample.shape
    depth = norm_w.shape[0]
    caches = (cache_kv_w128, cache_kv_w512, cache_kv_w2048)
    dils = [d for _, d in ATTN_GROUPS]
    row = lambda a: a.reshape(1, -1)

    p_lb = jax.nn.softmax(a_lb_logits.astype(F32), axis=0)
    lower_bounds = jnp.cumsum(p_lb, axis=0) - p_lb[0:1]
    fnw = row(final_norm_w)

    s_rows = n_seq * SAMPLE_PAD
    xs = jnp.pad(x_sample, ((0, 0), (0, SAMPLE_PAD - n_tok), (0, 0))).reshape(s_rows, D_MODEL)
    xp = x_prompt
    pos_p = jnp.arange(seq, dtype=F32)
    pos_s = jnp.tile(jnp.pad(PAST_LEN + jnp.arange(n_tok, dtype=F32), (0, SAMPLE_PAD - n_tok)), n_seq)

    hgrn_p, hgrn_s, s5_p, s5_s = [], [], [], []
    kv_p = [[] for _ in ATTN_GROUPS]
    kv_s = [[] for _ in ATTN_GROUPS]
    for layer in range(depth):
        kind, j = layer % 3, layer // 3
        last = layer == depth - 1
        nw = row(norm_w[layer])
        if kind == 0:
            win, wout = a_w_in[j].astype(BF16), a_w_out[j].astype(BF16)
            lb, onw = row(lower_bounds[j]), row(a_onorm_w[j])
            xp, st = _hgrn_prompt(xp, nw, win, lb, onw, wout, fnw, final_norm=last)
            hgrn_p.append(st)
            xs, st = _hgrn_sample(xs, state_hgrn, j, nw, win, lb, onw, wout, fnw, n_tok=n_tok, final_norm=last)
            hgrn_s.append(st)
        elif kind == 1:
            win, wout = b_w_in[j].astype(BF16), b_w_out[j].astype(BF16)
            cos, sin = _rope_tables(pos_p)
            *qkvs, gate, kv0, kv1, kv2 = _attn_proj(xp, cos, sin, nw, win, dils=dils,
                                                    keeps=[min(w, seq) for w, _ in ATTN_GROUPS], tb=256)
            for g, kv in enumerate((kv0, kv1, kv2)):
                kv_p[g].append(kv)
            outs, lses = zip(*[_attn_group(qkv) for qkv in qkvs])
            xp = _attn_out(outs, lses, gate, xp, wout, tb=256)

            cos, sin = _rope_tables(pos_s)
            *qkvs, gate, kv0, kv1, kv2 = _attn_proj(xs.reshape(1, s_rows, D_MODEL), cos, sin, nw, win,
                                                    dils=[1] * len(dils), keeps=[s_rows] * len(dils), tb=s_rows)
            new_rows = [kv.reshape(s_rows, 2, N_HEADS, HEAD_DIM) for kv in (kv0, kv1, kv2)]
            xs = _attn_sample([q.reshape(s_rows, 3 * D_MODEL) for q in qkvs], new_rows, caches, j,
                              gate.reshape(s_rows, D_MODEL), xs, wout, n_tok=n_tok)
            news = [kv.reshape(n_seq, SAMPLE_PAD, 2, N_HEADS, HEAD_DIM)[:, :n_tok] for kv in new_rows]
            for g, nb in enumerate(_cache_roll(caches, news, j, n_tok=n_tok)):
                kv_s[g].append(nb)
        else:
            wb, wc, a_re_cols, a_im_cols = _s5_params(c_a_re[j], c_a_im[j], c_b_re[j], c_b_im[j],
                                                      c_c_re[j], c_c_im[j], c_log_dt[j])
            wts = (nw, c_w_in[j].astype(BF16), wb, a_re_cols, a_im_cols, wc, row(c_d[j]),
                   c_w_glu[j].astype(BF16), row(c_b_glu[j]), c_w_out[j].astype(BF16))
            xp, sfin = _s5_layer(xp, jnp.zeros((batch, 2 * S5_NSTATE), F32), *wts,
                                 n_seq=batch, seq=seq, tb=32, seq_major_io=True)
            s5_p.append(_s5_cols_to_state(sfin))
            xs_tm = jnp.swapaxes(xs.reshape(n_seq, SAMPLE_PAD, D_MODEL)[:, :n_tok], 0, 1)
            xs_tm, sfin = _s5_layer(xs_tm.reshape(n_tok * n_seq, D_MODEL), _s5_state_to_cols(state_s5[j]), *wts,
                                    n_seq=n_seq, seq=n_tok, tb=n_tok, seq_major_io=False)
            s5_s.append(_s5_cols_to_state(sfin))
            xs = jnp.pad(jnp.swapaxes(xs_tm.reshape(n_tok, n_seq, D_MODEL), 0, 1),
                         ((0, 0), (0, SAMPLE_PAD - n_tok), (0, 0))).reshape(s_rows, D_MODEL)

    if depth % 3 != 1:
        raise NotImplementedError("the final norm is fused into a last HGRN2 layer")
    y_sample = xs.reshape(n_seq, SAMPLE_PAD, D_MODEL)[:, :n_tok]
    stack = lambda parts: jnp.stack(parts, axis=0)
    return (xp, y_sample, stack(hgrn_p), stack(hgrn_s),
            stack(kv_p[0]), stack(kv_s[0]), stack(kv_p[1]), stack(kv_s[1]), stack(kv_p[2]), stack(kv_s[2]),
            stack(s5_p), stack(s5_s))
```

```python
import functools
import math

import numpy as np
import jax
import jax.numpy as jnp
from jax import lax
from jax.experimental import pallas as pl
from jax.experimental.pallas import tpu as pltpu

F32 = jnp.float32
BF16 = jnp.bfloat16

D_MODEL = 1024
N_HEADS = 8
HEAD_DIM = 128
NORM_EPS = 1e-6
NEG_BIG = -1e30
PAST_LEN = 8192
ROPE_THETA = 10000.0
A_EXP_CLIP = 60.0
ATTN_GROUPS = ((128, 1), (512, 4), (2048, 16))
ATTN_BLOCK = 128
S5_GROUPS = 64
S5_STATE = 64
S5_GROUP_CH = 16
S5_MAX_RE = -1e-4
S5_KTILES = 4
S5_NSTATE = S5_GROUPS * S5_STATE
SAMPLE_PAD = 8
KV_ROWS = 2 * N_HEADS
VMEM_LIMIT_V7X = 56 * 1024 * 1024


def _cparams(*sem):
    return pltpu.CompilerParams(dimension_semantics=sem, vmem_limit_bytes=VMEM_LIMIT_V7X)


def _const_spec(shape):
    nd = len(shape)
    return pl.BlockSpec(shape, lambda *_: (0,) * nd, pipeline_mode=pl.Buffered(1))


def _dot(a, b):
    return jnp.dot(a, b, preferred_element_type=F32)


def _dot_nt(a, b):
    return lax.dot_general(a, b, (((1,), (1,)), ((), ())), preferred_element_type=F32)


def _dot_tn(a, b):
    return lax.dot_general(a, b, (((0,), (0,)), ((), ())), preferred_element_type=F32)


def _rms(x, w):
    ms = jnp.mean(x * x, axis=-1, keepdims=True)
    return x * lax.rsqrt(ms + NORM_EPS) * w


def _silu(x):
    return x * jax.nn.sigmoid(x)


def _hgrn_gates(zf, lb):
    ez = jnp.exp(-jnp.abs(zf))
    log_sig = jnp.minimum(zf, 0.0) - jnp.log1p(ez)
    log_f = log_sig + jnp.log1p(lb * jnp.exp(jnp.minimum(-zf, A_EXP_CLIP)))
    r = 1.0 / (1.0 + ez)
    key = (1.0 - lb) * jnp.where(zf >= 0.0, ez * r, r)
    return log_f, key


def _hgrn_decay_matrix(c):
    levels = int(math.log2(c))
    t = np.arange(c)[:, None]
    u = np.arange(c)[None, :]
    blocks = [u <= t, u > t]
    for lev in range(1, levels + 1):
        size = 2 ** lev
        half = size // 2
        mid = (t // size) * size + half
        second = (t % size) >= half
        blocks.append(np.where(second, (u >= mid) & (u <= t), (u > t) & (u <= mid - 1)))
    return np.concatenate(blocks, axis=0).astype(np.float32)


def _hgrn_level_matrix(c):
    t = np.arange(c)[:, None]
    s = np.arange(c)[None, :]
    x = t ^ s
    lev = np.zeros((c, c), np.int32)
    nz = x > 0
    lev[nz] = np.floor(np.log2(x[nz])).astype(np.int32) + 1
    return np.where(s <= t, lev, -1).astype(np.int32)


def _hgrn_prompt_kernel(x_ref, nw_ref, win_ref, lb_ref, onw_ref, wout_ref, fnw_ref, dmat_ref, lvl_ref,
                        xo_ref, st_ref,
                        q_s, k_s, v_s, g_s, gate_s, o_s, st_s, *, chunk, final_norm):
    ti = pl.program_id(1)
    tb = x_ref.shape[0]
    levels = int(math.log2(chunk))

    @pl.when(ti == 0)
    def _():
        st_s[...] = jnp.zeros_like(st_s)

    x = x_ref[...]
    h = _rms(x, nw_ref[...]).astype(BF16)
    q_s[...] = _silu(_dot(h, win_ref[:, 0:D_MODEL]))
    log_f, key = _hgrn_gates(_dot(h, win_ref[:, D_MODEL:2 * D_MODEL]), lb_ref[...])
    g_s[...] = log_f
    k_s[...] = key
    v_s[...] = _dot(h, win_ref[:, 2 * D_MODEL:3 * D_MODEL])
    gate_s[...] = _silu(_dot(h, win_ref[:, 3 * D_MODEL:4 * D_MODEL]))

    lvl = lvl_ref[...]
    row = lax.broadcasted_iota(jnp.int32, (chunk, HEAD_DIM), 0)

    def chunk_body(ci, carry):
        r0 = pl.multiple_of(ci * chunk, chunk)
        rows = pl.ds(r0, chunk)
        for hh in range(N_HEADS):
            cs = slice(hh * HEAD_DIM, (hh + 1) * HEAD_DIM)
            qh = q_s[rows, cs]
            kh = k_s[rows, cs]
            vh = v_s[rows, cs].astype(BF16)
            gh = g_s[rows, cs]
            g_hi = gh.astype(BF16)
            g_lo = (gh - g_hi.astype(F32)).astype(BF16)
            ex = _dot(dmat_ref[...], jnp.concatenate([g_hi, g_lo], axis=1))
            ex = ex[:, :HEAD_DIM] + ex[:, HEAD_DIM:]
            b_read = ex[0:chunk]
            b_write = ex[chunk:2 * chunk]
            st = st_s[hh]
            o = _dot_nt((qh * jnp.exp(b_read)).astype(BF16), st.astype(BF16))
            k_dec = (kh * jnp.exp(b_write)).astype(BF16)
            st_s[hh] = st * jnp.exp(b_read[chunk - 1:chunk, :]) + _dot_tn(vh, k_dec)
            p = jnp.where(lvl == 0, _dot_nt(qh.astype(BF16), kh.astype(BF16)), 0.0)
            for lev in range(1, levels + 1):
                e_l = jnp.exp(ex[(lev + 1) * chunk:(lev + 2) * chunk])
                second = ((row >> (lev - 1)) & 1) == 1
                a_l = (jnp.where(second, qh, kh) * e_l).astype(BF16)
                p = jnp.where(lvl == lev, _dot_nt(a_l, a_l), p)
            o = o + _dot(p.astype(BF16), vh)
            ms = jnp.mean(o * o, axis=-1, keepdims=True)
            o_s[rows, cs] = o * lax.rsqrt(ms + NORM_EPS) * onw_ref[:, cs]
        return carry

    lax.fori_loop(0, tb // chunk, chunk_body, 0)

    y = (o_s[...] * gate_s[...]).astype(BF16)
    xo = x + _dot(y, wout_ref[...])
    if final_norm:
        xo = _rms(xo, fnw_ref[...])
    xo_ref[...] = xo

    @pl.when(ti == pl.num_programs(1) - 1)
    def _():
        for hh in range(N_HEADS):
            st_ref[hh] = st_s[hh].T


def _hgrn_prompt(x, nw, win, lb, onw, wout, fnw, *, final_norm, tb=256, chunk=128):
    batch, seq, _ = x.shape
    dmat = jnp.asarray(_hgrn_decay_matrix(chunk), BF16)
    lvl = jnp.asarray(_hgrn_level_matrix(chunk))
    row_scratch = pltpu.VMEM((tb, D_MODEL), F32)
    x_spec = pl.BlockSpec((None, tb, D_MODEL), lambda b, t: (b, t, 0))
    return pl.pallas_call(
        functools.partial(_hgrn_prompt_kernel, chunk=chunk, final_norm=final_norm),
        grid=(batch, seq // tb),
        in_specs=[x_spec, _const_spec((1, D_MODEL)), _const_spec((D_MODEL, 4 * D_MODEL)),
                  _const_spec((1, D_MODEL)), _const_spec((1, D_MODEL)), _const_spec((D_MODEL, D_MODEL)),
                  _const_spec((1, D_MODEL)), _const_spec(dmat.shape), _const_spec(lvl.shape)],
        out_specs=[x_spec, pl.BlockSpec((None, N_HEADS, HEAD_DIM, HEAD_DIM), lambda b, t: (b, 0, 0, 0))],
        out_shape=[jax.ShapeDtypeStruct((batch, seq, D_MODEL), F32),
                   jax.ShapeDtypeStruct((batch, N_HEADS, HEAD_DIM, HEAD_DIM), F32)],
        scratch_shapes=[row_scratch] * 6 + [pltpu.VMEM((N_HEADS, HEAD_DIM, HEAD_DIM), F32)],
        compiler_params=_cparams("arbitrary", "arbitrary"),
        name="hgrn_prompt",
    )(x, nw, win, lb, onw, wout, fnw, dmat, lvl)


def _hgrn_sample_kernel(x_ref, s0_ref, nw_ref, win_ref, lb_ref, onw_ref, wout_ref, fnw_ref,
                        xo_ref, s_ref, q_s, k_s, v_s, f_s, o_s, *, n_tok, final_norm):
    nb = s0_ref.shape[0]
    x = x_ref[...]
    h = _rms(x, nw_ref[...]).astype(BF16)
    q_s[...] = _silu(_dot(h, win_ref[:, 0:D_MODEL]))
    log_f, key = _hgrn_gates(_dot(h, win_ref[:, D_MODEL:2 * D_MODEL]), lb_ref[...])
    f_s[...] = jnp.exp(log_f)
    k_s[...] = key
    v_s[...] = _dot(h, win_ref[:, 2 * D_MODEL:3 * D_MODEL])
    gate = _silu(_dot(h, win_ref[:, 3 * D_MODEL:4 * D_MODEL]))
    row = lax.broadcasted_iota(jnp.int32, (SAMPLE_PAD, HEAD_DIM), 0)

    def seq_body(bi, carry):
        rows = pl.ds(pl.multiple_of(bi * SAMPLE_PAD, SAMPLE_PAD), SAMPLE_PAD)
        for hh in range(N_HEADS):
            cs = slice(hh * HEAD_DIM, (hh + 1) * HEAD_DIM)
            q_t = q_s[rows, cs].T
            k_t = k_s[rows, cs].T
            f_t = f_s[rows, cs].T
            vh = v_s[rows, cs]
            st = s0_ref[bi, hh]
            o = jnp.zeros((SAMPLE_PAD, HEAD_DIM), F32)
            for t in range(n_tok):
                st = st * f_t[:, t:t + 1] + k_t[:, t:t + 1] * vh[t:t + 1, :]
                o_t = jnp.sum(st * q_t[:, t:t + 1], axis=0, keepdims=True)
                o = jnp.where(row == t, o_t, o)
            s_ref[bi, hh] = st
            ms = jnp.mean(o * o, axis=-1, keepdims=True)
            o_s[rows, cs] = o * lax.rsqrt(ms + NORM_EPS) * onw_ref[:, cs]
        return carry

    lax.fori_loop(0, nb, seq_body, 0)
    y = (o_s[...] * gate).astype(BF16)
    xo = x + _dot(y, wout_ref[...])
    if final_norm:
        xo = _rms(xo, fnw_ref[...])
    xo_ref[...] = xo


def _hgrn_sample(x, s0, layer, nw, win, lb, onw, wout, fnw, *, n_tok, final_norm, nb=8):
    n_seq = s0.shape[1]
    rows = nb * SAMPLE_PAD
    row_scratch = pltpu.VMEM((rows, D_MODEL), F32)
    return pl.pallas_call(
        functools.partial(_hgrn_sample_kernel, n_tok=n_tok, final_norm=final_norm),
        grid=(n_seq // nb,),
        in_specs=[pl.BlockSpec((rows, D_MODEL), lambda i: (i, 0)),
                  pl.BlockSpec((None, nb, N_HEADS, HEAD_DIM, HEAD_DIM), lambda i: (layer, i, 0, 0, 0)),
                  _const_spec((1, D_MODEL)), _const_spec((D_MODEL, 4 * D_MODEL)), _const_spec((1, D_MODEL)),
                  _const_spec((1, D_MODEL)), _const_spec((D_MODEL, D_MODEL)), _const_spec((1, D_MODEL))],
        out_specs=[pl.BlockSpec((rows, D_MODEL), lambda i: (i, 0)),
                   pl.BlockSpec((nb, N_HEADS, HEAD_DIM, HEAD_DIM), lambda i: (i, 0, 0, 0))],
        out_shape=[jax.ShapeDtypeStruct(x.shape, F32), jax.ShapeDtypeStruct(s0.shape[1:], F32)],
        scratch_shapes=[row_scratch] * 5,
        compiler_params=_cparams("arbitrary"),
        name="hgrn_sample",
    )(x, s0, nw, win, lb, onw, wout, fnw)


def _attn_proj_kernel(x_ref, cos_ref, sin_ref, nw_ref, win_ref,
                      g0_ref, g1_ref, g2_ref, gate_ref, kv0_ref, kv1_ref, kv2_ref):
    tb = x_ref.shape[0]
    h = _rms(x_ref[...], nw_ref[...]).astype(BF16)
    cos = cos_ref[...]
    sin = sin_ref[...]

    def rope(a):
        cols = []
        for hh in range(N_HEADS):
            ah = a[:, hh * HEAD_DIM:(hh + 1) * HEAD_DIM]
            cols.append(ah * cos + pltpu.roll(ah, HEAD_DIM // 2, 1) * sin)
        return jnp.concatenate(cols, axis=1)

    for gi, (qkv_ref, kv_ref) in enumerate(((g0_ref, kv0_ref), (g1_ref, kv1_ref), (g2_ref, kv2_ref))):
        base = 3 * gi * D_MODEL
        dil = qkv_ref.shape[0]
        q = rope(_dot(h, win_ref[:, base:base + D_MODEL])) * (HEAD_DIM ** -0.5)
        k = rope(_dot(h, win_ref[:, base + D_MODEL:base + 2 * D_MODEL]))
        v = _dot(h, win_ref[:, base + 2 * D_MODEL:base + 3 * D_MODEL])
        for ci, val in enumerate((q, k, v)):
            if dil == 1:
                val = val.reshape(1, tb, D_MODEL)
            else:
                val = jnp.swapaxes(val.reshape(tb // dil, dil, D_MODEL), 0, 1)
            qkv_ref[:, :, ci * D_MODEL:(ci + 1) * D_MODEL] = val.astype(BF16)
        keep = kv_ref.shape[0]
        kv_ref[:, 0] = k[tb - keep:, :].reshape(keep, N_HEADS, HEAD_DIM)
        kv_ref[:, 1] = v[tb - keep:, :].reshape(keep, N_HEADS, HEAD_DIM)
    gate_ref[...] = _silu(_dot(h, win_ref[:, 9 * D_MODEL:10 * D_MODEL]))


def _attn_proj(x, cos, sin, nw, win, *, dils, keeps, tb):
    batch, seq, _ = x.shape
    qkv_specs, qkv_shapes, kv_specs, kv_shapes = [], [], [], []
    for dil, keep in zip(dils, keeps):
        qkv_specs.append(pl.BlockSpec((None, dil, tb // dil, 3 * D_MODEL), lambda b, t: (b, 0, t, 0)))
        qkv_shapes.append(jax.ShapeDtypeStruct((batch, dil, seq // dil, 3 * D_MODEL), BF16))
        kb = min(tb, keep)
        first = (seq - keep) // tb if keep >= tb else 0
        kv_specs.append(pl.BlockSpec(
            (None, kb, 2, N_HEADS, HEAD_DIM),
            lambda b, t, first=first, kb=kb: (b, jnp.maximum(t - first, 0) if kb == tb else 0, 0, 0, 0)))
        kv_shapes.append(jax.ShapeDtypeStruct((batch, keep, 2, N_HEADS, HEAD_DIM), F32))
    return pl.pallas_call(
        _attn_proj_kernel,
        grid=(batch, seq // tb),
        in_specs=[pl.BlockSpec((None, tb, D_MODEL), lambda b, t: (b, t, 0)),
                  pl.BlockSpec((tb, HEAD_DIM), lambda b, t: (t, 0)),
                  pl.BlockSpec((tb, HEAD_DIM), lambda b, t: (t, 0)),
                  _const_spec((1, D_MODEL)), _const_spec((D_MODEL, 10 * D_MODEL))],
        out_specs=qkv_specs + [pl.BlockSpec((None, tb, D_MODEL), lambda b, t: (b, t, 0))] + kv_specs,
        out_shape=qkv_shapes + [jax.ShapeDtypeStruct((batch, seq, D_MODEL), F32)] + kv_shapes,
        compiler_params=_cparams("arbitrary", "arbitrary"),
        name="attn_proj",
    )(x, cos, sin, nw, win)


def _attn_group_kernel(q_ref, kc_ref, kp_ref, vc_ref, vp_ref, o_ref, lse_ref):
    j = pl.program_id(2)
    a = lax.broadcasted_iota(jnp.int32, (ATTN_BLOCK, ATTN_BLOCK), 0)
    c = lax.broadcasted_iota(jnp.int32, (ATTN_BLOCK, ATTN_BLOCK), 1)
    mask_cur = c <= a
    mask_prev = (c >= a) & (j > 0)
    lane = lax.broadcasted_iota(jnp.int32, (ATTN_BLOCK, HEAD_DIM), 1)
    lse_tile = jnp.zeros((ATTN_BLOCK, HEAD_DIM), F32)
    for hh in range(N_HEADS):
        cs = slice(hh * HEAD_DIM, (hh + 1) * HEAD_DIM)
        qh = q_ref[:, cs]
        s_c = jnp.where(mask_cur, _dot_nt(qh, kc_ref[:, cs]), NEG_BIG)
        s_p = jnp.where(mask_prev, _dot_nt(qh, kp_ref[:, cs]), NEG_BIG)
        m = jnp.maximum(jnp.max(s_c, axis=1, keepdims=True), jnp.max(s_p, axis=1, keepdims=True))
        p_c = jnp.exp(s_c - m)
        p_p = jnp.exp(s_p - m)
        den = jnp.sum(p_c, axis=1, keepdims=True) + jnp.sum(p_p, axis=1, keepdims=True)
        o = _dot(p_c.astype(BF16), vc_ref[:, cs]) + _dot(p_p.astype(BF16), vp_ref[:, cs])
        o_ref[:, cs] = o / den
        lse_tile = jnp.where(lane == hh, m + jnp.log(den), lse_tile)
    lse_ref[...] = lse_tile


def _attn_group(qkv):
    batch, dil, n, _ = qkv.shape

    def spec(col, prev):
        if prev:
            return pl.BlockSpec((None, None, ATTN_BLOCK, D_MODEL), lambda b, r, j: (b, r, jnp.maximum(j - 1, 0), col))
        return pl.BlockSpec((None, None, ATTN_BLOCK, D_MODEL), lambda b, r, j: (b, r, j, col))

    return pl.pallas_call(
        _attn_group_kernel,
        grid=(batch, dil, n // ATTN_BLOCK),
        in_specs=[spec(0, False), spec(1, False), spec(1, True), spec(2, False), spec(2, True)],
        out_specs=[pl.BlockSpec((None, None, ATTN_BLOCK, D_MODEL), lambda b, r, j: (b, r, j, 0)),
                   pl.BlockSpec((None, None, ATTN_BLOCK, HEAD_DIM), lambda b, r, j: (b, r, j, 0))],
        out_shape=[jax.ShapeDtypeStruct((batch, dil, n, D_MODEL), F32),
                   jax.ShapeDtypeStruct((batch, dil, n, HEAD_DIM), F32)],
        compiler_params=_cparams("arbitrary", "arbitrary", "arbitrary"),
        name=f"attn_group_d{dil}",
    )(qkv, qkv, qkv, qkv, qkv)


def _merge_heads(outs, lses):
    m = functools.reduce(jnp.maximum, lses)
    es = [jnp.exp(l - m) for l in lses]
    den = functools.reduce(jnp.add, es)
    ws = [e / den for e in es]
    cols = []
    for hh in range(N_HEADS):
        cs = slice(hh * HEAD_DIM, (hh + 1) * HEAD_DIM)
        cols.append(functools.reduce(jnp.add, [w[:, hh:hh + 1] * o[:, cs] for w, o in zip(ws, outs)]))
    return jnp.concatenate(cols, axis=1)


def _attn_out_kernel(o0_ref, o1_ref, o2_ref, l0_ref, l1_ref, l2_ref, gate_ref, x_ref, wout_ref, xo_ref):
    def natural(ref):
        dil, n, w = ref.shape
        return ref[0] if dil == 1 else jnp.swapaxes(ref[...], 0, 1).reshape(dil * n, w)

    o = _merge_heads([natural(r) for r in (o0_ref, o1_ref, o2_ref)], [natural(r) for r in (l0_ref, l1_ref, l2_ref)])
    y = (o * gate_ref[...]).astype(BF16)
    xo_ref[...] = x_ref[...] + _dot(y, wout_ref[...])


def _attn_out(outs, lses, gate, x, wout, *, tb):
    batch, seq, _ = x.shape
    regrouped = lambda a: pl.BlockSpec((None, a.shape[1], tb // a.shape[1], a.shape[3]), lambda b, t: (b, 0, t, 0))
    row = pl.BlockSpec((None, tb, D_MODEL), lambda b, t: (b, t, 0))
    return pl.pallas_call(
        _attn_out_kernel,
        grid=(batch, seq // tb),
        in_specs=[regrouped(a) for a in outs] + [regrouped(a) for a in lses] + [row, row, _const_spec((D_MODEL, D_MODEL))],
        out_specs=row,
        out_shape=jax.ShapeDtypeStruct((batch, seq, D_MODEL), F32),
        compiler_params=_cparams("arbitrary", "arbitrary"),
        name="attn_out",
    )(*outs, *lses, gate, x, wout)


def _attn_sample_kernel(q0_ref, q1_ref, q2_ref, n0_ref, n1_ref, n2_ref, c0_ref, c1_ref, c2_ref,
                        gate_ref, x_ref, wout_ref, xo_ref, o_s, *, n_tok):
    nb = c0_ref.shape[0]
    pos_i = lax.broadcasted_iota(jnp.int32, (ATTN_BLOCK, N_HEADS, 1), 0)
    row_t = lax.broadcasted_iota(jnp.int32, (SAMPLE_PAD, D_MODEL), 0)
    q_refs, new_refs, cache_refs = (q0_ref, q1_ref, q2_ref), (n0_ref, n1_ref, n2_ref), (c0_ref, c1_ref, c2_ref)

    def seq_body(bi, carry):
        r0 = pl.multiple_of(bi * SAMPLE_PAD, SAMPLE_PAD)
        q_rows = [q_ref[pl.ds(r0, SAMPLE_PAD), :].astype(F32) for q_ref in q_refs]
        o_tile = jnp.zeros((SAMPLE_PAD, D_MODEL), F32)
        for t in range(n_tok):
            outs, lses = [], []
            for gi, (_, dil) in enumerate(ATTN_GROUPS):
                q_t = jnp.concatenate([q_rows[gi][t:t + 1, hh * HEAD_DIM:(hh + 1) * HEAD_DIM]
                                       for hh in range(N_HEADS)], axis=0)
                res = t % dil
                k_c = cache_refs[gi][bi, :, res, 0]
                v_c = cache_refs[gi][bi, :, res, 1]
                s_c = jnp.sum(k_c * q_t[None], axis=-1, keepdims=True)
                s_c = jnp.where(pos_i * dil + res >= t, s_c, NEG_BIG)
                new_t = [u for u in range(t + 1) if (t - u) % dil == 0]
                k_n = [new_refs[gi][r0 + u, 0] for u in new_t]
                v_n = [new_refs[gi][r0 + u, 1] for u in new_t]
                s_n = [jnp.sum(kk * q_t, axis=-1, keepdims=True) for kk in k_n]
                m = functools.reduce(jnp.maximum, s_n + [jnp.max(s_c, axis=0)])
                p_c = jnp.exp(s_c - m[None])
                p_n = [jnp.exp(s - m) for s in s_n]
                den = functools.reduce(jnp.add, p_n + [jnp.sum(p_c, axis=0)])
                acc = functools.reduce(jnp.add, [p * vv for p, vv in zip(p_n, v_n)] + [jnp.sum(p_c * v_c, axis=0)])
                outs.append(acc / den)
                lses.append(m + jnp.log(den))
            m_g = functools.reduce(jnp.maximum, lses)
            e_g = [jnp.exp(l - m_g) for l in lses]
            den_g = functools.reduce(jnp.add, e_g)
            o_t = functools.reduce(jnp.add, [e / den_g * o for e, o in zip(e_g, outs)])
            o_row = jnp.concatenate([o_t[hh:hh + 1, :] for hh in range(N_HEADS)], axis=1)
            o_tile = jnp.where(row_t == t, o_row, o_tile)
        o_s[pl.ds(r0, SAMPLE_PAD), :] = o_tile
        return carry

    lax.fori_loop(0, nb, seq_body, 0)
    y = (o_s[...] * gate_ref[...]).astype(BF16)
    xo_ref[...] = x_ref[...] + _dot(y, wout_ref[...])


def _attn_sample(qkvs, news, caches, layer, gate, x, wout, *, n_tok, nb=2):
    n_seq = caches[0].shape[1]
    rows = nb * SAMPLE_PAD
    c_views, c_specs = [], []
    for (window, dil), cache in zip(ATTN_GROUPS, caches):
        assert cache.shape[2] == window == ATTN_BLOCK * dil and (n_tok <= dil or dil == 1)
        c_views.append(cache.reshape(cache.shape[0], n_seq, ATTN_BLOCK, dil, 2, N_HEADS, HEAD_DIM))
        n_res = min(dil, n_tok)
        c_specs.append(pl.BlockSpec((None, nb, ATTN_BLOCK, n_res, 2, N_HEADS, HEAD_DIM),
                                    lambda i: (layer, i, 0, 0, 0, 0, 0)))
    row = lambda w: pl.BlockSpec((rows, w), lambda i: (i, 0))
    new_spec = pl.BlockSpec((rows, 2, N_HEADS, HEAD_DIM), lambda i: (i, 0, 0, 0))
    return pl.pallas_call(
        functools.partial(_attn_sample_kernel, n_tok=n_tok),
        grid=(n_seq // nb,),
        in_specs=[row(D_MODEL)] * 3 + [new_spec] * 3 + c_specs + [row(D_MODEL), row(D_MODEL),
                                                                  _const_spec((D_MODEL, D_MODEL))],
        out_specs=row(D_MODEL),
        out_shape=jax.ShapeDtypeStruct(x.shape, F32),
        scratch_shapes=[pltpu.VMEM((rows, D_MODEL), F32)],
        compiler_params=_cparams("arbitrary"),
        name="attn_sample",
    )(*qkvs, *news, *c_views, gate, x, wout)


def _cache_roll_kernel(c_ref, n_ref, o_ref):
    i = pl.program_id(0)
    last = pl.num_programs(0) - 1

    @pl.when(i < last)
    def _():
        o_ref[...] = c_ref[...]

    @pl.when(i == last)
    def _():
        o_ref[...] = n_ref[...]


def _cache_roll(cache, new, layer, *, n_tok):
    _, n_seq, length = cache.shape[:3]
    n_grp = length // n_tok
    rows = n_tok * KV_ROWS
    c_view = cache.reshape(cache.shape[0], n_seq, n_grp, rows, HEAD_DIM)
    n_view = new.reshape(n_seq, 1, rows, HEAD_DIM)
    out = pl.pallas_call(
        _cache_roll_kernel,
        grid=(n_grp,),
        in_specs=[pl.BlockSpec((None, n_seq, 1, rows, HEAD_DIM),
                               lambda i: (layer, 0, jnp.minimum(i + 1, n_grp - 1), 0, 0)),
                  pl.BlockSpec((n_seq, 1, rows, HEAD_DIM), lambda i: (0, 0, 0, 0))],
        out_specs=pl.BlockSpec((n_seq, 1, rows, HEAD_DIM), lambda i: (0, i, 0, 0)),
        out_shape=jax.ShapeDtypeStruct((n_seq, n_grp, rows, HEAD_DIM), cache.dtype),
        compiler_params=_cparams("arbitrary"),
        name="cache_roll",
    )(c_view, n_view)
    return out.reshape(cache.shape[1:])


def _gelu_tanh(y):
    return 0.5 * y * (1.0 + jnp.tanh(math.sqrt(2.0 / math.pi) * (y + 0.044715 * (y * y * y))))


def _s5_kernel(x_ref, s0_ref, nw_ref, win_ref, wb_ref, are_ref, aim_ref, wc_ref, dsk_ref, wglu_ref, bglu_ref,
               wout_ref, xo_ref, sfin_ref, bu_s, st_s, *, n_seq, tb, seq_major_io):
    ti = pl.program_id(0)
    tile = 2 * S5_NSTATE // S5_KTILES
    half = tile // 2
    lanes = 512

    @pl.when(ti == 0)
    def _():
        st_s[...] = s0_ref[...]

    if seq_major_io:
        x = jnp.swapaxes(x_ref[...], 0, 1).reshape(tb * n_seq, D_MODEL)
    else:
        x = x_ref[...]
    h = _rms(x, nw_ref[...]).astype(BF16)
    u = _dot(h, win_ref[:, 0:D_MODEL])
    gate = _silu(_dot(h, win_ref[:, D_MODEL:2 * D_MODEL]))
    ub = u.astype(BF16)
    kw = D_MODEL // S5_KTILES
    for kt in range(S5_KTILES):
        bu_s[:, kt * tile:(kt + 1) * tile] = _dot(ub[:, kt * kw:(kt + 1) * kw], wb_ref[kt])

    for sg in range(n_seq // 8):
        srow = slice(sg * 8, (sg + 1) * 8)
        for kt in range(S5_KTILES):
            for part in range(half // lanes):
                c_re = slice(kt * tile + part * lanes, kt * tile + (part + 1) * lanes)
                c_im = slice(kt * tile + half + part * lanes, kt * tile + half + (part + 1) * lanes)
                a_re = are_ref[:, c_re]
                a_im = aim_ref[:, c_re]

                def step(t, carry):
                    s_re, s_im = carry
                    rows = pl.ds(pl.multiple_of(t * n_seq + sg * 8, 8), 8)
                    n_re = a_re * s_re - a_im * s_im + bu_s[rows, c_re]
                    n_im = a_re * s_im + a_im * s_re + bu_s[rows, c_im]
                    bu_s[rows, c_re] = n_re
                    bu_s[rows, c_im] = n_im
                    return n_re, n_im

                s_re, s_im = lax.fori_loop(0, tb, step, (st_s[srow, c_re], st_s[srow, c_im]))
                st_s[srow, c_re] = s_re
                st_s[srow, c_im] = s_im

    ys = []
    for kt in range(S5_KTILES):
        ys.append(_dot(bu_s[:, kt * tile:(kt + 1) * tile].astype(BF16), wc_ref[kt]))
    y = jnp.concatenate(ys, axis=1) + dsk_ref[...] * u
    y = _gelu_tanh(y)
    y = y * jax.nn.sigmoid(_dot(y.astype(BF16), wglu_ref[...]) + bglu_ref[...])
    y = (y * gate).astype(BF16)
    xo = x + _dot(y, wout_ref[...])
    if seq_major_io:
        xo_ref[...] = jnp.swapaxes(xo.reshape(tb, n_seq, D_MODEL), 0, 1)
    else:
        xo_ref[...] = xo

    @pl.when(ti == pl.num_programs(0) - 1)
    def _():
        sfin_ref[...] = st_s[...]


def _s5_layer(x, s0, nw, win, wb, a_re, a_im, wc, dsk, wglu, bglu, wout, *, n_seq, seq, tb, seq_major_io):
    rows = tb * n_seq
    ncol = 2 * S5_NSTATE
    if seq_major_io:
        x_spec = pl.BlockSpec((n_seq, tb, D_MODEL), lambda t: (0, t, 0))
    else:
        x_spec = pl.BlockSpec((rows, D_MODEL), lambda t: (t, 0))
    return pl.pallas_call(
        functools.partial(_s5_kernel, n_seq=n_seq, tb=tb, seq_major_io=seq_major_io),
        grid=(seq // tb,),
        in_specs=[x_spec, _const_spec((n_seq, ncol)),
                  _const_spec((1, D_MODEL)), _const_spec((D_MODEL, 2 * D_MODEL)), _const_spec(wb.shape),
                  _const_spec((8, ncol)), _const_spec((8, ncol)), _const_spec(wc.shape),
                  _const_spec((1, D_MODEL)), _const_spec((D_MODEL, D_MODEL)), _const_spec((1, D_MODEL)),
                  _const_spec((D_MODEL, D_MODEL))],
        out_specs=[x_spec, pl.BlockSpec((n_seq, ncol), lambda t: (0, 0))],
        out_shape=[jax.ShapeDtypeStruct(x.shape, F32), jax.ShapeDtypeStruct((n_seq, ncol), F32)],
        scratch_shapes=[pltpu.VMEM((rows, ncol), F32), pltpu.VMEM((n_seq, ncol), F32)],
        compiler_params=_cparams("arbitrary"),
        name="s5_layer",
    )(x, s0, nw, win, wb, a_re, a_im, wc, dsk, wglu, bglu, wout)


def _s5_params(a_re, a_im, b_re, b_im, c_re, c_im, log_dt):
    lam_re = jnp.minimum(a_re, S5_MAX_RE)
    lam_im = a_im
    dt = jnp.exp(log_dt)[:, None]
    mag = jnp.exp(lam_re * dt)
    bar_re = mag * jnp.cos(lam_im * dt)
    bar_im = mag * jnp.sin(lam_im * dt)
    den = lam_re * lam_re + lam_im * lam_im
    xr = bar_re - 1.0
    coef_re = (xr * lam_re + bar_im * lam_im) / den
    coef_im = (bar_im * lam_re - xr * lam_im) / den
    bbar_re = coef_re[..., None] * b_re - coef_im[..., None] * b_im
    bbar_im = coef_re[..., None] * b_im + coef_im[..., None] * b_re
    gl = S5_GROUPS // S5_KTILES
    eye = jnp.eye(gl, dtype=F32)

    def to_cols(a):
        return a.reshape(S5_KTILES, gl * S5_STATE)

    def b_tile(bb):
        bb = bb.reshape(S5_KTILES, gl, S5_STATE, S5_GROUP_CH)
        return jnp.einsum('kgpc,gh->kgchp', bb, eye).reshape(S5_KTILES, gl * S5_GROUP_CH, gl * S5_STATE)

    def c_tile(cc):
        cc = cc.reshape(S5_KTILES, gl, S5_GROUP_CH, S5_STATE)
        return jnp.einsum('kgcp,gh->kgphc', cc, eye).reshape(S5_KTILES, gl * S5_STATE, gl * S5_GROUP_CH)

    wb = jnp.concatenate([b_tile(bbar_re), b_tile(bbar_im)], axis=2).astype(BF16)
    wc = jnp.concatenate([c_tile(c_re), -c_tile(c_im)], axis=1).astype(BF16)
    cols = lambda a: jnp.concatenate([to_cols(a), to_cols(a)], axis=1).reshape(1, -1)
    a_re_cols = jnp.broadcast_to(cols(bar_re), (8, 2 * S5_NSTATE))
    a_im_cols = jnp.broadcast_to(cols(bar_im), (8, 2 * S5_NSTATE))
    return wb, wc, a_re_cols, a_im_cols


def _s5_state_to_cols(s):
    n = s.shape[0]
    gl = S5_GROUPS // S5_KTILES
    s = s.reshape(n, S5_KTILES, gl * S5_STATE, 2)
    return jnp.moveaxis(s, 3, 2).reshape(n, 2 * S5_NSTATE)


def _s5_cols_to_state(c):
    n = c.shape[0]
    gl = S5_GROUPS // S5_KTILES
    c = c.reshape(n, S5_KTILES, 2, gl * S5_STATE)
    return jnp.moveaxis(c, 2, 3).reshape(n, S5_GROUPS, S5_STATE, 2)


def _rope_tables(pos):
    half = HEAD_DIM // 2
    inv_freq = ROPE_THETA ** (-jnp.arange(half, dtype=F32) / half)
    ang = pos[:, None] * inv_freq[None, :]
    cos, sin = jnp.cos(ang), jnp.sin(ang)
    return jnp.concatenate([cos, cos], axis=1), jnp.concatenate([-sin, sin], axis=1)


def kernel(x_prompt, x_sample, state_hgrn, cache_kv_w128, cache_kv_w512, cache_kv_w2048, state_s5,
           norm_w, final_norm_w, a_w_in, a_lb_logits, a_onorm_w, a_w_out, b_w_in, b_w_out,
           c_w_in, c_a_re, c_a_im, c_b_re, c_b_im, c_c_re, c_c_im, c_d, c_log_dt, c_w_glu, c_b_glu, c_w_out):
    batch, seq, _ = x_prompt.shape
    n_seq, n_tok, _ = x_sample.shape
    depth = norm_w.shape[0]
    caches = (cache_kv_w128, cache_kv_w512, cache_kv_w2048)
    dils = [d for _, d in ATTN_GROUPS]
    row = lambda a: a.reshape(1, -1)

    p_lb = jax.nn.softmax(a_lb_logits.astype(F32), axis=0)
    lower_bounds = jnp.cumsum(p_lb, axis=0) - p_lb[0:1]
    fnw = row(final_norm_w)

    s_rows = n_seq * SAMPLE_PAD
    xs = jnp.pad(x_sample, ((0, 0), (0, SAMPLE_PAD - n_tok), (0, 0))).reshape(s_rows, D_MODEL)
    xp = x_prompt
    pos_p = jnp.arange(seq, dtype=F32)
    pos_s = jnp.tile(jnp.pad(PAST_LEN + jnp.arange(n_tok, dtype=F32), (0, SAMPLE_PAD - n_tok)), n_seq)

    hgrn_p, hgrn_s, s5_p, s5_s = [], [], [], []
    kv_p = [[] for _ in ATTN_GROUPS]
    kv_s = [[] for _ in ATTN_GROUPS]
    for layer in range(depth):
        kind, j = layer % 3, layer // 3
        last = layer == depth - 1
        nw = row(norm_w[layer])
        if kind == 0:
            win, wout = a_w_in[j].astype(BF16), a_w_out[j].astype(BF16)
            lb, onw = row(lower_bounds[j]), row(a_onorm_w[j])
            xp, st = _hgrn_prompt(xp, nw, win, lb, onw, wout, fnw, final_norm=last)
            hgrn_p.append(st)
            xs, st = _hgrn_sample(xs, state_hgrn, j, nw, win, lb, onw, wout, fnw, n_tok=n_tok, final_norm=last)
            hgrn_s.append(st)
        elif kind == 1:
            win, wout = b_w_in[j].astype(BF16), b_w_out[j].astype(BF16)
            cos, sin = _rope_tables(pos_p)
            *qkvs, gate, kv0, kv1, kv2 = _attn_proj(xp, cos, sin, nw, win, dils=dils,
                                                    keeps=[min(w, seq) for w, _ in ATTN_GROUPS], tb=256)
            for g, kv in enumerate((kv0, kv1, kv2)):
                kv_p[g].append(kv)
            outs, lses = zip(*[_attn_group(qkv) for qkv in qkvs])
            xp = _attn_out(outs, lses, gate, xp, wout, tb=256)

            cos, sin = _rope_tables(pos_s)
            *qkvs, gate, kv0, kv1, kv2 = _attn_proj(xs.reshape(1, s_rows, D_MODEL), cos, sin, nw, win,
                                                    dils=[1] * len(dils), keeps=[s_rows] * len(dils), tb=s_rows)
            new_rows = [kv.reshape(s_rows, 2, N_HEADS, HEAD_DIM) for kv in (kv0, kv1, kv2)]
            xs = _attn_sample([q.reshape(s_rows, 3 * D_MODEL) for q in qkvs], new_rows, caches, j,
                              gate.reshape(s_rows, D_MODEL), xs, wout, n_tok=n_tok)
            news = [kv.reshape(n_seq, SAMPLE_PAD, 2, N_HEADS, HEAD_DIM)[:, :n_tok] for kv in new_rows]
            for g, (cache, new) in enumerate(zip(caches, news)):
                kv_s[g].append(_cache_roll(cache, new, j, n_tok=n_tok))
        else:
            wb, wc, a_re_cols, a_im_cols = _s5_params(c_a_re[j], c_a_im[j], c_b_re[j], c_b_im[j],
                                                      c_c_re[j], c_c_im[j], c_log_dt[j])
            wts = (nw, c_w_in[j].astype(BF16), wb, a_re_cols, a_im_cols, wc, row(c_d[j]),
                   c_w_glu[j].astype(BF16), row(c_b_glu[j]), c_w_out[j].astype(BF16))
            xp, sfin = _s5_layer(xp, jnp.zeros((batch, 2 * S5_NSTATE), F32), *wts,
                                 n_seq=batch, seq=seq, tb=32, seq_major_io=True)
            s5_p.append(_s5_cols_to_state(sfin))
            xs_tm = jnp.swapaxes(xs.reshape(n_seq, SAMPLE_PAD, D_MODEL)[:, :n_tok], 0, 1)
            xs_tm, sfin = _s5_layer(xs_tm.reshape(n_tok * n_seq, D_MODEL), _s5_state_to_cols(state_s5[j]), *wts,
                                    n_seq=n_seq, seq=n_tok, tb=n_tok, seq_major_io=False)
            s5_s.append(_s5_cols_to_state(sfin))
            xs = jnp.pad(jnp.swapaxes(xs_tm.reshape(n_tok, n_seq, D_MODEL), 0, 1),
                         ((0, 0), (0, SAMPLE_PAD - n_tok), (0, 0))).reshape(s_rows, D_MODEL)

    if depth % 3 != 1:
        raise NotImplementedError("the final norm is fused into a last HGRN2 layer")
    y_sample = xs.reshape(n_seq, SAMPLE_PAD, D_MODEL)[:, :n_tok]
    stack = lambda parts: jnp.stack(parts, axis=0)
    return (xp, y_sample, stack(hgrn_p), stack(hgrn_s),
            stack(kv_p[0]), stack(kv_s[0]), stack(kv_p[1]), stack(kv_s[1]), stack(kv_p[2]), stack(kv_s[2]),
            stack(s5_p), stack(s5_s))
```

```python
import functools
import math

import numpy as np
import jax
import jax.numpy as jnp
from jax import lax
from jax.experimental import pallas as pl
from jax.experimental.pallas import tpu as pltpu

F32 = jnp.float32
BF16 = jnp.bfloat16

D_MODEL = 1024
N_HEADS = 8
HEAD_DIM = 128
NORM_EPS = 1e-6
NEG_BIG = -1e30
PAST_LEN = 8192
ROPE_THETA = 10000.0
A_EXP_CLIP = 60.0
LOG2_E = 1.4426950408889634
ATTN_GROUPS = ((128, 1), (512, 4), (2048, 16))
ATTN_BLOCK = 128
S5_GROUPS = 64
S5_STATE = 64
S5_GROUP_CH = 16
S5_MAX_RE = -1e-4
S5_KTILES = 4
S5_NSTATE = S5_GROUPS * S5_STATE
SAMPLE_PAD = 8
KV_ROWS = 2 * N_HEADS
VMEM_LIMIT_V7X = 56 * 1024 * 1024


def _cparams(*sem):
    return pltpu.CompilerParams(dimension_semantics=sem, vmem_limit_bytes=VMEM_LIMIT_V7X)


def _const_spec(shape):
    nd = len(shape)
    return pl.BlockSpec(shape, lambda *_: (0,) * nd, pipeline_mode=pl.Buffered(1))


def _dot(a, b):
    return jnp.dot(a, b, preferred_element_type=F32)


def _dot_nt(a, b):
    return lax.dot_general(a, b, (((1,), (1,)), ((), ())), preferred_element_type=F32)


def _dot_tn(a, b):
    return lax.dot_general(a, b, (((0,), (0,)), ((), ())), preferred_element_type=F32)


def _rms(x, w):
    ms = jnp.mean(x * x, axis=-1, keepdims=True)
    return x * lax.rsqrt(ms + NORM_EPS) * w


def _silu(x):
    return x * jax.nn.sigmoid(x)


def _hgrn_gates(zf, lb):
    ez = jnp.exp(-jnp.abs(zf))
    log_sig = jnp.minimum(zf, 0.0) - jnp.log1p(ez)
    log_f = log_sig + jnp.log1p(lb * jnp.exp(jnp.minimum(-zf, A_EXP_CLIP)))
    r = 1.0 / (1.0 + ez)
    key = (1.0 - lb) * jnp.where(zf >= 0.0, ez * r, r)
    return log_f, key


SUBLANES = 8


def _rows_from_partner(z, bit, take_upper):
    c = z.shape[0]
    step = 1 << bit
    if step < SUBLANES:
        z3 = z.reshape(c // SUBLANES, SUBLANES, z.shape[1])
        sub = lax.broadcasted_iota(jnp.int32, z3.shape, 1)
        shift = SUBLANES - step if take_upper else step
        moved = pltpu.roll(z3, shift, 1)
        keep = ((sub >> bit) & 1) == (1 if take_upper else 0)
        return jnp.where(keep, z3, moved).reshape(z.shape)
    parts = []
    for k in range(c // (2 * step)):
        src = z[2 * k * step + step:2 * (k + 1) * step] if take_upper else z[2 * k * step:2 * k * step + step]
        parts += [src, src]
    return jnp.concatenate(parts, axis=0)


def _hgrn_level_matrix(c):
    t = np.arange(c)[:, None]
    s = np.arange(c)[None, :]
    x = t ^ s
    lev = np.zeros((c, c), np.int32)
    nz = x > 0
    lev[nz] = np.floor(np.log2(x[nz])).astype(np.int32) + 1
    return np.where(s <= t, lev, -1).astype(np.int32)


def _hgrn_prompt_kernel(x_ref, nw_ref, win_ref, lb_ref, onw_ref, wout_ref, fnw_ref, tri_ref, lvl_ref,
                        xo_ref, st_ref,
                        q_s, k_s, v_s, g_s, gate_s, o_s, st_s, *, chunk, final_norm):
    ti = pl.program_id(1)
    tb = x_ref.shape[0]
    levels = int(math.log2(chunk))

    @pl.when(ti == 0)
    def _():
        st_s[...] = jnp.zeros_like(st_s)

    x = x_ref[...]
    h = _rms(x, nw_ref[...]).astype(BF16)
    q_s[...] = _silu(_dot(h, win_ref[:, 0:D_MODEL]))
    log_f, key = _hgrn_gates(_dot(h, win_ref[:, D_MODEL:2 * D_MODEL]), lb_ref[...])
    g_s[...] = log_f
    k_s[...] = key
    v_s[...] = _dot(h, win_ref[:, 2 * D_MODEL:3 * D_MODEL])
    gate_s[...] = _silu(_dot(h, win_ref[:, 3 * D_MODEL:4 * D_MODEL]))

    lvl = lvl_ref[...]
    row = lax.broadcasted_iota(jnp.int32, (chunk, HEAD_DIM), 0)

    def chunk_body(ci, carry):
        r0 = pl.multiple_of(ci * chunk, chunk)
        rows = pl.ds(r0, chunk)
        for hh in range(N_HEADS):
            cs = slice(hh * HEAD_DIM, (hh + 1) * HEAD_DIM)
            qh = q_s[rows, cs]
            kh = k_s[rows, cs]
            vh = v_s[rows, cs].astype(BF16)
            gh = g_s[rows, cs]
            g_hi = gh.astype(BF16)
            g_lo = (gh - g_hi.astype(F32)).astype(BF16)
            b = _dot(tri_ref[...], jnp.concatenate([g_hi, g_lo], axis=1))
            b = b[:, :HEAD_DIM] + b[:, HEAD_DIM:]
            b_last = b[chunk - 1:chunk, :]
            st = st_s[hh]
            o = _dot_nt((qh * jnp.exp(b)).astype(BF16), st.astype(BF16))
            k_dec = (kh * jnp.exp(b_last - b)).astype(BF16)
            st_s[hh] = st * jnp.exp(b_last) + _dot_tn(vh, k_dec)
            p = jnp.where(lvl == 0, _dot_nt(qh.astype(BF16), kh.astype(BF16)), 0.0)
            z = b
            for lev in range(1, levels + 1):
                bit = lev - 1
                edge = _rows_from_partner(z, bit, take_upper=False)
                e_l = jnp.exp2(jnp.abs(b - edge) * (-LOG2_E))
                second = ((row >> bit) & 1) == 1
                a_l = (jnp.where(second, qh, kh) * e_l).astype(BF16)
                p = jnp.where(lvl == lev, _dot_nt(a_l, a_l), p)
                if lev < levels:
                    z = _rows_from_partner(z, bit, take_upper=True)
            o = o + _dot(p.astype(BF16), vh)
            ms = jnp.mean(o * o, axis=-1, keepdims=True)
            o_s[rows, cs] = o * lax.rsqrt(ms + NORM_EPS) * onw_ref[:, cs]
        return carry

    lax.fori_loop(0, tb // chunk, chunk_body, 0)

    y = (o_s[...] * gate_s[...]).astype(BF16)
    xo = x + _dot(y, wout_ref[...])
    if final_norm:
        xo = _rms(xo, fnw_ref[...])
    xo_ref[...] = xo

    @pl.when(ti == pl.num_programs(1) - 1)
    def _():
        for hh in range(N_HEADS):
            st_ref[hh] = st_s[hh].T


def _hgrn_prompt(x, nw, win, lb, onw, wout, fnw, *, final_norm, tb=256, chunk=128):
    batch, seq, _ = x.shape
    tri = jnp.asarray(np.tril(np.ones((chunk, chunk), np.float32)), BF16)
    lvl = jnp.asarray(_hgrn_level_matrix(chunk))
    row_scratch = pltpu.VMEM((tb, D_MODEL), F32)
    x_spec = pl.BlockSpec((None, tb, D_MODEL), lambda b, t: (b, t, 0))
    return pl.pallas_call(
        functools.partial(_hgrn_prompt_kernel, chunk=chunk, final_norm=final_norm),
        grid=(batch, seq // tb),
        in_specs=[x_spec, _const_spec((1, D_MODEL)), _const_spec((D_MODEL, 4 * D_MODEL)),
                  _const_spec((1, D_MODEL)), _const_spec((1, D_MODEL)), _const_spec((D_MODEL, D_MODEL)),
                  _const_spec((1, D_MODEL)), _const_spec(tri.shape), _const_spec(lvl.shape)],
        out_specs=[x_spec, pl.BlockSpec((None, N_HEADS, HEAD_DIM, HEAD_DIM), lambda b, t: (b, 0, 0, 0))],
        out_shape=[jax.ShapeDtypeStruct((batch, seq, D_MODEL), F32),
                   jax.ShapeDtypeStruct((batch, N_HEADS, HEAD_DIM, HEAD_DIM), F32)],
        scratch_shapes=[row_scratch] * 6 + [pltpu.VMEM((N_HEADS, HEAD_DIM, HEAD_DIM), F32)],
        compiler_params=_cparams("arbitrary", "arbitrary"),
        name="hgrn_prompt",
    )(x, nw, win, lb, onw, wout, fnw, tri, lvl)


def _hgrn_sample_kernel(x_ref, s0_ref, nw_ref, win_ref, lb_ref, onw_ref, wout_ref, fnw_ref,
                        xo_ref, s_ref, q_s, k_s, v_s, f_s, o_s, *, n_tok, final_norm):
    nb = s0_ref.shape[0]
    x = x_ref[...]
    h = _rms(x, nw_ref[...]).astype(BF16)
    q_s[...] = _silu(_dot(h, win_ref[:, 0:D_MODEL]))
    log_f, key = _hgrn_gates(_dot(h, win_ref[:, D_MODEL:2 * D_MODEL]), lb_ref[...])
    f_s[...] = jnp.exp(log_f)
    k_s[...] = key
    v_s[...] = _dot(h, win_ref[:, 2 * D_MODEL:3 * D_MODEL])
    gate = _silu(_dot(h, win_ref[:, 3 * D_MODEL:4 * D_MODEL]))
    row = lax.broadcasted_iota(jnp.int32, (SAMPLE_PAD, HEAD_DIM), 0)

    def seq_body(bi, carry):
        rows = pl.ds(pl.multiple_of(bi * SAMPLE_PAD, SAMPLE_PAD), SAMPLE_PAD)
        for hh in range(N_HEADS):
            cs = slice(hh * HEAD_DIM, (hh + 1) * HEAD_DIM)
            q_t = q_s[rows, cs].T
            k_t = k_s[rows, cs].T
            f_t = f_s[rows, cs].T
            vh = v_s[rows, cs]
            st = s0_ref[bi, hh]
            o = jnp.zeros((SAMPLE_PAD, HEAD_DIM), F32)
            for t in range(n_tok):
                st = st * f_t[:, t:t + 1] + k_t[:, t:t + 1] * vh[t:t + 1, :]
                o_t = jnp.sum(st * q_t[:, t:t + 1], axis=0, keepdims=True)
                o = jnp.where(row == t, o_t, o)
            s_ref[bi, hh] = st
            ms = jnp.mean(o * o, axis=-1, keepdims=True)
            o_s[rows, cs] = o * lax.rsqrt(ms + NORM_EPS) * onw_ref[:, cs]
        return carry

    lax.fori_loop(0, nb, seq_body, 0)
    y = (o_s[...] * gate).astype(BF16)
    xo = x + _dot(y, wout_ref[...])
    if final_norm:
        xo = _rms(xo, fnw_ref[...])
    xo_ref[...] = xo


def _hgrn_sample(x, s0, layer, nw, win, lb, onw, wout, fnw, *, n_tok, final_norm, nb=8):
    n_seq = s0.shape[1]
    rows = nb * SAMPLE_PAD
    row_scratch = pltpu.VMEM((rows, D_MODEL), F32)
    return pl.pallas_call(
        functools.partial(_hgrn_sample_kernel, n_tok=n_tok, final_norm=final_norm),
        grid=(n_seq // nb,),
        in_specs=[pl.BlockSpec((rows, D_MODEL), lambda i: (i, 0)),
                  pl.BlockSpec((None, nb, N_HEADS, HEAD_DIM, HEAD_DIM), lambda i: (layer, i, 0, 0, 0)),
                  _const_spec((1, D_MODEL)), _const_spec((D_MODEL, 4 * D_MODEL)), _const_spec((1, D_MODEL)),
                  _const_spec((1, D_MODEL)), _const_spec((D_MODEL, D_MODEL)), _const_spec((1, D_MODEL))],
        out_specs=[pl.BlockSpec((rows, D_MODEL), lambda i: (i, 0)),
                   pl.BlockSpec((nb, N_HEADS, HEAD_DIM, HEAD_DIM), lambda i: (i, 0, 0, 0))],
        out_shape=[jax.ShapeDtypeStruct(x.shape, F32), jax.ShapeDtypeStruct(s0.shape[1:], F32)],
        scratch_shapes=[row_scratch] * 5,
        compiler_params=_cparams("arbitrary"),
        name="hgrn_sample",
    )(x, s0, nw, win, lb, onw, wout, fnw)


def _attn_proj_kernel(x_ref, cos_ref, sin_ref, nw_ref, win_ref,
                      g0_ref, g1_ref, g2_ref, gate_ref, kv0_ref, kv1_ref, kv2_ref):
    tb = x_ref.shape[0]
    h = _rms(x_ref[...], nw_ref[...]).astype(BF16)
    cos = cos_ref[...]
    sin = sin_ref[...]

    def heads(a):
        return [a[:, hh * HEAD_DIM:(hh + 1) * HEAD_DIM] for hh in range(N_HEADS)]

    def rope(cols):
        return [ah * cos + pltpu.roll(ah, HEAD_DIM // 2, 1) * sin for ah in cols]

    def token_tiles(cols, keep):
        return jnp.swapaxes(jnp.stack([c[tb - keep:, :] for c in cols], axis=0), 0, 1)

    for gi, (qkv_ref, kv_ref) in enumerate(((g0_ref, kv0_ref), (g1_ref, kv1_ref), (g2_ref, kv2_ref))):
        base = 3 * gi * D_MODEL
        dil = qkv_ref.shape[0]
        q_cols = rope(heads(_dot(h, win_ref[:, base:base + D_MODEL])))
        k_cols = rope(heads(_dot(h, win_ref[:, base + D_MODEL:base + 2 * D_MODEL])))
        v = _dot(h, win_ref[:, base + 2 * D_MODEL:base + 3 * D_MODEL])
        q = jnp.concatenate(q_cols, axis=1) * (HEAD_DIM ** -0.5)
        k = jnp.concatenate(k_cols, axis=1)
        for ci, val in enumerate((q, k, v)):
            if dil == 1:
                val = val.reshape(1, tb, D_MODEL)
            else:
                val = jnp.swapaxes(val.reshape(tb // dil, dil, D_MODEL), 0, 1)
            qkv_ref[:, :, ci * D_MODEL:(ci + 1) * D_MODEL] = val.astype(BF16)
        keep = kv_ref.shape[0]
        kv_ref[:, 0] = token_tiles(k_cols, keep)
        kv_ref[:, 1] = token_tiles(heads(v), keep)
    gate_ref[...] = _silu(_dot(h, win_ref[:, 9 * D_MODEL:10 * D_MODEL]))


def _attn_proj(x, cos, sin, nw, win, *, dils, keeps, tb):
    batch, seq, _ = x.shape
    qkv_specs, qkv_shapes, kv_specs, kv_shapes = [], [], [], []
    for dil, keep in zip(dils, keeps):
        qkv_specs.append(pl.BlockSpec((None, dil, tb // dil, 3 * D_MODEL), lambda b, t: (b, 0, t, 0)))
        qkv_shapes.append(jax.ShapeDtypeStruct((batch, dil, seq // dil, 3 * D_MODEL), BF16))
        kb = min(tb, keep)
        first = (seq - keep) // tb if keep >= tb else 0
        kv_specs.append(pl.BlockSpec(
            (None, kb, 2, N_HEADS, HEAD_DIM),
            lambda b, t, first=first, kb=kb: (b, jnp.maximum(t - first, 0) if kb == tb else 0, 0, 0, 0)))
        kv_shapes.append(jax.ShapeDtypeStruct((batch, keep, 2, N_HEADS, HEAD_DIM), F32))
    return pl.pallas_call(
        _attn_proj_kernel,
        grid=(batch, seq // tb),
        in_specs=[pl.BlockSpec((None, tb, D_MODEL), lambda b, t: (b, t, 0)),
                  pl.BlockSpec((tb, HEAD_DIM), lambda b, t: (t, 0)),
                  pl.BlockSpec((tb, HEAD_DIM), lambda b, t: (t, 0)),
                  _const_spec((1, D_MODEL)), _const_spec((D_MODEL, 10 * D_MODEL))],
        out_specs=qkv_specs + [pl.BlockSpec((None, tb, D_MODEL), lambda b, t: (b, t, 0))] + kv_specs,
        out_shape=qkv_shapes + [jax.ShapeDtypeStruct((batch, seq, D_MODEL), F32)] + kv_shapes,
        compiler_params=_cparams("arbitrary", "arbitrary"),
        name="attn_proj",
    )(x, cos, sin, nw, win)


def _attn_group_kernel(q_ref, kc_ref, kp_ref, vc_ref, vp_ref, o_ref, lse_ref):
    j = pl.program_id(2)
    a = lax.broadcasted_iota(jnp.int32, (ATTN_BLOCK, ATTN_BLOCK), 0)
    c = lax.broadcasted_iota(jnp.int32, (ATTN_BLOCK, ATTN_BLOCK), 1)
    mask_cur = c <= a
    mask_prev = (c >= a) & (j > 0)
    lane = lax.broadcasted_iota(jnp.int32, (ATTN_BLOCK, HEAD_DIM), 1)
    lse_tile = jnp.zeros((ATTN_BLOCK, HEAD_DIM), F32)
    ones_col = jnp.where(lane == 0, 1.0, 0.0).astype(BF16)
    for hh in range(N_HEADS):
        cs = slice(hh * HEAD_DIM, (hh + 1) * HEAD_DIM)
        qh = q_ref[:, cs]
        s_c = jnp.where(mask_cur, _dot_nt(qh, kc_ref[:, cs]), NEG_BIG)
        s_p = jnp.where(mask_prev, _dot_nt(qh, kp_ref[:, cs]), NEG_BIG)
        m = jnp.max(jnp.maximum(s_c, s_p), axis=1, keepdims=True)
        p_c = jnp.exp(s_c - m).astype(BF16)
        p_p = jnp.exp(s_p - m).astype(BF16)
        ov = (_dot(p_c, jnp.concatenate([vc_ref[:, cs], ones_col], axis=1))
              + _dot(p_p, jnp.concatenate([vp_ref[:, cs], ones_col], axis=1)))
        den = ov[:, HEAD_DIM:HEAD_DIM + 1]
        o_ref[:, cs] = ov[:, :HEAD_DIM] / den
        lse_tile = jnp.where(lane == hh, m + jnp.log(den), lse_tile)
    lse_ref[...] = lse_tile


def _attn_group(qkv):
    batch, dil, n, _ = qkv.shape

    def spec(col, prev):
        if prev:
            return pl.BlockSpec((None, None, ATTN_BLOCK, D_MODEL), lambda b, r, j: (b, r, jnp.maximum(j - 1, 0), col))
        return pl.BlockSpec((None, None, ATTN_BLOCK, D_MODEL), lambda b, r, j: (b, r, j, col))

    return pl.pallas_call(
        _attn_group_kernel,
        grid=(batch, dil, n // ATTN_BLOCK),
        in_specs=[spec(0, False), spec(1, False), spec(1, True), spec(2, False), spec(2, True)],
        out_specs=[pl.BlockSpec((None, None, ATTN_BLOCK, D_MODEL), lambda b, r, j: (b, r, j, 0)),
                   pl.BlockSpec((None, None, ATTN_BLOCK, HEAD_DIM), lambda b, r, j: (b, r, j, 0))],
        out_shape=[jax.ShapeDtypeStruct((batch, dil, n, D_MODEL), F32),
                   jax.ShapeDtypeStruct((batch, dil, n, HEAD_DIM), F32)],
        compiler_params=_cparams("arbitrary", "arbitrary", "arbitrary"),
        name=f"attn_group_d{dil}",
    )(qkv, qkv, qkv, qkv, qkv)


def _merge_heads(outs, lses):
    m = functools.reduce(jnp.maximum, lses)
    es = [jnp.exp(l - m) for l in lses]
    den = functools.reduce(jnp.add, es)
    ws = [e / den for e in es]
    cols = []
    for hh in range(N_HEADS):
        cs = slice(hh * HEAD_DIM, (hh + 1) * HEAD_DIM)
        cols.append(functools.reduce(jnp.add, [w[:, hh:hh + 1] * o[:, cs] for w, o in zip(ws, outs)]))
    return jnp.concatenate(cols, axis=1)


def _attn_out_kernel(o0_ref, o1_ref, o2_ref, l0_ref, l1_ref, l2_ref, gate_ref, x_ref, wout_ref, xo_ref):
    def natural(ref):
        dil, n, w = ref.shape
        return ref[0] if dil == 1 else jnp.swapaxes(ref[...], 0, 1).reshape(dil * n, w)

    o = _merge_heads([natural(r) for r in (o0_ref, o1_ref, o2_ref)], [natural(r) for r in (l0_ref, l1_ref, l2_ref)])
    y = (o * gate_ref[...]).astype(BF16)
    xo_ref[...] = x_ref[...] + _dot(y, wout_ref[...])


def _attn_out(outs, lses, gate, x, wout, *, tb):
    batch, seq, _ = x.shape
    regrouped = lambda a: pl.BlockSpec((None, a.shape[1], tb // a.shape[1], a.shape[3]), lambda b, t: (b, 0, t, 0))
    row = pl.BlockSpec((None, tb, D_MODEL), lambda b, t: (b, t, 0))
    return pl.pallas_call(
        _attn_out_kernel,
        grid=(batch, seq // tb),
        in_specs=[regrouped(a) for a in outs] + [regrouped(a) for a in lses] + [row, row, _const_spec((D_MODEL, D_MODEL))],
        out_specs=row,
        out_shape=jax.ShapeDtypeStruct((batch, seq, D_MODEL), F32),
        compiler_params=_cparams("arbitrary", "arbitrary"),
        name="attn_out",
    )(*outs, *lses, gate, x, wout)


def _attn_sample_kernel(q0_ref, q1_ref, q2_ref, n0_ref, n1_ref, n2_ref, c0_ref, c1_ref, c2_ref,
                        gate_ref, x_ref, wout_ref, xo_ref, o_s, *, n_tok):
    nb = c0_ref.shape[0]
    pos_i = lax.broadcasted_iota(jnp.int32, (ATTN_BLOCK, N_HEADS, 1), 0)
    row_t = lax.broadcasted_iota(jnp.int32, (SAMPLE_PAD, D_MODEL), 0)
    q_refs, new_refs, cache_refs = (q0_ref, q1_ref, q2_ref), (n0_ref, n1_ref, n2_ref), (c0_ref, c1_ref, c2_ref)

    def seq_body(bi, carry):
        r0 = pl.multiple_of(bi * SAMPLE_PAD, SAMPLE_PAD)
        q_rows = [q_ref[pl.ds(r0, SAMPLE_PAD), :].astype(F32) for q_ref in q_refs]
        o_tile = jnp.zeros((SAMPLE_PAD, D_MODEL), F32)
        for t in range(n_tok):
            outs, lses = [], []
            for gi, (_, dil) in enumerate(ATTN_GROUPS):
                q_t = jnp.concatenate([q_rows[gi][t:t + 1, hh * HEAD_DIM:(hh + 1) * HEAD_DIM]
                                       for hh in range(N_HEADS)], axis=0)
                res = t % dil
                k_c = cache_refs[gi][bi, :, res, 0]
                v_c = cache_refs[gi][bi, :, res, 1]
                s_c = jnp.sum(k_c * q_t[None], axis=-1, keepdims=True)
                s_c = jnp.where(pos_i * dil + res >= t, s_c, NEG_BIG)
                new_t = [u for u in range(t + 1) if (t - u) % dil == 0]
                k_n = [new_refs[gi][r0 + u, 0] for u in new_t]
                v_n = [new_refs[gi][r0 + u, 1] for u in new_t]
                s_n = [jnp.sum(kk * q_t, axis=-1, keepdims=True) for kk in k_n]
                m = functools.reduce(jnp.maximum, s_n + [jnp.max(s_c, axis=0)])
                p_c = jnp.exp(s_c - m[None])
                p_n = [jnp.exp(s - m) for s in s_n]
                den = functools.reduce(jnp.add, p_n + [jnp.sum(p_c, axis=0)])
                acc = functools.reduce(jnp.add, [p * vv for p, vv in zip(p_n, v_n)] + [jnp.sum(p_c * v_c, axis=0)])
                outs.append(acc / den)
                lses.append(m + jnp.log(den))
            m_g = functools.reduce(jnp.maximum, lses)
            e_g = [jnp.exp(l - m_g) for l in lses]
            den_g = functools.reduce(jnp.add, e_g)
            o_t = functools.reduce(jnp.add, [e / den_g * o for e, o in zip(e_g, outs)])
            o_row = jnp.concatenate([o_t[hh:hh + 1, :] for hh in range(N_HEADS)], axis=1)
            o_tile = jnp.where(row_t == t, o_row, o_tile)
        o_s[pl.ds(r0, SAMPLE_PAD), :] = o_tile
        return carry

    lax.fori_loop(0, nb, seq_body, 0)
    y = (o_s[...] * gate_ref[...]).astype(BF16)
    xo_ref[...] = x_ref[...] + _dot(y, wout_ref[...])


def _attn_sample(qkvs, news, caches, layer, gate, x, wout, *, n_tok, nb=2):
    n_seq = caches[0].shape[1]
    rows = nb * SAMPLE_PAD
    c_views, c_specs = [], []
    for (window, dil), cache in zip(ATTN_GROUPS, caches):
        assert cache.shape[2] == window == ATTN_BLOCK * dil and (n_tok <= dil or dil == 1)
        c_views.append(cache.reshape(cache.shape[0], n_seq, ATTN_BLOCK, dil, 2, N_HEADS, HEAD_DIM))
        n_res = min(dil, n_tok)
        c_specs.append(pl.BlockSpec((None, nb, ATTN_BLOCK, n_res, 2, N_HEADS, HEAD_DIM),
                                    lambda i: (layer, i, 0, 0, 0, 0, 0)))
    row = lambda w: pl.BlockSpec((rows, w), lambda i: (i, 0))
    new_spec = pl.BlockSpec((rows, 2, N_HEADS, HEAD_DIM), lambda i: (i, 0, 0, 0))
    return pl.pallas_call(
        functools.partial(_attn_sample_kernel, n_tok=n_tok),
        grid=(n_seq // nb,),
        in_specs=[row(D_MODEL)] * 3 + [new_spec] * 3 + c_specs + [row(D_MODEL), row(D_MODEL),
                                                                  _const_spec((D_MODEL, D_MODEL))],
        out_specs=row(D_MODEL),
        out_shape=jax.ShapeDtypeStruct(x.shape, F32),
        scratch_shapes=[pltpu.VMEM((rows, D_MODEL), F32)],
        compiler_params=_cparams("arbitrary"),
        name="attn_sample",
    )(*qkvs, *news, *c_views, gate, x, wout)


ROLL_GROUPS = 4


def _cache_roll_kernel(c_ref, nxt_ref, new_ref, o_ref):
    i = pl.program_id(0)
    last = pl.num_programs(0) - 1
    g = o_ref.shape[1]
    o_ref[:, 0:g - 1] = c_ref[:, 1:g]

    @pl.when(i < last)
    def _():
        o_ref[:, g - 1:g] = nxt_ref[...]

    @pl.when(i == last)
    def _():
        o_ref[:, g - 1:g] = new_ref[...]


def _cache_roll(cache, new, layer, *, n_tok):
    _, n_seq, length = cache.shape[:3]
    n_grp = length // n_tok
    g = ROLL_GROUPS
    rows = n_tok * KV_ROWS
    c_view = cache.reshape(cache.shape[0], n_seq, n_grp, rows, HEAD_DIM)
    n_view = new.reshape(n_seq, 1, rows, HEAD_DIM)
    out = pl.pallas_call(
        _cache_roll_kernel,
        grid=(n_grp // g,),
        in_specs=[pl.BlockSpec((None, n_seq, g, rows, HEAD_DIM), lambda i: (layer, 0, i, 0, 0)),
                  pl.BlockSpec((None, n_seq, 1, rows, HEAD_DIM),
                               lambda i: (layer, 0, jnp.minimum((i + 1) * g, n_grp - 1), 0, 0)),
                  pl.BlockSpec((n_seq, 1, rows, HEAD_DIM), lambda i: (0, 0, 0, 0))],
        out_specs=pl.BlockSpec((n_seq, g, rows, HEAD_DIM), lambda i: (0, i, 0, 0)),
        out_shape=jax.ShapeDtypeStruct((n_seq, n_grp, rows, HEAD_DIM), cache.dtype),
        compiler_params=_cparams("arbitrary"),
        name="cache_roll",
    )(c_view, c_view, n_view)
    return out.reshape(cache.shape[1:])


def _gelu_tanh(y):
    return 0.5 * y * (1.0 + jnp.tanh(math.sqrt(2.0 / math.pi) * (y + 0.044715 * (y * y * y))))


def _s5_kernel(x_ref, s0_ref, nw_ref, win_ref, wb_ref, are_ref, aim_ref, wc_ref, dsk_ref, wglu_ref, bglu_ref,
               wout_ref, xo_ref, sfin_ref, bu_s, st_s, *, n_seq, tb, seq_major_io):
    ti = pl.program_id(0)
    tile = 2 * S5_NSTATE // S5_KTILES
    half = tile // 2
    lanes = 512

    @pl.when(ti == 0)
    def _():
        st_s[...] = s0_ref[...]

    if seq_major_io:
        x = jnp.swapaxes(x_ref[...], 0, 1).reshape(tb * n_seq, D_MODEL)
    else:
        x = x_ref[...]
    h = _rms(x, nw_ref[...]).astype(BF16)
    u = _dot(h, win_ref[:, 0:D_MODEL])
    gate = _silu(_dot(h, win_ref[:, D_MODEL:2 * D_MODEL]))
    ub = u.astype(BF16)
    kw = D_MODEL // S5_KTILES
    for kt in range(S5_KTILES):
        bu_s[:, kt * tile:(kt + 1) * tile] = _dot(ub[:, kt * kw:(kt + 1) * kw], wb_ref[kt])

    for sg in range(n_seq // 8):
        srow = slice(sg * 8, (sg + 1) * 8)
        for kt in range(S5_KTILES):
            for part in range(half // lanes):
                c_re = slice(kt * tile + part * lanes, kt * tile + (part + 1) * lanes)
                c_im = slice(kt * tile + half + part * lanes, kt * tile + half + (part + 1) * lanes)
                a_re = are_ref[:, c_re]
                a_im = aim_ref[:, c_re]

                def step(t, carry):
                    s_re, s_im = carry
                    rows = pl.ds(pl.multiple_of(t * n_seq + sg * 8, 8), 8)
                    n_re = a_re * s_re - a_im * s_im + bu_s[rows, c_re]
                    n_im = a_re * s_im + a_im * s_re + bu_s[rows, c_im]
                    bu_s[rows, c_re] = n_re
                    bu_s[rows, c_im] = n_im
                    return n_re, n_im

                s_re, s_im = lax.fori_loop(0, tb, step, (st_s[srow, c_re], st_s[srow, c_im]))
                st_s[srow, c_re] = s_re
                st_s[srow, c_im] = s_im

    ys = []
    for kt in range(S5_KTILES):
        ys.append(_dot(bu_s[:, kt * tile:(kt + 1) * tile].astype(BF16), wc_ref[kt]))
    y = jnp.concatenate(ys, axis=1) + dsk_ref[...] * u
    y = _gelu_tanh(y)
    y = y * jax.nn.sigmoid(_dot(y.astype(BF16), wglu_ref[...]) + bglu_ref[...])
    y = (y * gate).astype(BF16)
    xo = x + _dot(y, wout_ref[...])
    if seq_major_io:
        xo_ref[...] = jnp.swapaxes(xo.reshape(tb, n_seq, D_MODEL), 0, 1)
    else:
        xo_ref[...] = xo

    @pl.when(ti == pl.num_programs(0) - 1)
    def _():
        sfin_ref[...] = st_s[...]


def _s5_layer(x, s0, nw, win, wb, a_re, a_im, wc, dsk, wglu, bglu, wout, *, n_seq, seq, tb, seq_major_io):
    rows = tb * n_seq
    ncol = 2 * S5_NSTATE
    if seq_major_io:
        x_spec = pl.BlockSpec((n_seq, tb, D_MODEL), lambda t: (0, t, 0))
    else:
        x_spec = pl.BlockSpec((rows, D_MODEL), lambda t: (t, 0))
    return pl.pallas_call(
        functools.partial(_s5_kernel, n_seq=n_seq, tb=tb, seq_major_io=seq_major_io),
        grid=(seq // tb,),
        in_specs=[x_spec, _const_spec((n_seq, ncol)),
                  _const_spec((1, D_MODEL)), _const_spec((D_MODEL, 2 * D_MODEL)), _const_spec(wb.shape),
                  _const_spec((8, ncol)), _const_spec((8, ncol)), _const_spec(wc.shape),
                  _const_spec((1, D_MODEL)), _const_spec((D_MODEL, D_MODEL)), _const_spec((1, D_MODEL)),
                  _const_spec((D_MODEL, D_MODEL))],
        out_specs=[x_spec, pl.BlockSpec((n_seq, ncol), lambda t: (0, 0))],
        out_shape=[jax.ShapeDtypeStruct(x.shape, F32), jax.ShapeDtypeStruct((n_seq, ncol), F32)],
        scratch_shapes=[pltpu.VMEM((rows, ncol), F32), pltpu.VMEM((n_seq, ncol), F32)],
        compiler_params=_cparams("arbitrary"),
        name="s5_layer",
    )(x, s0, nw, win, wb, a_re, a_im, wc, dsk, wglu, bglu, wout)


def _s5_params(a_re, a_im, b_re, b_im, c_re, c_im, log_dt):
    lam_re = jnp.minimum(a_re, S5_MAX_RE)
    lam_im = a_im
    dt = jnp.exp(log_dt)[:, None]
    mag = jnp.exp(lam_re * dt)
    bar_re = mag * jnp.cos(lam_im * dt)
    bar_im = mag * jnp.sin(lam_im * dt)
    den = lam_re * lam_re + lam_im * lam_im
    xr = bar_re - 1.0
    coef_re = (xr * lam_re + bar_im * lam_im) / den
    coef_im = (bar_im * lam_re - xr * lam_im) / den
    bbar_re = coef_re[..., None] * b_re - coef_im[..., None] * b_im
    bbar_im = coef_re[..., None] * b_im + coef_im[..., None] * b_re
    gl = S5_GROUPS // S5_KTILES
    eye = jnp.eye(gl, dtype=F32)

    def to_cols(a):
        return a.reshape(S5_KTILES, gl * S5_STATE)

    def b_tile(bb):
        bb = bb.reshape(S5_KTILES, gl, S5_STATE, S5_GROUP_CH)
        return jnp.einsum('kgpc,gh->kgchp', bb, eye).reshape(S5_KTILES, gl * S5_GROUP_CH, gl * S5_STATE)

    def c_tile(cc):
        cc = cc.reshape(S5_KTILES, gl, S5_GROUP_CH, S5_STATE)
        return jnp.einsum('kgcp,gh->kgphc', cc, eye).reshape(S5_KTILES, gl * S5_STATE, gl * S5_GROUP_CH)

    wb = jnp.concatenate([b_tile(bbar_re), b_tile(bbar_im)], axis=2).astype(BF16)
    wc = jnp.concatenate([c_tile(c_re), -c_tile(c_im)], axis=1).astype(BF16)
    cols = lambda a: jnp.concatenate([to_cols(a), to_cols(a)], axis=1).reshape(1, -1)
    a_re_cols = jnp.broadcast_to(cols(bar_re), (8, 2 * S5_NSTATE))
    a_im_cols = jnp.broadcast_to(cols(bar_im), (8, 2 * S5_NSTATE))
    return wb, wc, a_re_cols, a_im_cols


def _s5_state_to_cols(s):
    n = s.shape[0]
    gl = S5_GROUPS // S5_KTILES
    s = s.reshape(n, S5_KTILES, gl * S5_STATE, 2)
    return jnp.moveaxis(s, 3, 2).reshape(n, 2 * S5_NSTATE)


def _s5_cols_to_state(c):
    n = c.shape[0]
    gl = S5_GROUPS // S5_KTILES
    c = c.reshape(n, S5_KTILES, 2, gl * S5_STATE)
    return jnp.moveaxis(c, 2, 3).reshape(n, S5_GROUPS, S5_STATE, 2)


def _rope_tables(pos):
    half = HEAD_DIM // 2
    inv_freq = ROPE_THETA ** (-jnp.arange(half, dtype=F32) / half)
    ang = pos[:, None] * inv_freq[None, :]
    cos, sin = jnp.cos(ang), jnp.sin(ang)
    return jnp.concatenate([cos, cos], axis=1), jnp.concatenate([-sin, sin], axis=1)


def kernel(x_prompt, x_sample, state_hgrn, cache_kv_w128, cache_kv_w512, cache_kv_w2048, state_s5,
           norm_w, final_norm_w, a_w_in, a_lb_logits, a_onorm_w, a_w_out, b_w_in, b_w_out,
           c_w_in, c_a_re, c_a_im, c_b_re, c_b_im, c_c_re, c_c_im, c_d, c_log_dt, c_w_glu, c_b_glu, c_w_out):
    batch, seq, _ = x_prompt.shape
    n_seq, n_tok, _ = x_sample.shape
    depth = norm_w.shape[0]
    caches = (cache_kv_w128, cache_kv_w512, cache_kv_w2048)
    dils = [d for _, d in ATTN_GROUPS]
    row = lambda a: a.reshape(1, -1)

    p_lb = jax.nn.softmax(a_lb_logits.astype(F32), axis=0)
    lower_bounds = jnp.cumsum(p_lb, axis=0) - p_lb[0:1]
    fnw = row(final_norm_w)

    s_rows = n_seq * SAMPLE_PAD
    xs = jnp.pad(x_sample, ((0, 0), (0, SAMPLE_PAD - n_tok), (0, 0))).reshape(s_rows, D_MODEL)
    xp = x_prompt
    pos_p = jnp.arange(seq, dtype=F32)
    pos_s = jnp.tile(jnp.pad(PAST_LEN + jnp.arange(n_tok, dtype=F32), (0, SAMPLE_PAD - n_tok)), n_seq)

    hgrn_p, hgrn_s, s5_p, s5_s = [], [], [], []
    kv_p = [[] for _ in ATTN_GROUPS]
    kv_s = [[] for _ in ATTN_GROUPS]
    for layer in range(depth):
        kind, j = layer % 3, layer // 3
        last = layer == depth - 1
        nw = row(norm_w[layer])
        if kind == 0:
            win, wout = a_w_in[j].astype(BF16), a_w_out[j].astype(BF16)
            lb, onw = row(lower_bounds[j]), row(a_onorm_w[j])
            xp, st = _hgrn_prompt(xp, nw, win, lb, onw, wout, fnw, final_norm=last)
            hgrn_p.append(st)
            xs, st = _hgrn_sample(xs, state_hgrn, j, nw, win, lb, onw, wout, fnw, n_tok=n_tok, final_norm=last)
            hgrn_s.append(st)
        elif kind == 1:
            win, wout = b_w_in[j].astype(BF16), b_w_out[j].astype(BF16)
            cos, sin = _rope_tables(pos_p)
            *qkvs, gate, kv0, kv1, kv2 = _attn_proj(xp, cos, sin, nw, win, dils=dils,
                                                    keeps=[min(w, seq) for w, _ in ATTN_GROUPS], tb=256)
            for g, kv in enumerate((kv0, kv1, kv2)):
                kv_p[g].append(kv)
            outs, lses = zip(*[_attn_group(qkv) for qkv in qkvs])
            xp = _attn_out(outs, lses, gate, xp, wout, tb=256)

            cos, sin = _rope_tables(pos_s)
            *qkvs, gate, kv0, kv1, kv2 = _attn_proj(xs.reshape(1, s_rows, D_MODEL), cos, sin, nw, win,
                                                    dils=[1] * len(dils), keeps=[s_rows] * len(dils), tb=s_rows)
            new_rows = [kv.reshape(s_rows, 2, N_HEADS, HEAD_DIM) for kv in (kv0, kv1, kv2)]
            xs = _attn_sample([q.reshape(s_rows, 3 * D_MODEL) for q in qkvs], new_rows, caches, j,
                              gate.reshape(s_rows, D_MODEL), xs, wout, n_tok=n_tok)
            news = [kv.reshape(n_seq, SAMPLE_PAD, 2, N_HEADS, HEAD_DIM)[:, :n_tok] for kv in new_rows]
            for g, (cache, new) in enumerate(zip(caches, news)):
                kv_s[g].append(_cache_roll(cache, new, j, n_tok=n_tok))
        else:
            wb, wc, a_re_cols, a_im_cols = _s5_params(c_a_re[j], c_a_im[j], c_b_re[j], c_b_im[j],
                                                      c_c_re[j], c_c_im[j], c_log_dt[j])
            wts = (nw, c_w_in[j].astype(BF16), wb, a_re_cols, a_im_cols, wc, row(c_d[j]),
                   c_w_glu[j].astype(BF16), row(c_b_glu[j]), c_w_out[j].astype(BF16))
            xp, sfin = _s5_layer(xp, jnp.zeros((batch, 2 * S5_NSTATE), F32), *wts,
                                 n_seq=batch, seq=seq, tb=32, seq_major_io=True)
            s5_p.append(_s5_cols_to_state(sfin))
            xs_tm = jnp.swapaxes(xs.reshape(n_seq, SAMPLE_PAD, D_MODEL)[:, :n_tok], 0, 1)
            xs_tm, sfin = _s5_layer(xs_tm.reshape(n_tok * n_seq, D_MODEL), _s5_state_to_cols(state_s5[j]), *wts,
                                    n_seq=n_seq, seq=n_tok, tb=n_tok, seq_major_io=False)
            s5_s.append(_s5_cols_to_state(sfin))
            xs = jnp.pad(jnp.swapaxes(xs_tm.reshape(n_tok, n_seq, D_MODEL), 0, 1),
                         ((0, 0), (0, SAMPLE_PAD - n_tok), (0, 0))).reshape(s_rows, D_MODEL)

    if depth % 3 != 1:
        raise NotImplementedError("the final norm is fused into a last HGRN2 layer")
    y_sample = xs.reshape(n_seq, SAMPLE_PAD, D_MODEL)[:, :n_tok]
    stack = lambda parts: jnp.stack(parts, axis=0)
    return (xp, y_sample, stack(hgrn_p), stack(hgrn_s),
            stack(kv_p[0]), stack(kv_s[0]), stack(kv_p[1]), stack(kv_s[1]), stack(kv_p[2]), stack(kv_s[2]),
            stack(s5_p), stack(s5_s))
```

```python
import functools
import math

import numpy as np
import jax
import jax.numpy as jnp
from jax import lax
from jax.experimental import pallas as pl
from jax.experimental.pallas import tpu as pltpu

F32 = jnp.float32
BF16 = jnp.bfloat16

D_MODEL = 1024
N_HEADS = 8
HEAD_DIM = 128
NORM_EPS = 1e-6
NEG_BIG = -1e30
PAST_LEN = 8192
ROPE_THETA = 10000.0
A_EXP_CLIP = 60.0
LOG2_E = 1.4426950408889634
ATTN_GROUPS = ((128, 1), (512, 4), (2048, 16))
ATTN_BLOCK = 128
S5_GROUPS = 64
S5_STATE = 64
S5_GROUP_CH = 16
S5_MAX_RE = -1e-4
S5_KTILES = 4
S5_NSTATE = S5_GROUPS * S5_STATE
SAMPLE_PAD = 8
KV_ROWS = 2 * N_HEADS
VMEM_LIMIT_V7X = 56 * 1024 * 1024


def _cparams(*sem):
    return pltpu.CompilerParams(dimension_semantics=sem, vmem_limit_bytes=VMEM_LIMIT_V7X)


def _const_spec(shape):
    nd = len(shape)
    return pl.BlockSpec(shape, lambda *_: (0,) * nd, pipeline_mode=pl.Buffered(1))


def _dot(a, b):
    return jnp.dot(a, b, preferred_element_type=F32)


def _dot_nt(a, b):
    return lax.dot_general(a, b, (((1,), (1,)), ((), ())), preferred_element_type=F32)


def _dot_tn(a, b):
    return lax.dot_general(a, b, (((0,), (0,)), ((), ())), preferred_element_type=F32)


def _rms(x, w):
    ms = jnp.mean(x * x, axis=-1, keepdims=True)
    return x * lax.rsqrt(ms + NORM_EPS) * w


def _silu(x):
    return x * jax.nn.sigmoid(x)


def _hgrn_gates(zf, lb):
    ez = jnp.exp(-jnp.abs(zf))
    one_ez = 1.0 + ez
    log_sig = jnp.minimum(zf, 0.0) - jnp.log(one_ez)
    log_f = log_sig + jnp.log(1.0 + lb * jnp.exp(jnp.minimum(-zf, A_EXP_CLIP)))
    r = 1.0 / one_ez
    key = (1.0 - lb) * jnp.where(zf >= 0.0, ez * r, r)
    return log_f, key


SUBLANES = 8


def _rows_from_partner(z, bit, take_upper):
    c = z.shape[0]
    step = 1 << bit
    if step < SUBLANES:
        z3 = z.reshape(c // SUBLANES, SUBLANES, z.shape[1])
        sub = lax.broadcasted_iota(jnp.int32, z3.shape, 1)
        shift = SUBLANES - step if take_upper else step
        moved = pltpu.roll(z3, shift, 1)
        keep = ((sub >> bit) & 1) == (1 if take_upper else 0)
        return jnp.where(keep, z3, moved).reshape(z.shape)
    parts = []
    for k in range(c // (2 * step)):
        src = z[2 * k * step + step:2 * (k + 1) * step] if take_upper else z[2 * k * step:2 * k * step + step]
        parts += [src, src]
    return jnp.concatenate(parts, axis=0)


def _hgrn_level_matrix(c):
    t = np.arange(c)[:, None]
    s = np.arange(c)[None, :]
    x = t ^ s
    lev = np.zeros((c, c), np.int32)
    nz = x > 0
    lev[nz] = np.floor(np.log2(x[nz])).astype(np.int32) + 1
    return np.where(s <= t, lev, -1).astype(np.int32)


def _hgrn_prompt_kernel(x_ref, nw_ref, win_ref, lb_ref, onw_ref, wout_ref, fnw_ref, tri_ref, lvl_ref,
                        xo_ref, st_ref,
                        q_s, k_s, v_s, g_s, gate_s, o_s, st_s, *, chunk, final_norm):
    ti = pl.program_id(1)
    tb = x_ref.shape[0]
    levels = int(math.log2(chunk))

    @pl.when(ti == 0)
    def _():
        st_s[...] = jnp.zeros_like(st_s)

    x = x_ref[...]
    h = _rms(x, nw_ref[...]).astype(BF16)
    q_s[...] = _silu(_dot(h, win_ref[:, 0:D_MODEL]))
    log_f, key = _hgrn_gates(_dot(h, win_ref[:, D_MODEL:2 * D_MODEL]), lb_ref[...])
    g_s[...] = log_f
    k_s[...] = key
    v_s[...] = _dot(h, win_ref[:, 2 * D_MODEL:3 * D_MODEL])
    gate_s[...] = _silu(_dot(h, win_ref[:, 3 * D_MODEL:4 * D_MODEL]))

    def chunk_body(ci, carry):
        r0 = pl.multiple_of(ci * chunk, chunk)
        rows = pl.ds(r0, chunk)
        for hh in range(N_HEADS):
            cs = slice(hh * HEAD_DIM, (hh + 1) * HEAD_DIM)
            qh = q_s[rows, cs]
            kh = k_s[rows, cs]
            vh = v_s[rows, cs].astype(BF16)
            gh = g_s[rows, cs]
            g_hi = gh.astype(BF16)
            g_lo = (gh - g_hi.astype(F32)).astype(BF16)
            b = _dot(tri_ref[...], jnp.concatenate([g_hi, g_lo], axis=1))
            b = b[:, :HEAD_DIM] + b[:, HEAD_DIM:]
            b_last = b[chunk - 1:chunk, :]
            st = st_s[hh]
            o = _dot_nt((qh * jnp.exp(b)).astype(BF16), st.astype(BF16))
            k_dec = (kh * jnp.exp(b_last - b)).astype(BF16)
            st_s[hh] = st * jnp.exp(b_last) + _dot_tn(vh, k_dec)
            p = _dot_nt(qh.astype(BF16), kh.astype(BF16)).astype(BF16) * lvl_ref[0]
            z = b
            for lev in range(1, levels + 1):
                bit = lev - 1
                edge = _rows_from_partner(z, bit, take_upper=False)
                e_l = jnp.exp2(jnp.abs(b - edge) * (-LOG2_E))
                sc = _dot_nt((qh * e_l).astype(BF16), (kh * e_l).astype(BF16))
                p = p + sc.astype(BF16) * lvl_ref[lev]
                if lev < levels:
                    z = _rows_from_partner(z, bit, take_upper=True)
            o = o + _dot(p, vh)
            ms = jnp.mean(o * o, axis=-1, keepdims=True)
            o_s[rows, cs] = o * lax.rsqrt(ms + NORM_EPS) * onw_ref[:, cs]
        return carry

    lax.fori_loop(0, tb // chunk, chunk_body, 0)

    y = (o_s[...] * gate_s[...]).astype(BF16)
    xo = x + _dot(y, wout_ref[...])
    if final_norm:
        xo = _rms(xo, fnw_ref[...])
    xo_ref[...] = xo

    @pl.when(ti == pl.num_programs(1) - 1)
    def _():
        for hh in range(N_HEADS):
            st_ref[hh] = st_s[hh].T


def _hgrn_prompt(x, nw, win, lb, onw, wout, fnw, *, final_norm, tb=512, chunk=128):
    batch, seq, _ = x.shape
    tri = jnp.asarray(np.tril(np.ones((chunk, chunk), np.float32)), BF16)
    lvl_idx = _hgrn_level_matrix(chunk)
    lvl = jnp.asarray(np.stack([lvl_idx == lev for lev in range(int(math.log2(chunk)) + 1)]).astype(np.float32), BF16)
    row_scratch = pltpu.VMEM((tb, D_MODEL), F32)
    x_spec = pl.BlockSpec((None, tb, D_MODEL), lambda b, t: (b, t, 0))
    return pl.pallas_call(
        functools.partial(_hgrn_prompt_kernel, chunk=chunk, final_norm=final_norm),
        grid=(batch, seq // tb),
        in_specs=[x_spec, _const_spec((1, D_MODEL)), _const_spec((D_MODEL, 4 * D_MODEL)),
                  _const_spec((1, D_MODEL)), _const_spec((1, D_MODEL)), _const_spec((D_MODEL, D_MODEL)),
                  _const_spec((1, D_MODEL)), _const_spec(tri.shape), _const_spec(lvl.shape)],
        out_specs=[x_spec, pl.BlockSpec((None, N_HEADS, HEAD_DIM, HEAD_DIM), lambda b, t: (b, 0, 0, 0))],
        out_shape=[jax.ShapeDtypeStruct((batch, seq, D_MODEL), F32),
                   jax.ShapeDtypeStruct((batch, N_HEADS, HEAD_DIM, HEAD_DIM), F32)],
        scratch_shapes=[row_scratch] * 6 + [pltpu.VMEM((N_HEADS, HEAD_DIM, HEAD_DIM), F32)],
        compiler_params=_cparams("arbitrary", "arbitrary"),
        name="hgrn_prompt",
    )(x, nw, win, lb, onw, wout, fnw, tri, lvl)


def _hgrn_sample_kernel(x_ref, s0_ref, nw_ref, win_ref, lb_ref, onw_ref, wout_ref, fnw_ref,
                        xo_ref, s_ref, q_s, k_s, v_s, f_s, o_s, *, n_tok, final_norm):
    nb = s0_ref.shape[0]
    x = x_ref[...]
    h = _rms(x, nw_ref[...]).astype(BF16)
    q_s[...] = _silu(_dot(h, win_ref[:, 0:D_MODEL]))
    log_f, key = _hgrn_gates(_dot(h, win_ref[:, D_MODEL:2 * D_MODEL]), lb_ref[...])
    f_s[...] = jnp.exp(log_f)
    k_s[...] = key
    v_s[...] = _dot(h, win_ref[:, 2 * D_MODEL:3 * D_MODEL])
    gate = _silu(_dot(h, win_ref[:, 3 * D_MODEL:4 * D_MODEL]))
    row = lax.broadcasted_iota(jnp.int32, (SAMPLE_PAD, HEAD_DIM), 0)

    valid = row < n_tok

    def seq_body(bi, carry):
        rows = pl.ds(pl.multiple_of(bi * SAMPLE_PAD, SAMPLE_PAD), SAMPLE_PAD)
        for hh in range(N_HEADS):
            cs = slice(hh * HEAD_DIM, (hh + 1) * HEAD_DIM)
            qh, kh, vh = q_s[rows, cs], k_s[rows, cs], v_s[rows, cs]
            fm = jnp.where(valid, f_s[rows, cs], 1.0)
            run = [None, fm]
            for d in range(1, n_tok):
                run.append(run[d] * jnp.where(row >= d, pltpu.roll(fm, d, 0), 1.0))
            st = s0_ref[bi, hh]
            o = _dot((qh * run[n_tok]).astype(BF16), st.astype(BF16))
            o = o + jnp.sum(qh * kh, axis=-1, keepdims=True) * vh
            for d in range(1, n_tok):
                s_d = jnp.sum(qh * pltpu.roll(kh, d, 0) * run[d], axis=-1, keepdims=True)
                o = o + jnp.where(row >= d, s_d, 0.0) * pltpu.roll(vh, d, 0)
            after = jnp.ones_like(fm)
            for d in range(1, n_tok):
                after = after * pltpu.roll(fm, SAMPLE_PAD - d, 0)
            k_dec = jnp.where(valid, kh * after, 0.0)
            total = run[n_tok].T[:, n_tok - 1:n_tok]
            s_ref[bi, hh] = st * total + _dot_tn(k_dec, vh)
            ms = jnp.mean(o * o, axis=-1, keepdims=True)
            o_s[rows, cs] = o * lax.rsqrt(ms + NORM_EPS) * onw_ref[:, cs]
        return carry

    lax.fori_loop(0, nb, seq_body, 0)
    y = (o_s[...] * gate).astype(BF16)
    xo = x + _dot(y, wout_ref[...])
    if final_norm:
        xo = _rms(xo, fnw_ref[...])
    xo_ref[...] = xo


def _hgrn_sample(x, s0, layer, nw, win, lb, onw, wout, fnw, *, n_tok, final_norm, nb=8):
    n_seq = s0.shape[1]
    assert 2 * n_tok - 1 <= SAMPLE_PAD
    rows = nb * SAMPLE_PAD
    row_scratch = pltpu.VMEM((rows, D_MODEL), F32)
    return pl.pallas_call(
        functools.partial(_hgrn_sample_kernel, n_tok=n_tok, final_norm=final_norm),
        grid=(n_seq // nb,),
        in_specs=[pl.BlockSpec((rows, D_MODEL), lambda i: (i, 0)),
                  pl.BlockSpec((None, nb, N_HEADS, HEAD_DIM, HEAD_DIM), lambda i: (layer, i, 0, 0, 0)),
                  _const_spec((1, D_MODEL)), _const_spec((D_MODEL, 4 * D_MODEL)), _const_spec((1, D_MODEL)),
                  _const_spec((1, D_MODEL)), _const_spec((D_MODEL, D_MODEL)), _const_spec((1, D_MODEL))],
        out_specs=[pl.BlockSpec((rows, D_MODEL), lambda i: (i, 0)),
                   pl.BlockSpec((nb, N_HEADS, HEAD_DIM, HEAD_DIM), lambda i: (i, 0, 0, 0))],
        out_shape=[jax.ShapeDtypeStruct(x.shape, F32), jax.ShapeDtypeStruct(s0.shape[1:], F32)],
        scratch_shapes=[row_scratch] * 5,
        compiler_params=_cparams("arbitrary"),
        name="hgrn_sample",
    )(x, s0, nw, win, lb, onw, wout, fnw)


def _attn_proj_kernel(x_ref, cos_ref, sin_ref, nw_ref, win_ref,
                      g0_ref, g1_ref, g2_ref, gate_ref, kv0_ref, kv1_ref, kv2_ref):
    tb = x_ref.shape[0]
    h = _rms(x_ref[...], nw_ref[...]).astype(BF16)
    cos = cos_ref[...]
    sin = sin_ref[...]

    def heads(a):
        return [a[:, hh * HEAD_DIM:(hh + 1) * HEAD_DIM] for hh in range(N_HEADS)]

    def rope(cols):
        return [ah * cos + pltpu.roll(ah, HEAD_DIM // 2, 1) * sin for ah in cols]

    def token_tiles(cols, keep):
        return jnp.swapaxes(jnp.stack([c[tb - keep:, :] for c in cols], axis=0), 0, 1)

    for gi, (qkv_ref, kv_ref) in enumerate(((g0_ref, kv0_ref), (g1_ref, kv1_ref), (g2_ref, kv2_ref))):
        base = 3 * gi * D_MODEL
        dil = qkv_ref.shape[0]
        q_cols = rope(heads(_dot(h, win_ref[:, base:base + D_MODEL])))
        k_cols = rope(heads(_dot(h, win_ref[:, base + D_MODEL:base + 2 * D_MODEL])))
        v = _dot(h, win_ref[:, base + 2 * D_MODEL:base + 3 * D_MODEL])
        q = jnp.concatenate(q_cols, axis=1) * (HEAD_DIM ** -0.5)
        k = jnp.concatenate(k_cols, axis=1)
        for ci, val in enumerate((q, k, v)):
            if dil == 1:
                val = val.reshape(1, tb, D_MODEL)
            else:
                val = jnp.swapaxes(val.reshape(tb // dil, dil, D_MODEL), 0, 1)
            qkv_ref[:, :, ci * D_MODEL:(ci + 1) * D_MODEL] = val.astype(BF16)
        keep = kv_ref.shape[0]
        kv_ref[:, 0] = token_tiles(k_cols, keep)
        kv_ref[:, 1] = token_tiles(heads(v), keep)
    gate_ref[...] = _silu(_dot(h, win_ref[:, 9 * D_MODEL:10 * D_MODEL])).astype(BF16)


def _attn_proj(x, cos, sin, nw, win, *, dils, keeps, tb):
    batch, seq, _ = x.shape
    qkv_specs, qkv_shapes, kv_specs, kv_shapes = [], [], [], []
    for dil, keep in zip(dils, keeps):
        qkv_specs.append(pl.BlockSpec((None, dil, tb // dil, 3 * D_MODEL), lambda b, t: (b, 0, t, 0)))
        qkv_shapes.append(jax.ShapeDtypeStruct((batch, dil, seq // dil, 3 * D_MODEL), BF16))
        kb = min(tb, keep)
        first = (seq - keep) // tb if keep >= tb else 0
        kv_specs.append(pl.BlockSpec(
            (None, kb, 2, N_HEADS, HEAD_DIM),
            lambda b, t, first=first, kb=kb: (b, jnp.maximum(t - first, 0) if kb == tb else 0, 0, 0, 0)))
        kv_shapes.append(jax.ShapeDtypeStruct((batch, keep, 2, N_HEADS, HEAD_DIM), F32))
    return pl.pallas_call(
        _attn_proj_kernel,
        grid=(batch, seq // tb),
        in_specs=[pl.BlockSpec((None, tb, D_MODEL), lambda b, t: (b, t, 0)),
                  pl.BlockSpec((tb, HEAD_DIM), lambda b, t: (t, 0)),
                  pl.BlockSpec((tb, HEAD_DIM), lambda b, t: (t, 0)),
                  _const_spec((1, D_MODEL)), _const_spec((D_MODEL, 10 * D_MODEL))],
        out_specs=qkv_specs + [pl.BlockSpec((None, tb, D_MODEL), lambda b, t: (b, t, 0))] + kv_specs,
        out_shape=qkv_shapes + [jax.ShapeDtypeStruct((batch, seq, D_MODEL), BF16)] + kv_shapes,
        compiler_params=_cparams("arbitrary", "arbitrary"),
        name="attn_proj",
    )(x, cos, sin, nw, win)


def _attn_group_kernel(q_ref, kc_ref, kp_ref, vc_ref, vp_ref, o_ref, lse_ref):
    j = pl.program_id(2)
    a = lax.broadcasted_iota(jnp.int32, (ATTN_BLOCK, ATTN_BLOCK), 0)
    c = lax.broadcasted_iota(jnp.int32, (ATTN_BLOCK, ATTN_BLOCK), 1)
    mask_cur = c <= a
    mask_prev = (c >= a) & (j > 0)
    lane = lax.broadcasted_iota(jnp.int32, (ATTN_BLOCK, HEAD_DIM), 1)
    lse_tile = jnp.zeros((ATTN_BLOCK, HEAD_DIM), F32)
    ones_col = jnp.where(lane == 0, 1.0, 0.0).astype(BF16)
    for hh in range(N_HEADS):
        cs = slice(hh * HEAD_DIM, (hh + 1) * HEAD_DIM)
        qh = q_ref[:, cs]
        s_c = jnp.where(mask_cur, _dot_nt(qh, kc_ref[:, cs]), NEG_BIG)
        s_p = jnp.where(mask_prev, _dot_nt(qh, kp_ref[:, cs]), NEG_BIG)
        m = jnp.max(jnp.maximum(s_c, s_p), axis=1, keepdims=True)
        p_c = jnp.exp(s_c - m).astype(BF16)
        p_p = jnp.exp(s_p - m).astype(BF16)
        ov = (_dot(p_c, jnp.concatenate([vc_ref[:, cs], ones_col], axis=1))
              + _dot(p_p, jnp.concatenate([vp_ref[:, cs], ones_col], axis=1)))
        den = ov[:, HEAD_DIM:HEAD_DIM + 1]
        o_ref[:, cs] = (ov[:, :HEAD_DIM] / den).astype(BF16)
        lse_tile = jnp.where(lane == hh, m + jnp.log(den), lse_tile)
    lse_ref[...] = lse_tile


def _attn_group(qkv):
    batch, dil, n, _ = qkv.shape

    def spec(col, prev):
        if prev:
            return pl.BlockSpec((None, None, ATTN_BLOCK, D_MODEL), lambda b, r, j: (b, r, jnp.maximum(j - 1, 0), col))
        return pl.BlockSpec((None, None, ATTN_BLOCK, D_MODEL), lambda b, r, j: (b, r, j, col))

    return pl.pallas_call(
        _attn_group_kernel,
        grid=(batch, dil, n // ATTN_BLOCK),
        in_specs=[spec(0, False), spec(1, False), spec(1, True), spec(2, False), spec(2, True)],
        out_specs=[pl.BlockSpec((None, None, ATTN_BLOCK, D_MODEL), lambda b, r, j: (b, r, j, 0)),
                   pl.BlockSpec((None, None, ATTN_BLOCK, HEAD_DIM), lambda b, r, j: (b, r, j, 0))],
        out_shape=[jax.ShapeDtypeStruct((batch, dil, n, D_MODEL), BF16),
                   jax.ShapeDtypeStruct((batch, dil, n, HEAD_DIM), F32)],
        compiler_params=_cparams("arbitrary", "arbitrary", "arbitrary"),
        name=f"attn_group_d{dil}",
    )(qkv, qkv, qkv, qkv, qkv)


def _merge_heads(outs, lses):
    m = functools.reduce(jnp.maximum, lses)
    es = [jnp.exp(l - m) for l in lses]
    den = functools.reduce(jnp.add, es)
    ws = [e / den for e in es]
    cols = []
    for hh in range(N_HEADS):
        cs = slice(hh * HEAD_DIM, (hh + 1) * HEAD_DIM)
        cols.append(functools.reduce(jnp.add, [w[:, hh:hh + 1] * o[:, cs] for w, o in zip(ws, outs)]))
    return jnp.concatenate(cols, axis=1)


def _attn_out_kernel(o0_ref, o1_ref, o2_ref, l0_ref, l1_ref, l2_ref, gate_ref, x_ref, wout_ref, xo_ref):
    def natural(ref):
        dil, n, w = ref.shape
        val = ref[...].astype(F32)
        return val[0] if dil == 1 else jnp.swapaxes(val, 0, 1).reshape(dil * n, w)

    o = _merge_heads([natural(r) for r in (o0_ref, o1_ref, o2_ref)], [natural(r) for r in (l0_ref, l1_ref, l2_ref)])
    y = (o * gate_ref[...]).astype(BF16)
    xo_ref[...] = x_ref[...] + _dot(y, wout_ref[...])


def _attn_out(outs, lses, gate, x, wout, *, tb):
    batch, seq, _ = x.shape
    regrouped = lambda a: pl.BlockSpec((None, a.shape[1], tb // a.shape[1], a.shape[3]), lambda b, t: (b, 0, t, 0))
    row = pl.BlockSpec((None, tb, D_MODEL), lambda b, t: (b, t, 0))
    return pl.pallas_call(
        _attn_out_kernel,
        grid=(batch, seq // tb),
        in_specs=[regrouped(a) for a in outs] + [regrouped(a) for a in lses] + [row, row, _const_spec((D_MODEL, D_MODEL))],
        out_specs=row,
        out_shape=jax.ShapeDtypeStruct((batch, seq, D_MODEL), F32),
        compiler_params=_cparams("arbitrary", "arbitrary"),
        name="attn_out",
    )(*outs, *lses, gate, x, wout)


def _attn_sample_kernel(q0_ref, q1_ref, q2_ref, n0_ref, n1_ref, n2_ref, c0_ref, c1_ref, c2_ref,
                        gate_ref, x_ref, wout_ref, xo_ref, o_s, *, n_tok):
    nb = c0_ref.shape[0]
    pos_i = lax.broadcasted_iota(jnp.int32, (ATTN_BLOCK, N_HEADS, 1), 0)
    row_t = lax.broadcasted_iota(jnp.int32, (SAMPLE_PAD, D_MODEL), 0)
    q_refs, new_refs, cache_refs = (q0_ref, q1_ref, q2_ref), (n0_ref, n1_ref, n2_ref), (c0_ref, c1_ref, c2_ref)

    def seq_body(bi, carry):
        r0 = pl.multiple_of(bi * SAMPLE_PAD, SAMPLE_PAD)
        q_rows = [q_ref[pl.ds(r0, SAMPLE_PAD), :].astype(F32) for q_ref in q_refs]
        o_tile = jnp.zeros((SAMPLE_PAD, D_MODEL), F32)
        for t in range(n_tok):
            outs, lses = [], []
            for gi, (_, dil) in enumerate(ATTN_GROUPS):
                q_t = jnp.concatenate([q_rows[gi][t:t + 1, hh * HEAD_DIM:(hh + 1) * HEAD_DIM]
                                       for hh in range(N_HEADS)], axis=0)
                res = t % dil
                k_c = cache_refs[gi][bi, :, res, 0]
                v_c = cache_refs[gi][bi, :, res, 1]
                s_c = jnp.sum(k_c * q_t[None], axis=-1, keepdims=True)
                s_c = jnp.where(pos_i * dil + res >= t, s_c, NEG_BIG)
                new_t = [u for u in range(t + 1) if (t - u) % dil == 0]
                k_n = [new_refs[gi][r0 + u, 0] for u in new_t]
                v_n = [new_refs[gi][r0 + u, 1] for u in new_t]
                s_n = [jnp.sum(kk * q_t, axis=-1, keepdims=True) for kk in k_n]
                m = functools.reduce(jnp.maximum, s_n + [jnp.max(s_c, axis=0)])
                p_c = jnp.exp(s_c - m[None])
                p_n = [jnp.exp(s - m) for s in s_n]
                den = functools.reduce(jnp.add, p_n + [jnp.sum(p_c, axis=0)])
                acc = functools.reduce(jnp.add, [p * vv for p, vv in zip(p_n, v_n)] + [jnp.sum(p_c * v_c, axis=0)])
                outs.append(acc / den)
                lses.append(m + jnp.log(den))
            m_g = functools.reduce(jnp.maximum, lses)
            e_g = [jnp.exp(l - m_g) for l in lses]
            den_g = functools.reduce(jnp.add, e_g)
            o_t = functools.reduce(jnp.add, [e / den_g * o for e, o in zip(e_g, outs)])
            o_row = jnp.concatenate([o_t[hh:hh + 1, :] for hh in range(N_HEADS)], axis=1)
            o_tile = jnp.where(row_t == t, o_row, o_tile)
        o_s[pl.ds(r0, SAMPLE_PAD), :] = o_tile
        return carry

    lax.fori_loop(0, nb, seq_body, 0)
    y = (o_s[...] * gate_ref[...]).astype(BF16)
    xo_ref[...] = x_ref[...] + _dot(y, wout_ref[...])


def _attn_sample(qkvs, news, caches, layer, gate, x, wout, *, n_tok, nb=2):
    n_seq = caches[0].shape[1]
    rows = nb * SAMPLE_PAD
    c_views, c_specs = [], []
    for (window, dil), cache in zip(ATTN_GROUPS, caches):
        assert cache.shape[2] == window == ATTN_BLOCK * dil and (n_tok <= dil or dil == 1)
        c_views.append(cache.reshape(cache.shape[0], n_seq, ATTN_BLOCK, dil, 2, N_HEADS, HEAD_DIM))
        n_res = min(dil, n_tok)
        c_specs.append(pl.BlockSpec((None, nb, ATTN_BLOCK, n_res, 2, N_HEADS, HEAD_DIM),
                                    lambda i: (layer, i, 0, 0, 0, 0, 0)))
    row = lambda w: pl.BlockSpec((rows, w), lambda i: (i, 0))
    new_spec = pl.BlockSpec((rows, 2, N_HEADS, HEAD_DIM), lambda i: (i, 0, 0, 0))
    return pl.pallas_call(
        functools.partial(_attn_sample_kernel, n_tok=n_tok),
        grid=(n_seq // nb,),
        in_specs=[row(D_MODEL)] * 3 + [new_spec] * 3 + c_specs + [row(D_MODEL), row(D_MODEL),
                                                                  _const_spec((D_MODEL, D_MODEL))],
        out_specs=row(D_MODEL),
        out_shape=jax.ShapeDtypeStruct(x.shape, F32),
        scratch_shapes=[pltpu.VMEM((rows, D_MODEL), F32)],
        compiler_params=_cparams("arbitrary"),
        name="attn_sample",
    )(*qkvs, *news, *c_views, gate, x, wout)


ROLL_GROUPS = 8


def _cache_roll_kernel(c_ref, nxt_ref, new_ref, o_ref):
    i = pl.program_id(0)
    last = pl.num_programs(0) - 1
    g = o_ref.shape[1]
    o_ref[:, 0:g - 1] = c_ref[:, 1:g]

    @pl.when(i < last)
    def _():
        o_ref[:, g - 1:g] = nxt_ref[...]

    @pl.when(i == last)
    def _():
        o_ref[:, g - 1:g] = new_ref[...]


def _cache_roll(cache, new, layer, *, n_tok):
    _, n_seq, length = cache.shape[:3]
    n_grp = length // n_tok
    g = ROLL_GROUPS
    rows = n_tok * KV_ROWS
    c_view = cache.reshape(cache.shape[0], n_seq, n_grp, rows, HEAD_DIM)
    n_view = new.reshape(n_seq, 1, rows, HEAD_DIM)
    out = pl.pallas_call(
        _cache_roll_kernel,
        grid=(n_grp // g,),
        in_specs=[pl.BlockSpec((None, n_seq, g, rows, HEAD_DIM), lambda i: (layer, 0, i, 0, 0)),
                  pl.BlockSpec((None, n_seq, 1, rows, HEAD_DIM),
                               lambda i: (layer, 0, jnp.minimum((i + 1) * g, n_grp - 1), 0, 0)),
                  pl.BlockSpec((n_seq, 1, rows, HEAD_DIM), lambda i: (0, 0, 0, 0))],
        out_specs=pl.BlockSpec((n_seq, g, rows, HEAD_DIM), lambda i: (0, i, 0, 0)),
        out_shape=jax.ShapeDtypeStruct((n_seq, n_grp, rows, HEAD_DIM), cache.dtype),
        compiler_params=_cparams("arbitrary"),
        name="cache_roll",
    )(c_view, c_view, n_view)
    return out.reshape(cache.shape[1:])


def _gelu_tanh(y):
    return 0.5 * y * (1.0 + jnp.tanh(math.sqrt(2.0 / math.pi) * (y + 0.044715 * (y * y * y))))


def _s5_kernel(x_ref, s0_ref, nw_ref, win_ref, wb_ref, are_ref, aim_ref, wc_ref, dsk_ref, wglu_ref, bglu_ref,
               wout_ref, xo_ref, sfin_ref, bu_s, st_s, *, n_seq, tb, seq_major_io):
    ti = pl.program_id(0)
    tile = 2 * S5_NSTATE // S5_KTILES
    half = tile // 2
    lanes = 512

    @pl.when(ti == 0)
    def _():
        st_s[...] = s0_ref[...]

    if seq_major_io:
        x = jnp.swapaxes(x_ref[...], 0, 1).reshape(tb * n_seq, D_MODEL)
    else:
        x = x_ref[...]
    h = _rms(x, nw_ref[...]).astype(BF16)
    u = _dot(h, win_ref[:, 0:D_MODEL])
    gate = _silu(_dot(h, win_ref[:, D_MODEL:2 * D_MODEL]))
    ub = u.astype(BF16)
    kw = D_MODEL // S5_KTILES
    for kt in range(S5_KTILES):
        bu_s[:, kt * tile:(kt + 1) * tile] = _dot(ub[:, kt * kw:(kt + 1) * kw], wb_ref[kt])

    for sg in range(n_seq // 8):
        srow = slice(sg * 8, (sg + 1) * 8)
        for kt in range(S5_KTILES):
            for part in range(half // lanes):
                c_re = slice(kt * tile + part * lanes, kt * tile + (part + 1) * lanes)
                c_im = slice(kt * tile + half + part * lanes, kt * tile + half + (part + 1) * lanes)
                a_re = are_ref[:, c_re]
                a_im = aim_ref[:, c_re]

                def step(t, carry):
                    s_re, s_im = carry
                    rows = pl.ds(pl.multiple_of(t * n_seq + sg * 8, 8), 8)
                    n_re = a_re * s_re - a_im * s_im + bu_s[rows, c_re]
                    n_im = a_re * s_im + a_im * s_re + bu_s[rows, c_im]
                    bu_s[rows, c_re] = n_re
                    bu_s[rows, c_im] = n_im
                    return n_re, n_im

                s_re, s_im = lax.fori_loop(0, tb, step, (st_s[srow, c_re], st_s[srow, c_im]))
                st_s[srow, c_re] = s_re
                st_s[srow, c_im] = s_im

    ys = []
    for kt in range(S5_KTILES):
        ys.append(_dot(bu_s[:, kt * tile:(kt + 1) * tile].astype(BF16), wc_ref[kt]))
    y = jnp.concatenate(ys, axis=1) + dsk_ref[...] * u
    y = _gelu_tanh(y)
    y = y * jax.nn.sigmoid(_dot(y.astype(BF16), wglu_ref[...]) + bglu_ref[...])
    y = (y * gate).astype(BF16)
    xo = x + _dot(y, wout_ref[...])
    if seq_major_io:
        xo_ref[...] = jnp.swapaxes(xo.reshape(tb, n_seq, D_MODEL), 0, 1)
    else:
        xo_ref[...] = xo

    @pl.when(ti == pl.num_programs(0) - 1)
    def _():
        sfin_ref[...] = st_s[...]


def _s5_layer(x, s0, nw, win, wb, a_re, a_im, wc, dsk, wglu, bglu, wout, *, n_seq, seq, tb, seq_major_io):
    rows = tb * n_seq
    ncol = 2 * S5_NSTATE
    if seq_major_io:
        x_spec = pl.BlockSpec((n_seq, tb, D_MODEL), lambda t: (0, t, 0))
    else:
        x_spec = pl.BlockSpec((rows, D_MODEL), lambda t: (t, 0))
    return pl.pallas_call(
        functools.partial(_s5_kernel, n_seq=n_seq, tb=tb, seq_major_io=seq_major_io),
        grid=(seq // tb,),
        in_specs=[x_spec, _const_spec((n_seq, ncol)),
                  _const_spec((1, D_MODEL)), _const_spec((D_MODEL, 2 * D_MODEL)), _const_spec(wb.shape),
                  _const_spec((8, ncol)), _const_spec((8, ncol)), _const_spec(wc.shape),
                  _const_spec((1, D_MODEL)), _const_spec((D_MODEL, D_MODEL)), _const_spec((1, D_MODEL)),
                  _const_spec((D_MODEL, D_MODEL))],
        out_specs=[x_spec, pl.BlockSpec((n_seq, ncol), lambda t: (0, 0))],
        out_shape=[jax.ShapeDtypeStruct(x.shape, F32), jax.ShapeDtypeStruct((n_seq, ncol), F32)],
        scratch_shapes=[pltpu.VMEM((rows, ncol), F32), pltpu.VMEM((n_seq, ncol), F32)],
        compiler_params=_cparams("arbitrary"),
        name="s5_layer",
    )(x, s0, nw, win, wb, a_re, a_im, wc, dsk, wglu, bglu, wout)


def _s5_params(a_re, a_im, b_re, b_im, c_re, c_im, log_dt):
    lam_re = jnp.minimum(a_re, S5_MAX_RE)
    lam_im = a_im
    dt = jnp.exp(log_dt)[:, None]
    mag = jnp.exp(lam_re * dt)
    bar_re = mag * jnp.cos(lam_im * dt)
    bar_im = mag * jnp.sin(lam_im * dt)
    den = lam_re * lam_re + lam_im * lam_im
    xr = bar_re - 1.0
    coef_re = (xr * lam_re + bar_im * lam_im) / den
    coef_im = (bar_im * lam_re - xr * lam_im) / den
    bbar_re = coef_re[..., None] * b_re - coef_im[..., None] * b_im
    bbar_im = coef_re[..., None] * b_im + coef_im[..., None] * b_re
    gl = S5_GROUPS // S5_KTILES
    eye = jnp.eye(gl, dtype=F32)

    def to_cols(a):
        return a.reshape(S5_KTILES, gl * S5_STATE)

    def b_tile(bb):
        bb = bb.reshape(S5_KTILES, gl, S5_STATE, S5_GROUP_CH)
        return jnp.einsum('kgpc,gh->kgchp', bb, eye).reshape(S5_KTILES, gl * S5_GROUP_CH, gl * S5_STATE)

    def c_tile(cc):
        cc = cc.reshape(S5_KTILES, gl, S5_GROUP_CH, S5_STATE)
        return jnp.einsum('kgcp,gh->kgphc', cc, eye).reshape(S5_KTILES, gl * S5_STATE, gl * S5_GROUP_CH)

    wb = jnp.concatenate([b_tile(bbar_re), b_tile(bbar_im)], axis=2).astype(BF16)
    wc = jnp.concatenate([c_tile(c_re), -c_tile(c_im)], axis=1).astype(BF16)
    cols = lambda a: jnp.concatenate([to_cols(a), to_cols(a)], axis=1).reshape(1, -1)
    a_re_cols = jnp.broadcast_to(cols(bar_re), (8, 2 * S5_NSTATE))
    a_im_cols = jnp.broadcast_to(cols(bar_im), (8, 2 * S5_NSTATE))
    return wb, wc, a_re_cols, a_im_cols


def _s5_state_to_cols(s):
    n = s.shape[0]
    gl = S5_GROUPS // S5_KTILES
    s = s.reshape(n, S5_KTILES, gl * S5_STATE, 2)
    return jnp.moveaxis(s, 3, 2).reshape(n, 2 * S5_NSTATE)


def _s5_cols_to_state(c):
    n = c.shape[0]
    gl = S5_GROUPS // S5_KTILES
    c = c.reshape(n, S5_KTILES, 2, gl * S5_STATE)
    return jnp.moveaxis(c, 2, 3).reshape(n, S5_GROUPS, S5_STATE, 2)


def _rope_tables(pos):
    half = HEAD_DIM // 2
    inv_freq = ROPE_THETA ** (-jnp.arange(half, dtype=F32) / half)
    ang = pos[:, None] * inv_freq[None, :]
    cos, sin = jnp.cos(ang), jnp.sin(ang)
    return jnp.concatenate([cos, cos], axis=1), jnp.concatenate([-sin, sin], axis=1)


def kernel(x_prompt, x_sample, state_hgrn, cache_kv_w128, cache_kv_w512, cache_kv_w2048, state_s5,
           norm_w, final_norm_w, a_w_in, a_lb_logits, a_onorm_w, a_w_out, b_w_in, b_w_out,
           c_w_in, c_a_re, c_a_im, c_b_re, c_b_im, c_c_re, c_c_im, c_d, c_log_dt, c_w_glu, c_b_glu, c_w_out):
    batch, seq, _ = x_prompt.shape
    n_seq, n_tok, _ = x_sample.shape
    depth = norm_w.shape[0]
    caches = (cache_kv_w128, cache_kv_w512, cache_kv_w2048)
    dils = [d for _, d in ATTN_GROUPS]
    row = lambda a: a.reshape(1, -1)

    p_lb = jax.nn.softmax(a_lb_logits.astype(F32), axis=0)
    lower_bounds = jnp.cumsum(p_lb, axis=0) - p_lb[0:1]
    fnw = row(final_norm_w)

    s_rows = n_seq * SAMPLE_PAD
    xs = jnp.pad(x_sample, ((0, 0), (0, SAMPLE_PAD - n_tok), (0, 0))).reshape(s_rows, D_MODEL)
    xp = x_prompt
    pos_p = jnp.arange(seq, dtype=F32)
    pos_s = jnp.tile(jnp.pad(PAST_LEN + jnp.arange(n_tok, dtype=F32), (0, SAMPLE_PAD - n_tok)), n_seq)

    hgrn_p, hgrn_s, s5_p, s5_s = [], [], [], []
    kv_p = [[] for _ in ATTN_GROUPS]
    kv_s = [[] for _ in ATTN_GROUPS]
    for layer in range(depth):
        kind, j = layer % 3, layer // 3
        last = layer == depth - 1
        nw = row(norm_w[layer])
        if kind == 0:
            win, wout = a_w_in[j].astype(BF16), a_w_out[j].astype(BF16)
            lb, onw = row(lower_bounds[j]), row(a_onorm_w[j])
            xp, st = _hgrn_prompt(xp, nw, win, lb, onw, wout, fnw, final_norm=last)
            hgrn_p.append(st)
            xs, st = _hgrn_sample(xs, state_hgrn, j, nw, win, lb, onw, wout, fnw, n_tok=n_tok, final_norm=last)
            hgrn_s.append(st)
        elif kind == 1:
            win, wout = b_w_in[j].astype(BF16), b_w_out[j].astype(BF16)
            cos, sin = _rope_tables(pos_p)
            *qkvs, gate, kv0, kv1, kv2 = _attn_proj(xp, cos, sin, nw, win, dils=dils,
                                                    keeps=[min(w, seq) for w, _ in ATTN_GROUPS], tb=256)
            for g, kv in enumerate((kv0, kv1, kv2)):
                kv_p[g].append(kv)
            outs, lses = zip(*[_attn_group(qkv) for qkv in qkvs])
            xp = _attn_out(outs, lses, gate, xp, wout, tb=256)

            cos, sin = _rope_tables(pos_s)
            *qkvs, gate, kv0, kv1, kv2 = _attn_proj(xs.reshape(1, s_rows, D_MODEL), cos, sin, nw, win,
                                                    dils=[1] * len(dils), keeps=[s_rows] * len(dils), tb=s_rows)
            new_rows = [kv.reshape(s_rows, 2, N_HEADS, HEAD_DIM) for kv in (kv0, kv1, kv2)]
            xs = _attn_sample([q.reshape(s_rows, 3 * D_MODEL) for q in qkvs], new_rows, caches, j,
                              gate.reshape(s_rows, D_MODEL), xs, wout, n_tok=n_tok)
            news = [kv.reshape(n_seq, SAMPLE_PAD, 2, N_HEADS, HEAD_DIM)[:, :n_tok] for kv in new_rows]
            for g, (cache, new) in enumerate(zip(caches, news)):
                kv_s[g].append(_cache_roll(cache, new, j, n_tok=n_tok))
        else:
            wb, wc, a_re_cols, a_im_cols = _s5_params(c_a_re[j], c_a_im[j], c_b_re[j], c_b_im[j],
                                                      c_c_re[j], c_c_im[j], c_log_dt[j])
            wts = (nw, c_w_in[j].astype(BF16), wb, a_re_cols, a_im_cols, wc, row(c_d[j]),
                   c_w_glu[j].astype(BF16), row(c_b_glu[j]), c_w_out[j].astype(BF16))
            xp, sfin = _s5_layer(xp, jnp.zeros((batch, 2 * S5_NSTATE), F32), *wts,
                                 n_seq=batch, seq=seq, tb=32, seq_major_io=True)
            s5_p.append(_s5_cols_to_state(sfin))
            xs_tm = jnp.swapaxes(xs.reshape(n_seq, SAMPLE_PAD, D_MODEL)[:, :n_tok], 0, 1)
            xs_tm, sfin = _s5_layer(xs_tm.reshape(n_tok * n_seq, D_MODEL), _s5_state_to_cols(state_s5[j]), *wts,
                                    n_seq=n_seq, seq=n_tok, tb=n_tok, seq_major_io=False)
            s5_s.append(_s5_cols_to_state(sfin))
            xs = jnp.pad(jnp.swapaxes(xs_tm.reshape(n_tok, n_seq, D_MODEL), 0, 1),
                         ((0, 0), (0, SAMPLE_PAD - n_tok), (0, 0))).reshape(s_rows, D_MODEL)

    if depth % 3 != 1:
        raise NotImplementedError("the final norm is fused into a last HGRN2 layer")
    y_sample = xs.reshape(n_seq, SAMPLE_PAD, D_MODEL)[:, :n_tok]
    stack = lambda parts: jnp.stack(parts, axis=0)
    return (xp, y_sample, stack(hgrn_p), stack(hgrn_s),
            stack(kv_p[0]), stack(kv_s[0]), stack(kv_p[1]), stack(kv_s[1]), stack(kv_p[2]), stack(kv_s[2]),
            stack(s5_p), stack(s5_s))
```

```python
import functools
import math

import numpy as np
import jax
import jax.numpy as jnp
from jax import lax
from jax.experimental import pallas as pl
from jax.experimental.pallas import tpu as pltpu

F32 = jnp.float32
BF16 = jnp.bfloat16

D_MODEL = 1024
N_HEADS = 8
HEAD_DIM = 128
NORM_EPS = 1e-6
NEG_BIG = -1e30
PAST_LEN = 8192
ROPE_THETA = 10000.0
A_EXP_CLIP = 60.0
LOG2_E = 1.4426950408889634
ATTN_GROUPS = ((128, 1), (512, 4), (2048, 16))
ATTN_BLOCK = 128
S5_GROUPS = 64
S5_STATE = 64
S5_GROUP_CH = 16
S5_MAX_RE = -1e-4
S5_KTILES = 4
S5_NSTATE = S5_GROUPS * S5_STATE
SAMPLE_PAD = 8
KV_ROWS = 2 * N_HEADS
VMEM_LIMIT_V7X = 56 * 1024 * 1024


def _cparams(*sem):
    return pltpu.CompilerParams(dimension_semantics=sem, vmem_limit_bytes=VMEM_LIMIT_V7X)


def _const_spec(shape):
    nd = len(shape)
    return pl.BlockSpec(shape, lambda *_: (0,) * nd, pipeline_mode=pl.Buffered(1))


def _dot(a, b):
    return jnp.dot(a, b, preferred_element_type=F32)


def _dot_nt(a, b):
    return lax.dot_general(a, b, (((1,), (1,)), ((), ())), preferred_element_type=F32)


def _dot_tn(a, b):
    return lax.dot_general(a, b, (((0,), (0,)), ((), ())), preferred_element_type=F32)


def _rms(x, w):
    ms = jnp.mean(x * x, axis=-1, keepdims=True)
    return x * lax.rsqrt(ms + NORM_EPS) * w


def _silu(x):
    return x * jax.nn.sigmoid(x)


def _hgrn_gates(zf, lb):
    ez = jnp.exp(-jnp.abs(zf))
    one_ez = 1.0 + ez
    log_sig = jnp.minimum(zf, 0.0) - jnp.log(one_ez)
    log_f = log_sig + jnp.log(1.0 + lb * jnp.exp(jnp.minimum(-zf, A_EXP_CLIP)))
    r = 1.0 / one_ez
    key = (1.0 - lb) * jnp.where(zf >= 0.0, ez * r, r)
    return log_f, key


SUBLANES = 8


def _rows_from_partner(z, bit, take_upper):
    c = z.shape[0]
    step = 1 << bit
    if step < SUBLANES:
        z3 = z.reshape(c // SUBLANES, SUBLANES, z.shape[1])
        sub = lax.broadcasted_iota(jnp.int32, z3.shape, 1)
        shift = SUBLANES - step if take_upper else step
        moved = pltpu.roll(z3, shift, 1)
        keep = ((sub >> bit) & 1) == (1 if take_upper else 0)
        return jnp.where(keep, z3, moved).reshape(z.shape)
    parts = []
    for k in range(c // (2 * step)):
        src = z[2 * k * step + step:2 * (k + 1) * step] if take_upper else z[2 * k * step:2 * k * step + step]
        parts += [src, src]
    return jnp.concatenate(parts, axis=0)


def _hgrn_level_matrix(c):
    t = np.arange(c)[:, None]
    s = np.arange(c)[None, :]
    x = t ^ s
    lev = np.zeros((c, c), np.int32)
    nz = x > 0
    lev[nz] = np.floor(np.log2(x[nz])).astype(np.int32) + 1
    return np.where(s <= t, lev, -1).astype(np.int32)


def _hgrn_prompt_kernel(x_ref, nw_ref, win_ref, lb_ref, onw_ref, wout_ref, fnw_ref, tri_ref, lvl_ref,
                        xo_ref, st_ref,
                        q_s, k_s, v_s, g_s, gate_s, o_s, st_s, *, chunk, final_norm):
    ti = pl.program_id(1)
    tb = x_ref.shape[0]
    levels = int(math.log2(chunk))

    @pl.when(ti == 0)
    def _():
        st_s[...] = jnp.zeros_like(st_s)

    x = x_ref[...]
    h = _rms(x, nw_ref[...]).astype(BF16)
    q_s[...] = _silu(_dot(h, win_ref[:, 0:D_MODEL]))
    log_f, key = _hgrn_gates(_dot(h, win_ref[:, D_MODEL:2 * D_MODEL]), lb_ref[...])
    g_s[...] = log_f
    k_s[...] = key
    v_s[...] = _dot(h, win_ref[:, 2 * D_MODEL:3 * D_MODEL])
    gate_s[...] = _silu(_dot(h, win_ref[:, 3 * D_MODEL:4 * D_MODEL]))

    def chunk_body(ci, carry):
        r0 = pl.multiple_of(ci * chunk, chunk)
        rows = pl.ds(r0, chunk)
        for hh in range(N_HEADS):
            cs = slice(hh * HEAD_DIM, (hh + 1) * HEAD_DIM)
            qh = q_s[rows, cs]
            kh = k_s[rows, cs]
            vh = v_s[rows, cs].astype(BF16)
            gh = g_s[rows, cs]
            g_hi = gh.astype(BF16)
            g_lo = (gh - g_hi.astype(F32)).astype(BF16)
            b = _dot(tri_ref[...], jnp.concatenate([g_hi, g_lo], axis=1))
            b = b[:, :HEAD_DIM] + b[:, HEAD_DIM:]
            b_last = b[chunk - 1:chunk, :]
            st = st_s[hh]
            o = _dot_nt((qh * jnp.exp(b)).astype(BF16), st.astype(BF16))
            k_dec = (kh * jnp.exp(b_last - b)).astype(BF16)
            st_s[hh] = st * jnp.exp(b_last) + _dot_tn(vh, k_dec)
            p = _dot_nt(qh.astype(BF16), kh.astype(BF16)).astype(BF16) * lvl_ref[0]
            z = b
            for lev in range(1, levels + 1):
                bit = lev - 1
                edge = _rows_from_partner(z, bit, take_upper=False)
                e_l = jnp.exp2(jnp.abs(b - edge) * (-LOG2_E))
                sc = _dot_nt((qh * e_l).astype(BF16), (kh * e_l).astype(BF16))
                p = p + sc.astype(BF16) * lvl_ref[lev]
                if lev < levels:
                    z = _rows_from_partner(z, bit, take_upper=True)
            o = o + _dot(p, vh)
            ms = jnp.mean(o * o, axis=-1, keepdims=True)
            o_s[rows, cs] = o * lax.rsqrt(ms + NORM_EPS) * onw_ref[:, cs]
        return carry

    lax.fori_loop(0, tb // chunk, chunk_body, 0, unroll=2)

    y = (o_s[...] * gate_s[...]).astype(BF16)
    xo = x + _dot(y, wout_ref[...])
    if final_norm:
        xo = _rms(xo, fnw_ref[...])
    xo_ref[...] = xo

    @pl.when(ti == pl.num_programs(1) - 1)
    def _():
        for hh in range(N_HEADS):
            st_ref[hh] = st_s[hh].T


def _hgrn_prompt(x, nw, win, lb, onw, wout, fnw, *, final_norm, tb=512, chunk=128):
    batch, seq, _ = x.shape
    tri = jnp.asarray(np.tril(np.ones((chunk, chunk), np.float32)), BF16)
    lvl_idx = _hgrn_level_matrix(chunk)
    lvl = jnp.asarray(np.stack([lvl_idx == lev for lev in range(int(math.log2(chunk)) + 1)]).astype(np.float32), BF16)
    row_scratch = pltpu.VMEM((tb, D_MODEL), F32)
    x_spec = pl.BlockSpec((None, tb, D_MODEL), lambda b, t: (b, t, 0))
    return pl.pallas_call(
        functools.partial(_hgrn_prompt_kernel, chunk=chunk, final_norm=final_norm),
        grid=(batch, seq // tb),
        in_specs=[x_spec, _const_spec((1, D_MODEL)), _const_spec((D_MODEL, 4 * D_MODEL)),
                  _const_spec((1, D_MODEL)), _const_spec((1, D_MODEL)), _const_spec((D_MODEL, D_MODEL)),
                  _const_spec((1, D_MODEL)), _const_spec(tri.shape), _const_spec(lvl.shape)],
        out_specs=[x_spec, pl.BlockSpec((None, N_HEADS, HEAD_DIM, HEAD_DIM), lambda b, t: (b, 0, 0, 0))],
        out_shape=[jax.ShapeDtypeStruct((batch, seq, D_MODEL), F32),
                   jax.ShapeDtypeStruct((batch, N_HEADS, HEAD_DIM, HEAD_DIM), F32)],
        scratch_shapes=[row_scratch] * 6 + [pltpu.VMEM((N_HEADS, HEAD_DIM, HEAD_DIM), F32)],
        compiler_params=_cparams("arbitrary", "arbitrary"),
        name="hgrn_prompt",
    )(x, nw, win, lb, onw, wout, fnw, tri, lvl)


def _hgrn_sample_kernel(x_ref, s0_ref, nw_ref, win_ref, lb_ref, onw_ref, wout_ref, fnw_ref,
                        xo_ref, s_ref, q_s, k_s, v_s, f_s, o_s, *, n_tok, final_norm):
    nb = s0_ref.shape[0]
    x = x_ref[...]
    h = _rms(x, nw_ref[...]).astype(BF16)
    q_s[...] = _silu(_dot(h, win_ref[:, 0:D_MODEL]))
    log_f, key = _hgrn_gates(_dot(h, win_ref[:, D_MODEL:2 * D_MODEL]), lb_ref[...])
    f_s[...] = jnp.exp(log_f)
    k_s[...] = key
    v_s[...] = _dot(h, win_ref[:, 2 * D_MODEL:3 * D_MODEL])
    gate = _silu(_dot(h, win_ref[:, 3 * D_MODEL:4 * D_MODEL]))
    row = lax.broadcasted_iota(jnp.int32, (SAMPLE_PAD, HEAD_DIM), 0)

    valid = row < n_tok

    def seq_body(bi, carry):
        rows = pl.ds(pl.multiple_of(bi * SAMPLE_PAD, SAMPLE_PAD), SAMPLE_PAD)
        for hh in range(N_HEADS):
            cs = slice(hh * HEAD_DIM, (hh + 1) * HEAD_DIM)
            qh, kh, vh = q_s[rows, cs], k_s[rows, cs], v_s[rows, cs]
            fm = jnp.where(valid, f_s[rows, cs], 1.0)
            run = [None, fm]
            for d in range(1, n_tok):
                run.append(run[d] * jnp.where(row >= d, pltpu.roll(fm, d, 0), 1.0))
            st = s0_ref[bi, hh]
            o = _dot((qh * run[n_tok]).astype(BF16), st.astype(BF16))
            o = o + jnp.sum(qh * kh, axis=-1, keepdims=True) * vh
            for d in range(1, n_tok):
                s_d = jnp.sum(qh * pltpu.roll(kh, d, 0) * run[d], axis=-1, keepdims=True)
                o = o + jnp.where(row >= d, s_d, 0.0) * pltpu.roll(vh, d, 0)
            after = jnp.ones_like(fm)
            for d in range(1, n_tok):
                after = after * pltpu.roll(fm, SAMPLE_PAD - d, 0)
            k_dec = jnp.where(valid, kh * after, 0.0)
            total = run[n_tok].T[:, n_tok - 1:n_tok]
            s_ref[bi, hh] = st * total + _dot_tn(k_dec, vh)
            ms = jnp.mean(o * o, axis=-1, keepdims=True)
            o_s[rows, cs] = o * lax.rsqrt(ms + NORM_EPS) * onw_ref[:, cs]
        return carry

    lax.fori_loop(0, nb, seq_body, 0)
    y = (o_s[...] * gate).astype(BF16)
    xo = x + _dot(y, wout_ref[...])
    if final_norm:
        xo = _rms(xo, fnw_ref[...])
    xo_ref[...] = xo


def _hgrn_sample(x, s0, layer, nw, win, lb, onw, wout, fnw, *, n_tok, final_norm, nb=8):
    n_seq = s0.shape[1]
    assert 2 * n_tok - 1 <= SAMPLE_PAD
    rows = nb * SAMPLE_PAD
    row_scratch = pltpu.VMEM((rows, D_MODEL), F32)
    return pl.pallas_call(
        functools.partial(_hgrn_sample_kernel, n_tok=n_tok, final_norm=final_norm),
        grid=(n_seq // nb,),
        in_specs=[pl.BlockSpec((rows, D_MODEL), lambda i: (i, 0)),
                  pl.BlockSpec((None, nb, N_HEADS, HEAD_DIM, HEAD_DIM), lambda i: (layer, i, 0, 0, 0)),
                  _const_spec((1, D_MODEL)), _const_spec((D_MODEL, 4 * D_MODEL)), _const_spec((1, D_MODEL)),
                  _const_spec((1, D_MODEL)), _const_spec((D_MODEL, D_MODEL)), _const_spec((1, D_MODEL))],
        out_specs=[pl.BlockSpec((rows, D_MODEL), lambda i: (i, 0)),
                   pl.BlockSpec((nb, N_HEADS, HEAD_DIM, HEAD_DIM), lambda i: (i, 0, 0, 0))],
        out_shape=[jax.ShapeDtypeStruct(x.shape, F32), jax.ShapeDtypeStruct(s0.shape[1:], F32)],
        scratch_shapes=[row_scratch] * 5,
        compiler_params=_cparams("arbitrary"),
        name="hgrn_sample",
    )(x, s0, nw, win, lb, onw, wout, fnw)


def _attn_proj_kernel(x_ref, cos_ref, sin_ref, nw_ref, win_ref,
                      g0_ref, g1_ref, g2_ref, gate_ref, kv0_ref, kv1_ref, kv2_ref):
    tb = x_ref.shape[0]
    h = _rms(x_ref[...], nw_ref[...]).astype(BF16)
    cos = cos_ref[...]
    sin = sin_ref[...]

    def heads(a):
        return [a[:, hh * HEAD_DIM:(hh + 1) * HEAD_DIM] for hh in range(N_HEADS)]

    def rope(cols):
        return [ah * cos + pltpu.roll(ah, HEAD_DIM // 2, 1) * sin for ah in cols]

    def token_tiles(cols, keep):
        return jnp.swapaxes(jnp.stack([c[tb - keep:, :] for c in cols], axis=0), 0, 1)

    for gi, (qkv_ref, kv_ref) in enumerate(((g0_ref, kv0_ref), (g1_ref, kv1_ref), (g2_ref, kv2_ref))):
        base = 3 * gi * D_MODEL
        dil = qkv_ref.shape[0]
        q_cols = rope(heads(_dot(h, win_ref[:, base:base + D_MODEL])))
        k_cols = rope(heads(_dot(h, win_ref[:, base + D_MODEL:base + 2 * D_MODEL])))
        v = _dot(h, win_ref[:, base + 2 * D_MODEL:base + 3 * D_MODEL])
        q = jnp.concatenate(q_cols, axis=1) * (HEAD_DIM ** -0.5)
        k = jnp.concatenate(k_cols, axis=1)
        for ci, val in enumerate((q, k, v)):
            if dil == 1:
                val = val.reshape(1, tb, D_MODEL)
            else:
                val = jnp.swapaxes(val.reshape(tb // dil, dil, D_MODEL), 0, 1)
            qkv_ref[:, :, ci * D_MODEL:(ci + 1) * D_MODEL] = val.astype(BF16)
        keep = kv_ref.shape[0]
        kv_ref[:, 0] = token_tiles(k_cols, keep)
        kv_ref[:, 1] = token_tiles(heads(v), keep)
    gate_ref[...] = _silu(_dot(h, win_ref[:, 9 * D_MODEL:10 * D_MODEL])).astype(BF16)


def _attn_proj(x, cos, sin, nw, win, *, dils, keeps, tb):
    batch, seq, _ = x.shape
    qkv_specs, qkv_shapes, kv_specs, kv_shapes = [], [], [], []
    for dil, keep in zip(dils, keeps):
        qkv_specs.append(pl.BlockSpec((None, dil, tb // dil, 3 * D_MODEL), lambda b, t: (b, 0, t, 0)))
        qkv_shapes.append(jax.ShapeDtypeStruct((batch, dil, seq // dil, 3 * D_MODEL), BF16))
        kb = min(tb, keep)
        first = (seq - keep) // tb if keep >= tb else 0
        kv_specs.append(pl.BlockSpec(
            (None, kb, 2, N_HEADS, HEAD_DIM),
            lambda b, t, first=first, kb=kb: (b, jnp.maximum(t - first, 0) if kb == tb else 0, 0, 0, 0)))
        kv_shapes.append(jax.ShapeDtypeStruct((batch, keep, 2, N_HEADS, HEAD_DIM), F32))
    return pl.pallas_call(
        _attn_proj_kernel,
        grid=(batch, seq // tb),
        in_specs=[pl.BlockSpec((None, tb, D_MODEL), lambda b, t: (b, t, 0)),
                  pl.BlockSpec((tb, HEAD_DIM), lambda b, t: (t, 0)),
                  pl.BlockSpec((tb, HEAD_DIM), lambda b, t: (t, 0)),
                  _const_spec((1, D_MODEL)), _const_spec((D_MODEL, 10 * D_MODEL))],
        out_specs=qkv_specs + [pl.BlockSpec((None, tb, D_MODEL), lambda b, t: (b, t, 0))] + kv_specs,
        out_shape=qkv_shapes + [jax.ShapeDtypeStruct((batch, seq, D_MODEL), BF16)] + kv_shapes,
        compiler_params=_cparams("arbitrary", "arbitrary"),
        name="attn_proj",
    )(x, cos, sin, nw, win)


ATTN_UNITS = 4


def _attn_group_kernel(bps_ref, q_ref, kc_ref, kp_ref, vc_ref, vp_ref, o_ref, lse_ref):
    units = q_ref.shape[0]
    first = pl.program_id(1) * units
    blocks_per_seq = bps_ref[0]
    a = lax.broadcasted_iota(jnp.int32, (ATTN_BLOCK, ATTN_BLOCK), 0)
    c = lax.broadcasted_iota(jnp.int32, (ATTN_BLOCK, ATTN_BLOCK), 1)
    mask_cur = c <= a
    lane = lax.broadcasted_iota(jnp.int32, (ATTN_BLOCK, HEAD_DIM), 1)
    ones_col = jnp.where(lane == 0, 1.0, 0.0).astype(BF16)
    for u in range(units):
        k_prev, v_prev = (kp_ref.at[0], vp_ref.at[0]) if u == 0 else (kc_ref.at[u - 1], vc_ref.at[u - 1])
        mask_prev = (c >= a) & (lax.rem(first + u, blocks_per_seq) > 0)
        lse_tile = jnp.zeros((ATTN_BLOCK, HEAD_DIM), F32)
        for hh in range(N_HEADS):
            cs = slice(hh * HEAD_DIM, (hh + 1) * HEAD_DIM)
            qh = q_ref[u, :, cs]
            s_c = jnp.where(mask_cur, _dot_nt(qh, kc_ref[u, :, cs]), NEG_BIG)
            s_p = jnp.where(mask_prev, _dot_nt(qh, k_prev[:, cs]), NEG_BIG)
            m = jnp.max(jnp.maximum(s_c, s_p), axis=1, keepdims=True)
            ov = (_dot(jnp.exp(s_c - m).astype(BF16), jnp.concatenate([vc_ref[u, :, cs], ones_col], axis=1))
                  + _dot(jnp.exp(s_p - m).astype(BF16), jnp.concatenate([v_prev[:, cs], ones_col], axis=1)))
            den = ov[:, HEAD_DIM:HEAD_DIM + 1]
            o_ref[u, :, cs] = (ov[:, :HEAD_DIM] / den).astype(BF16)
            lse_tile = jnp.where(lane == hh, m + jnp.log(den), lse_tile)
        lse_ref[u] = lse_tile


def _attn_group(qkv):
    batch, dil, n, _ = qkv.shape
    blocks_per_seq = n // ATTN_BLOCK
    n_blocks = dil * blocks_per_seq
    units = ATTN_UNITS
    assert n_blocks % units == 0 and (blocks_per_seq % units == 0 or blocks_per_seq == 1)
    blocks = qkv.reshape(batch, n_blocks, ATTN_BLOCK, 3 * D_MODEL)

    def cur(col):
        return pl.BlockSpec((None, units, ATTN_BLOCK, D_MODEL), lambda b, i: (b, i, 0, col))

    def prev(col):
        return pl.BlockSpec((None, 1, ATTN_BLOCK, D_MODEL), lambda b, i: (b, jnp.maximum(i * units - 1, 0), 0, col))

    o, lse = pl.pallas_call(
        _attn_group_kernel,
        grid=(batch, n_blocks // units),
        in_specs=[pl.BlockSpec(memory_space=pltpu.SMEM), cur(0), cur(1), prev(1), cur(2), prev(2)],
        out_specs=[pl.BlockSpec((None, units, ATTN_BLOCK, D_MODEL), lambda b, i: (b, i, 0, 0)),
                   pl.BlockSpec((None, units, ATTN_BLOCK, HEAD_DIM), lambda b, i: (b, i, 0, 0))],
        out_shape=[jax.ShapeDtypeStruct((batch, n_blocks, ATTN_BLOCK, D_MODEL), BF16),
                   jax.ShapeDtypeStruct((batch, n_blocks, ATTN_BLOCK, HEAD_DIM), F32)],
        compiler_params=_cparams("arbitrary", "arbitrary"),
        name=f"attn_group_d{dil}",
    )(jnp.full((1,), blocks_per_seq, jnp.int32), blocks, blocks, blocks, blocks, blocks)
    return o.reshape(batch, dil, n, D_MODEL), lse.reshape(batch, dil, n, HEAD_DIM)


def _merge_heads(outs, lses):
    m = functools.reduce(jnp.maximum, lses)
    es = [jnp.exp(l - m) for l in lses]
    den = functools.reduce(jnp.add, es)
    ws = [e / den for e in es]
    cols = []
    for hh in range(N_HEADS):
        cs = slice(hh * HEAD_DIM, (hh + 1) * HEAD_DIM)
        cols.append(functools.reduce(jnp.add, [w[:, hh:hh + 1] * o[:, cs] for w, o in zip(ws, outs)]))
    return jnp.concatenate(cols, axis=1)


def _attn_out_kernel(o0_ref, o1_ref, o2_ref, l0_ref, l1_ref, l2_ref, gate_ref, x_ref, wout_ref, xo_ref):
    def natural(ref):
        dil, n, w = ref.shape
        val = ref[...].astype(F32)
        return val[0] if dil == 1 else jnp.swapaxes(val, 0, 1).reshape(dil * n, w)

    o = _merge_heads([natural(r) for r in (o0_ref, o1_ref, o2_ref)], [natural(r) for r in (l0_ref, l1_ref, l2_ref)])
    y = (o * gate_ref[...]).astype(BF16)
    xo_ref[...] = x_ref[...] + _dot(y, wout_ref[...])


def _attn_out(outs, lses, gate, x, wout, *, tb):
    batch, seq, _ = x.shape
    regrouped = lambda a: pl.BlockSpec((None, a.shape[1], tb // a.shape[1], a.shape[3]), lambda b, t: (b, 0, t, 0))
    row = pl.BlockSpec((None, tb, D_MODEL), lambda b, t: (b, t, 0))
    return pl.pallas_call(
        _attn_out_kernel,
        grid=(batch, seq // tb),
        in_specs=[regrouped(a) for a in outs] + [regrouped(a) for a in lses] + [row, row, _const_spec((D_MODEL, D_MODEL))],
        out_specs=row,
        out_shape=jax.ShapeDtypeStruct((batch, seq, D_MODEL), F32),
        compiler_params=_cparams("arbitrary", "arbitrary"),
        name="attn_out",
    )(*outs, *lses, gate, x, wout)


def _attn_sample_kernel(q0_ref, q1_ref, q2_ref, n0_ref, n1_ref, n2_ref, c0_ref, c1_ref, c2_ref,
                        gate_ref, x_ref, wout_ref, xo_ref, o_s, *, n_tok):
    nb = c0_ref.shape[0]
    pos_i = lax.broadcasted_iota(jnp.int32, (ATTN_BLOCK, N_HEADS, 1), 0)
    row_t = lax.broadcasted_iota(jnp.int32, (SAMPLE_PAD, D_MODEL), 0)
    q_refs, new_refs, cache_refs = (q0_ref, q1_ref, q2_ref), (n0_ref, n1_ref, n2_ref), (c0_ref, c1_ref, c2_ref)

    def seq_body(bi, carry):
        r0 = pl.multiple_of(bi * SAMPLE_PAD, SAMPLE_PAD)
        q_rows = [q_ref[pl.ds(r0, SAMPLE_PAD), :].astype(F32) for q_ref in q_refs]
        o_tile = jnp.zeros((SAMPLE_PAD, D_MODEL), F32)
        for t in range(n_tok):
            outs, lses = [], []
            for gi, (_, dil) in enumerate(ATTN_GROUPS):
                q_t = jnp.concatenate([q_rows[gi][t:t + 1, hh * HEAD_DIM:(hh + 1) * HEAD_DIM]
                                       for hh in range(N_HEADS)], axis=0)
                res = t % dil
                k_c = cache_refs[gi][bi, :, res, 0]
                v_c = cache_refs[gi][bi, :, res, 1]
                s_c = jnp.sum(k_c * q_t[None], axis=-1, keepdims=True)
                s_c = jnp.where(pos_i * dil + res >= t, s_c, NEG_BIG)
                new_t = [u for u in range(t + 1) if (t - u) % dil == 0]
                k_n = [new_refs[gi][r0 + u, 0] for u in new_t]
                v_n = [new_refs[gi][r0 + u, 1] for u in new_t]
                s_n = [jnp.sum(kk * q_t, axis=-1, keepdims=True) for kk in k_n]
                m = functools.reduce(jnp.maximum, s_n + [jnp.max(s_c, axis=0)])
                p_c = jnp.exp(s_c - m[None])
                p_n = [jnp.exp(s - m) for s in s_n]
                den = functools.reduce(jnp.add, p_n + [jnp.sum(p_c, axis=0)])
                acc = functools.reduce(jnp.add, [p * vv for p, vv in zip(p_n, v_n)] + [jnp.sum(p_c * v_c, axis=0)])
                outs.append(acc / den)
                lses.append(m + jnp.log(den))
            m_g = functools.reduce(jnp.maximum, lses)
            e_g = [jnp.exp(l - m_g) for l in lses]
            den_g = functools.reduce(jnp.add, e_g)
            o_t = functools.reduce(jnp.add, [e / den_g * o for e, o in zip(e_g, outs)])
            o_row = jnp.concatenate([o_t[hh:hh + 1, :] for hh in range(N_HEADS)], axis=1)
            o_tile = jnp.where(row_t == t, o_row, o_tile)
        o_s[pl.ds(r0, SAMPLE_PAD), :] = o_tile
        return carry

    lax.fori_loop(0, nb, seq_body, 0)
    y = (o_s[...] * gate_ref[...]).astype(BF16)
    xo_ref[...] = x_ref[...] + _dot(y, wout_ref[...])


def _attn_sample(qkvs, news, caches, layer, gate, x, wout, *, n_tok, nb=2):
    n_seq = caches[0].shape[1]
    rows = nb * SAMPLE_PAD
    c_views, c_specs = [], []
    for (window, dil), cache in zip(ATTN_GROUPS, caches):
        assert cache.shape[2] == window == ATTN_BLOCK * dil and (n_tok <= dil or dil == 1)
        c_views.append(cache.reshape(cache.shape[0], n_seq, ATTN_BLOCK, dil, 2, N_HEADS, HEAD_DIM))
        n_res = min(dil, n_tok)
        c_specs.append(pl.BlockSpec((None, nb, ATTN_BLOCK, n_res, 2, N_HEADS, HEAD_DIM),
                                    lambda i: (layer, i, 0, 0, 0, 0, 0)))
    row = lambda w: pl.BlockSpec((rows, w), lambda i: (i, 0))
    new_spec = pl.BlockSpec((rows, 2, N_HEADS, HEAD_DIM), lambda i: (i, 0, 0, 0))
    return pl.pallas_call(
        functools.partial(_attn_sample_kernel, n_tok=n_tok),
        grid=(n_seq // nb,),
        in_specs=[row(D_MODEL)] * 3 + [new_spec] * 3 + c_specs + [row(D_MODEL), row(D_MODEL),
                                                                  _const_spec((D_MODEL, D_MODEL))],
        out_specs=row(D_MODEL),
        out_shape=jax.ShapeDtypeStruct(x.shape, F32),
        scratch_shapes=[pltpu.VMEM((rows, D_MODEL), F32)],
        compiler_params=_cparams("arbitrary"),
        name="attn_sample",
    )(*qkvs, *news, *c_views, gate, x, wout)


ROLL_GROUPS = 8


def _cache_roll_kernel(c_ref, nxt_ref, new_ref, o_ref):
    i = pl.program_id(0)
    last = pl.num_programs(0) - 1
    g = o_ref.shape[1]
    o_ref[:, 0:g - 1] = c_ref[:, 1:g]
    o_ref[:, g - 1:g] = jnp.where(i < last, nxt_ref[...], new_ref[...])


def _cache_roll(cache, new, layer, *, n_tok):
    _, n_seq, length = cache.shape[:3]
    n_grp = length // n_tok
    g = ROLL_GROUPS
    rows = n_tok * KV_ROWS
    c_view = cache.reshape(cache.shape[0], n_seq, n_grp, rows, HEAD_DIM)
    n_view = new.reshape(n_seq, 1, rows, HEAD_DIM)
    out = pl.pallas_call(
        _cache_roll_kernel,
        grid=(n_grp // g,),
        in_specs=[pl.BlockSpec((None, n_seq, g, rows, HEAD_DIM), lambda i: (layer, 0, i, 0, 0)),
                  pl.BlockSpec((None, n_seq, 1, rows, HEAD_DIM),
                               lambda i: (layer, 0, jnp.minimum((i + 1) * g, n_grp - 1), 0, 0)),
                  pl.BlockSpec((n_seq, 1, rows, HEAD_DIM), lambda i: (0, 0, 0, 0))],
        out_specs=pl.BlockSpec((n_seq, g, rows, HEAD_DIM), lambda i: (0, i, 0, 0)),
        out_shape=jax.ShapeDtypeStruct((n_seq, n_grp, rows, HEAD_DIM), cache.dtype),
        compiler_params=_cparams("arbitrary"),
        name="cache_roll",
    )(c_view, c_view, n_view)
    return out.reshape(cache.shape[1:])


def _gelu_tanh(y):
    return 0.5 * y * (1.0 + jnp.tanh(math.sqrt(2.0 / math.pi) * (y + 0.044715 * (y * y * y))))


def _s5_kernel(x_ref, s0_ref, nw_ref, win_ref, wb_ref, are_ref, aim_ref, wc_ref, dsk_ref, wglu_ref, bglu_ref,
               wout_ref, xo_ref, sfin_ref, bu_s, st_s, *, n_seq, tb, seq_major_io):
    ti = pl.program_id(0)
    tile = 2 * S5_NSTATE // S5_KTILES
    half = tile // 2
    lanes = 1024

    @pl.when(ti == 0)
    def _():
        st_s[...] = s0_ref[...]

    if seq_major_io:
        x = jnp.swapaxes(x_ref[...], 0, 1).reshape(tb * n_seq, D_MODEL)
    else:
        x = x_ref[...]
    h = _rms(x, nw_ref[...]).astype(BF16)
    u = _dot(h, win_ref[:, 0:D_MODEL])
    gate = _silu(_dot(h, win_ref[:, D_MODEL:2 * D_MODEL]))
    ub = u.astype(BF16)
    kw = D_MODEL // S5_KTILES
    for kt in range(S5_KTILES):
        bu_s[:, kt * tile:(kt + 1) * tile] = _dot(ub[:, kt * kw:(kt + 1) * kw], wb_ref[kt])

    for sg in range(n_seq // 8):
        srow = slice(sg * 8, (sg + 1) * 8)
        for kt in range(S5_KTILES):
            for part in range(half // lanes):
                c_re = slice(kt * tile + part * lanes, kt * tile + (part + 1) * lanes)
                c_im = slice(kt * tile + half + part * lanes, kt * tile + half + (part + 1) * lanes)
                a_re = are_ref[:, c_re]
                a_im = aim_ref[:, c_re]

                def step(t, carry):
                    s_re, s_im = carry
                    rows = pl.ds(pl.multiple_of(t * n_seq + sg * 8, 8), 8)
                    n_re = a_re * s_re - a_im * s_im + bu_s[rows, c_re]
                    n_im = a_re * s_im + a_im * s_re + bu_s[rows, c_im]
                    bu_s[rows, c_re] = n_re
                    bu_s[rows, c_im] = n_im
                    return n_re, n_im

                s_re, s_im = lax.fori_loop(0, tb, step, (st_s[srow, c_re], st_s[srow, c_im]))
                st_s[srow, c_re] = s_re
                st_s[srow, c_im] = s_im

    ys = []
    for kt in range(S5_KTILES):
        ys.append(_dot(bu_s[:, kt * tile:(kt + 1) * tile].astype(BF16), wc_ref[kt]))
    y = jnp.concatenate(ys, axis=1) + dsk_ref[...] * u
    y = _gelu_tanh(y)
    y = y * jax.nn.sigmoid(_dot(y.astype(BF16), wglu_ref[...]) + bglu_ref[...])
    y = (y * gate).astype(BF16)
    xo = x + _dot(y, wout_ref[...])
    if seq_major_io:
        xo_ref[...] = jnp.swapaxes(xo.reshape(tb, n_seq, D_MODEL), 0, 1)
    else:
        xo_ref[...] = xo

    @pl.when(ti == pl.num_programs(0) - 1)
    def _():
        sfin_ref[...] = st_s[...]


def _s5_layer(x, s0, nw, win, wb, a_re, a_im, wc, dsk, wglu, bglu, wout, *, n_seq, seq, tb, seq_major_io):
    rows = tb * n_seq
    ncol = 2 * S5_NSTATE
    if seq_major_io:
        x_spec = pl.BlockSpec((n_seq, tb, D_MODEL), lambda t: (0, t, 0))
    else:
        x_spec = pl.BlockSpec((rows, D_MODEL), lambda t: (t, 0))
    return pl.pallas_call(
        functools.partial(_s5_kernel, n_seq=n_seq, tb=tb, seq_major_io=seq_major_io),
        grid=(seq // tb,),
        in_specs=[x_spec, _const_spec((n_seq, ncol)),
                  _const_spec((1, D_MODEL)), _const_spec((D_MODEL, 2 * D_MODEL)), _const_spec(wb.shape),
                  _const_spec((8, ncol)), _const_spec((8, ncol)), _const_spec(wc.shape),
                  _const_spec((1, D_MODEL)), _const_spec((D_MODEL, D_MODEL)), _const_spec((1, D_MODEL)),
                  _const_spec((D_MODEL, D_MODEL))],
        out_specs=[x_spec, pl.BlockSpec((n_seq, ncol), lambda t: (0, 0))],
        out_shape=[jax.ShapeDtypeStruct(x.shape, F32), jax.ShapeDtypeStruct((n_seq, ncol), F32)],
        scratch_shapes=[pltpu.VMEM((rows, ncol), F32), pltpu.VMEM((n_seq, ncol), F32)],
        compiler_params=_cparams("arbitrary"),
        name="s5_layer",
    )(x, s0, nw, win, wb, a_re, a_im, wc, dsk, wglu, bglu, wout)


def _s5_params(a_re, a_im, b_re, b_im, c_re, c_im, log_dt):
    lam_re = jnp.minimum(a_re, S5_MAX_RE)
    lam_im = a_im
    dt = jnp.exp(log_dt)[:, None]
    mag = jnp.exp(lam_re * dt)
    bar_re = mag * jnp.cos(lam_im * dt)
    bar_im = mag * jnp.sin(lam_im * dt)
    den = lam_re * lam_re + lam_im * lam_im
    xr = bar_re - 1.0
    coef_re = (xr * lam_re + bar_im * lam_im) / den
    coef_im = (bar_im * lam_re - xr * lam_im) / den
    bbar_re = coef_re[..., None] * b_re - coef_im[..., None] * b_im
    bbar_im = coef_re[..., None] * b_im + coef_im[..., None] * b_re
    gl = S5_GROUPS // S5_KTILES
    eye = jnp.eye(gl, dtype=F32)

    def to_cols(a):
        return a.reshape(S5_KTILES, gl * S5_STATE)

    def b_tile(bb):
        bb = bb.reshape(S5_KTILES, gl, S5_STATE, S5_GROUP_CH)
        return jnp.einsum('kgpc,gh->kgchp', bb, eye).reshape(S5_KTILES, gl * S5_GROUP_CH, gl * S5_STATE)

    def c_tile(cc):
        cc = cc.reshape(S5_KTILES, gl, S5_GROUP_CH, S5_STATE)
        return jnp.einsum('kgcp,gh->kgphc', cc, eye).reshape(S5_KTILES, gl * S5_STATE, gl * S5_GROUP_CH)

    wb = jnp.concatenate([b_tile(bbar_re), b_tile(bbar_im)], axis=2).astype(BF16)
    wc = jnp.concatenate([c_tile(c_re), -c_tile(c_im)], axis=1).astype(BF16)
    cols = lambda a: jnp.concatenate([to_cols(a), to_cols(a)], axis=1).reshape(1, -1)
    a_re_cols = jnp.broadcast_to(cols(bar_re), (8, 2 * S5_NSTATE))
    a_im_cols = jnp.broadcast_to(cols(bar_im), (8, 2 * S5_NSTATE))
    return wb, wc, a_re_cols, a_im_cols


def _s5_state_to_cols(s):
    n = s.shape[0]
    gl = S5_GROUPS // S5_KTILES
    s = s.reshape(n, S5_KTILES, gl * S5_STATE, 2)
    return jnp.moveaxis(s, 3, 2).reshape(n, 2 * S5_NSTATE)


def _s5_cols_to_state(c):
    n = c.shape[0]
    gl = S5_GROUPS // S5_KTILES
    c = c.reshape(n, S5_KTILES, 2, gl * S5_STATE)
    return jnp.moveaxis(c, 2, 3).reshape(n, S5_GROUPS, S5_STATE, 2)


def _rope_tables(pos):
    half = HEAD_DIM // 2
    inv_freq = ROPE_THETA ** (-jnp.arange(half, dtype=F32) / half)
    ang = pos[:, None] * inv_freq[None, :]
    cos, sin = jnp.cos(ang), jnp.sin(ang)
    return jnp.concatenate([cos, cos], axis=1), jnp.concatenate([-sin, sin], axis=1)


def kernel(x_prompt, x_sample, state_hgrn, cache_kv_w128, cache_kv_w512, cache_kv_w2048, state_s5,
           norm_w, final_norm_w, a_w_in, a_lb_logits, a_onorm_w, a_w_out, b_w_in, b_w_out,
           c_w_in, c_a_re, c_a_im, c_b_re, c_b_im, c_c_re, c_c_im, c_d, c_log_dt, c_w_glu, c_b_glu, c_w_out):
    batch, seq, _ = x_prompt.shape
    n_seq, n_tok, _ = x_sample.shape
    depth = norm_w.shape[0]
    caches = (cache_kv_w128, cache_kv_w512, cache_kv_w2048)
    dils = [d for _, d in ATTN_GROUPS]
    row = lambda a: a.reshape(1, -1)

    p_lb = jax.nn.softmax(a_lb_logits.astype(F32), axis=0)
    lower_bounds = jnp.cumsum(p_lb, axis=0) - p_lb[0:1]
    fnw = row(final_norm_w)

    s_rows = n_seq * SAMPLE_PAD
    xs = jnp.pad(x_sample, ((0, 0), (0, SAMPLE_PAD - n_tok), (0, 0))).reshape(s_rows, D_MODEL)
    xp = x_prompt
    pos_p = jnp.arange(seq, dtype=F32)
    pos_s = jnp.tile(jnp.pad(PAST_LEN + jnp.arange(n_tok, dtype=F32), (0, SAMPLE_PAD - n_tok)), n_seq)

    hgrn_p, hgrn_s, s5_p, s5_s = [], [], [], []
    kv_p = [[] for _ in ATTN_GROUPS]
    kv_s = [[] for _ in ATTN_GROUPS]
    for layer in range(depth):
        kind, j = layer % 3, layer // 3
        last = layer == depth - 1
        nw = row(norm_w[layer])
        if kind == 0:
            win, wout = a_w_in[j].astype(BF16), a_w_out[j].astype(BF16)
            lb, onw = row(lower_bounds[j]), row(a_onorm_w[j])
            xp, st = _hgrn_prompt(xp, nw, win, lb, onw, wout, fnw, final_norm=last)
            hgrn_p.append(st)
            xs, st = _hgrn_sample(xs, state_hgrn, j, nw, win, lb, onw, wout, fnw, n_tok=n_tok, final_norm=last)
            hgrn_s.append(st)
        elif kind == 1:
            win, wout = b_w_in[j].astype(BF16), b_w_out[j].astype(BF16)
            cos, sin = _rope_tables(pos_p)
            *qkvs, gate, kv0, kv1, kv2 = _attn_proj(xp, cos, sin, nw, win, dils=dils,
                                                    keeps=[min(w, seq) for w, _ in ATTN_GROUPS], tb=256)
            for g, kv in enumerate((kv0, kv1, kv2)):
                kv_p[g].append(kv)
            outs, lses = zip(*[_attn_group(qkv) for qkv in qkvs])
            xp = _attn_out(outs, lses, gate, xp, wout, tb=256)

            cos, sin = _rope_tables(pos_s)
            *qkvs, gate, kv0, kv1, kv2 = _attn_proj(xs.reshape(1, s_rows, D_MODEL), cos, sin, nw, win,
                                                    dils=[1] * len(dils), keeps=[s_rows] * len(dils), tb=s_rows)
            new_rows = [kv.reshape(s_rows, 2, N_HEADS, HEAD_DIM) for kv in (kv0, kv1, kv2)]
            xs = _attn_sample([q.reshape(s_rows, 3 * D_MODEL) for q in qkvs], new_rows, caches, j,
                              gate.reshape(s_rows, D_MODEL), xs, wout, n_tok=n_tok)
            news = [kv.reshape(n_seq, SAMPLE_PAD, 2, N_HEADS, HEAD_DIM)[:, :n_tok] for kv in new_rows]
            for g, (cache, new) in enumerate(zip(caches, news)):
                kv_s[g].append(_cache_roll(cache, new, j, n_tok=n_tok))
        else:
            wb, wc, a_re_cols, a_im_cols = _s5_params(c_a_re[j], c_a_im[j], c_b_re[j], c_b_im[j],
                                                      c_c_re[j], c_c_im[j], c_log_dt[j])
            wts = (nw, c_w_in[j].astype(BF16), wb, a_re_cols, a_im_cols, wc, row(c_d[j]),
                   c_w_glu[j].astype(BF16), row(c_b_glu[j]), c_w_out[j].astype(BF16))
            xp, sfin = _s5_layer(xp, jnp.zeros((batch, 2 * S5_NSTATE), F32), *wts,
                                 n_seq=batch, seq=seq, tb=32, seq_major_io=True)
            s5_p.append(_s5_cols_to_state(sfin))
            xs_tm = jnp.swapaxes(xs.reshape(n_seq, SAMPLE_PAD, D_MODEL)[:, :n_tok], 0, 1)
            xs_tm, sfin = _s5_layer(xs_tm.reshape(n_tok * n_seq, D_MODEL), _s5_state_to_cols(state_s5[j]), *wts,
                                    n_seq=n_seq, seq=n_tok, tb=n_tok, seq_major_io=False)
            s5_s.append(_s5_cols_to_state(sfin))
            xs = jnp.pad(jnp.swapaxes(xs_tm.reshape(n_tok, n_seq, D_MODEL), 0, 1),
                         ((0, 0), (0, SAMPLE_PAD - n_tok), (0, 0))).reshape(s_rows, D_MODEL)

    if depth % 3 != 1:
        raise NotImplementedError("the final norm is fused into a last HGRN2 layer")
    y_sample = xs.reshape(n_seq, SAMPLE_PAD, D_MODEL)[:, :n_tok]
    stack = lambda parts: jnp.stack(parts, axis=0)
    return (xp, y_sample, stack(hgrn_p), stack(hgrn_s),
            stack(kv_p[0]), stack(kv_s[0]), stack(kv_p[1]), stack(kv_s[1]), stack(kv_p[2]), stack(kv_s[2]),
            stack(s5_p), stack(s5_s))
```

```python
import functools
import math

import numpy as np
import jax
import jax.numpy as jnp
from jax import lax
from jax.experimental import pallas as pl
from jax.experimental.pallas import tpu as pltpu

F32 = jnp.float32
BF16 = jnp.bfloat16

D_MODEL = 1024
N_HEADS = 8
HEAD_DIM = 128
NORM_EPS = 1e-6
NEG_BIG = -1e30
PAST_LEN = 8192
ROPE_THETA = 10000.0
A_EXP_CLIP = 60.0
LOG2_E = 1.4426950408889634
ATTN_GROUPS = ((128, 1), (512, 4), (2048, 16))
ATTN_BLOCK = 128
S5_GROUPS = 64
S5_STATE = 64
S5_GROUP_CH = 16
S5_MAX_RE = -1e-4
S5_KTILES = 4
S5_NSTATE = S5_GROUPS * S5_STATE
SAMPLE_PAD = 8
KV_ROWS = 2 * N_HEADS
VMEM_LIMIT_V7X = 56 * 1024 * 1024


def _cparams(*sem):
    return pltpu.CompilerParams(dimension_semantics=sem, vmem_limit_bytes=VMEM_LIMIT_V7X)


def _const_spec(shape):
    nd = len(shape)
    return pl.BlockSpec(shape, lambda *_: (0,) * nd, pipeline_mode=pl.Buffered(1))


def _dot(a, b):
    return jnp.dot(a, b, preferred_element_type=F32)


def _dot_nt(a, b):
    return lax.dot_general(a, b, (((1,), (1,)), ((), ())), preferred_element_type=F32)


def _dot_tn(a, b):
    return lax.dot_general(a, b, (((0,), (0,)), ((), ())), preferred_element_type=F32)


def _rms(x, w):
    ms = jnp.mean(x * x, axis=-1, keepdims=True)
    return x * lax.rsqrt(ms + NORM_EPS) * w


def _silu(x):
    return x * jax.nn.sigmoid(x)


def _hgrn_gates(zf, lb):
    ez = jnp.exp(-jnp.abs(zf))
    one_ez = 1.0 + ez
    log_sig = jnp.minimum(zf, 0.0) - jnp.log(one_ez)
    log_f = log_sig + jnp.log(1.0 + lb * jnp.exp(jnp.minimum(-zf, A_EXP_CLIP)))
    r = 1.0 / one_ez
    key = (1.0 - lb) * jnp.where(zf >= 0.0, ez * r, r)
    return log_f, key


SUBLANES = 8


def _rows_from_partner(z, bit, take_upper):
    c = z.shape[0]
    step = 1 << bit
    if step < SUBLANES:
        z3 = z.reshape(c // SUBLANES, SUBLANES, z.shape[1])
        sub = lax.broadcasted_iota(jnp.int32, z3.shape, 1)
        shift = SUBLANES - step if take_upper else step
        moved = pltpu.roll(z3, shift, 1)
        keep = ((sub >> bit) & 1) == (1 if take_upper else 0)
        return jnp.where(keep, z3, moved).reshape(z.shape)
    parts = []
    for k in range(c // (2 * step)):
        src = z[2 * k * step + step:2 * (k + 1) * step] if take_upper else z[2 * k * step:2 * k * step + step]
        parts += [src, src]
    return jnp.concatenate(parts, axis=0)


def _hgrn_level_matrix(c):
    t = np.arange(c)[:, None]
    s = np.arange(c)[None, :]
    x = t ^ s
    lev = np.zeros((c, c), np.int32)
    nz = x > 0
    lev[nz] = np.floor(np.log2(x[nz])).astype(np.int32) + 1
    return np.where(s <= t, lev, -1).astype(np.int32)


def _hgrn_prompt_kernel(x_ref, nw_ref, win_ref, lb_ref, onw_ref, wout_ref, fnw_ref, tri_ref, lvl_ref,
                        xo_ref, st_ref,
                        q_s, k_s, v_s, g_s, gate_s, o_s, st_s, *, chunk, final_norm):
    ti = pl.program_id(1)
    tb = x_ref.shape[0]
    levels = int(math.log2(chunk))

    @pl.when(ti == 0)
    def _():
        st_s[...] = jnp.zeros_like(st_s)

    x = x_ref[...]
    h = _rms(x, nw_ref[...]).astype(BF16)
    q_s[...] = _silu(_dot(h, win_ref[:, 0:D_MODEL]))
    log_f, key = _hgrn_gates(_dot(h, win_ref[:, D_MODEL:2 * D_MODEL]), lb_ref[...])
    g_s[...] = log_f
    k_s[...] = key
    v_s[...] = _dot(h, win_ref[:, 2 * D_MODEL:3 * D_MODEL])
    gate_s[...] = _silu(_dot(h, win_ref[:, 3 * D_MODEL:4 * D_MODEL]))

    def head_body(ci, hh):
        rows = slice(ci * chunk, (ci + 1) * chunk)
        cs = slice(hh * HEAD_DIM, (hh + 1) * HEAD_DIM)
        qh = q_s[rows, cs]
        kh = k_s[rows, cs]
        vh = v_s[rows, cs].astype(BF16)
        gh = g_s[rows, cs]
        g_hi = gh.astype(BF16)
        g_lo = (gh - g_hi.astype(F32)).astype(BF16)
        b = _dot(tri_ref[...], jnp.concatenate([g_hi, g_lo], axis=1))
        b = b[:, :HEAD_DIM] + b[:, HEAD_DIM:]
        b_last = b[chunk - 1:chunk, :]
        st = st_s[hh]
        o = _dot_nt((qh * jnp.exp(b)).astype(BF16), st.astype(BF16))
        k_dec = (kh * jnp.exp(b_last - b)).astype(BF16)
        st_s[hh] = st * jnp.exp(b_last) + _dot_tn(vh, k_dec)
        p = _dot_nt(qh.astype(BF16), kh.astype(BF16)).astype(BF16) * lvl_ref[0]
        z = b
        for lev in range(1, levels + 1):
            bit = lev - 1
            edge = _rows_from_partner(z, bit, take_upper=False)
            e_l = jnp.exp2(jnp.abs(b - edge) * (-LOG2_E))
            sc = _dot_nt((qh * e_l).astype(BF16), (kh * e_l).astype(BF16))
            p = p + sc.astype(BF16) * lvl_ref[lev]
            if lev < levels:
                z = _rows_from_partner(z, bit, take_upper=True)
        o = o + _dot(p, vh)
        ms = jnp.mean(o * o, axis=-1, keepdims=True)
        o_s[rows, cs] = o * lax.rsqrt(ms + NORM_EPS) * onw_ref[:, cs]

    for ci in range(tb // chunk):
        for hh in range(N_HEADS):
            head_body(ci, hh)

    y = (o_s[...] * gate_s[...]).astype(BF16)
    xo = x + _dot(y, wout_ref[...])
    if final_norm:
        xo = _rms(xo, fnw_ref[...])
    xo_ref[...] = xo

    @pl.when(ti == pl.num_programs(1) - 1)
    def _():
        for hh in range(N_HEADS):
            st_ref[hh] = st_s[hh].T


def _hgrn_prompt(x, nw, win, lb, onw, wout, fnw, *, final_norm, tb=512, chunk=128):
    batch, seq, _ = x.shape
    tri = jnp.asarray(np.tril(np.ones((chunk, chunk), np.float32)), BF16)
    lvl_idx = _hgrn_level_matrix(chunk)
    lvl = jnp.asarray(np.stack([lvl_idx == lev for lev in range(int(math.log2(chunk)) + 1)]).astype(np.float32), BF16)
    row_scratch = pltpu.VMEM((tb, D_MODEL), F32)
    x_spec = pl.BlockSpec((None, tb, D_MODEL), lambda b, t: (b, t, 0))
    return pl.pallas_call(
        functools.partial(_hgrn_prompt_kernel, chunk=chunk, final_norm=final_norm),
        grid=(batch, seq // tb),
        in_specs=[x_spec, _const_spec((1, D_MODEL)), _const_spec((D_MODEL, 4 * D_MODEL)),
                  _const_spec((1, D_MODEL)), _const_spec((1, D_MODEL)), _const_spec((D_MODEL, D_MODEL)),
                  _const_spec((1, D_MODEL)), _const_spec(tri.shape), _const_spec(lvl.shape)],
        out_specs=[x_spec, pl.BlockSpec((None, N_HEADS, HEAD_DIM, HEAD_DIM), lambda b, t: (b, 0, 0, 0))],
        out_shape=[jax.ShapeDtypeStruct((batch, seq, D_MODEL), F32),
                   jax.ShapeDtypeStruct((batch, N_HEADS, HEAD_DIM, HEAD_DIM), F32)],
        scratch_shapes=[row_scratch] * 6 + [pltpu.VMEM((N_HEADS, HEAD_DIM, HEAD_DIM), F32)],
        compiler_params=_cparams("arbitrary", "arbitrary"),
        name="hgrn_prompt",
    )(x, nw, win, lb, onw, wout, fnw, tri, lvl)


def _hgrn_sample_kernel(x_ref, s0_ref, nw_ref, win_ref, lb_ref, onw_ref, wout_ref, fnw_ref,
                        xo_ref, s_ref, q_s, k_s, v_s, f_s, o_s, *, n_tok, final_norm):
    nb = s0_ref.shape[0]
    x = x_ref[...]
    h = _rms(x, nw_ref[...]).astype(BF16)
    q_s[...] = _silu(_dot(h, win_ref[:, 0:D_MODEL]))
    log_f, key = _hgrn_gates(_dot(h, win_ref[:, D_MODEL:2 * D_MODEL]), lb_ref[...])
    f_s[...] = jnp.exp(log_f)
    k_s[...] = key
    v_s[...] = _dot(h, win_ref[:, 2 * D_MODEL:3 * D_MODEL])
    gate = _silu(_dot(h, win_ref[:, 3 * D_MODEL:4 * D_MODEL]))
    row = lax.broadcasted_iota(jnp.int32, (SAMPLE_PAD, HEAD_DIM), 0)

    valid = row < n_tok

    def seq_body(bi, carry):
        rows = pl.ds(pl.multiple_of(bi * SAMPLE_PAD, SAMPLE_PAD), SAMPLE_PAD)
        for hh in range(N_HEADS):
            cs = slice(hh * HEAD_DIM, (hh + 1) * HEAD_DIM)
            qh, kh, vh = q_s[rows, cs], k_s[rows, cs], v_s[rows, cs]
            fm = jnp.where(valid, f_s[rows, cs], 1.0)
            run = [None, fm]
            for d in range(1, n_tok):
                run.append(run[d] * jnp.where(row >= d, pltpu.roll(fm, d, 0), 1.0))
            st = s0_ref[bi, hh]
            o = _dot((qh * run[n_tok]).astype(BF16), st.astype(BF16))
            o = o + jnp.sum(qh * kh, axis=-1, keepdims=True) * vh
            for d in range(1, n_tok):
                s_d = jnp.sum(qh * pltpu.roll(kh, d, 0) * run[d], axis=-1, keepdims=True)
                o = o + jnp.where(row >= d, s_d, 0.0) * pltpu.roll(vh, d, 0)
            after = jnp.ones_like(fm)
            for d in range(1, n_tok):
                after = after * pltpu.roll(fm, SAMPLE_PAD - d, 0)
            k_dec = jnp.where(valid, kh * after, 0.0)
            total = run[n_tok].T[:, n_tok - 1:n_tok]
            s_ref[bi, hh] = st * total + _dot_tn(k_dec, vh)
            ms = jnp.mean(o * o, axis=-1, keepdims=True)
            o_s[rows, cs] = o * lax.rsqrt(ms + NORM_EPS) * onw_ref[:, cs]
        return carry

    lax.fori_loop(0, nb, seq_body, 0)
    y = (o_s[...] * gate).astype(BF16)
    xo = x + _dot(y, wout_ref[...])
    if final_norm:
        xo = _rms(xo, fnw_ref[...])
    xo_ref[...] = xo


def _hgrn_sample(x, s0, layer, nw, win, lb, onw, wout, fnw, *, n_tok, final_norm, nb=8):
    n_seq = s0.shape[1]
    assert 2 * n_tok - 1 <= SAMPLE_PAD
    rows = nb * SAMPLE_PAD
    row_scratch = pltpu.VMEM((rows, D_MODEL), F32)
    return pl.pallas_call(
        functools.partial(_hgrn_sample_kernel, n_tok=n_tok, final_norm=final_norm),
        grid=(n_seq // nb,),
        in_specs=[pl.BlockSpec((rows, D_MODEL), lambda i: (i, 0)),
                  pl.BlockSpec((None, nb, N_HEADS, HEAD_DIM, HEAD_DIM), lambda i: (layer, i, 0, 0, 0)),
                  _const_spec((1, D_MODEL)), _const_spec((D_MODEL, 4 * D_MODEL)), _const_spec((1, D_MODEL)),
                  _const_spec((1, D_MODEL)), _const_spec((D_MODEL, D_MODEL)), _const_spec((1, D_MODEL))],
        out_specs=[pl.BlockSpec((rows, D_MODEL), lambda i: (i, 0)),
                   pl.BlockSpec((nb, N_HEADS, HEAD_DIM, HEAD_DIM), lambda i: (i, 0, 0, 0))],
        out_shape=[jax.ShapeDtypeStruct(x.shape, F32), jax.ShapeDtypeStruct(s0.shape[1:], F32)],
        scratch_shapes=[row_scratch] * 5,
        compiler_params=_cparams("arbitrary"),
        name="hgrn_sample",
    )(x, s0, nw, win, lb, onw, wout, fnw)


def _attn_proj_kernel(x_ref, cos_ref, sin_ref, nw_ref, win_ref,
                      g0_ref, g1_ref, g2_ref, gate_ref, kv0_ref, kv1_ref, kv2_ref):
    tb = x_ref.shape[0]
    h = _rms(x_ref[...], nw_ref[...]).astype(BF16)
    cos = cos_ref[...]
    sin = sin_ref[...]

    def heads(a):
        return [a[:, hh * HEAD_DIM:(hh + 1) * HEAD_DIM] for hh in range(N_HEADS)]

    def rope(cols):
        return [ah * cos + pltpu.roll(ah, HEAD_DIM // 2, 1) * sin for ah in cols]

    def token_tiles(cols, keep):
        return jnp.swapaxes(jnp.stack([c[tb - keep:, :] for c in cols], axis=0), 0, 1)

    for gi, (qkv_ref, kv_ref) in enumerate(((g0_ref, kv0_ref), (g1_ref, kv1_ref), (g2_ref, kv2_ref))):
        base = 3 * gi * D_MODEL
        dil = qkv_ref.shape[0]
        q_cols = rope(heads(_dot(h, win_ref[:, base:base + D_MODEL])))
        k_cols = rope(heads(_dot(h, win_ref[:, base + D_MODEL:base + 2 * D_MODEL])))
        v = _dot(h, win_ref[:, base + 2 * D_MODEL:base + 3 * D_MODEL])
        q = jnp.concatenate(q_cols, axis=1) * (HEAD_DIM ** -0.5)
        k = jnp.concatenate(k_cols, axis=1)
        for ci, val in enumerate((q, k, v)):
            if dil == 1:
                val = val.reshape(1, tb, D_MODEL)
            else:
                val = jnp.swapaxes(val.reshape(tb // dil, dil, D_MODEL), 0, 1)
            qkv_ref[:, :, ci * D_MODEL:(ci + 1) * D_MODEL] = val.astype(BF16)
        keep = kv_ref.shape[0]
        kv_ref[:, 0] = token_tiles(k_cols, keep)
        kv_ref[:, 1] = token_tiles(heads(v), keep)
    gate_ref[...] = _silu(_dot(h, win_ref[:, 9 * D_MODEL:10 * D_MODEL])).astype(BF16)


def _attn_proj(x, cos, sin, nw, win, *, dils, keeps, tb):
    batch, seq, _ = x.shape
    qkv_specs, qkv_shapes, kv_specs, kv_shapes = [], [], [], []
    for dil, keep in zip(dils, keeps):
        qkv_specs.append(pl.BlockSpec((None, dil, tb // dil, 3 * D_MODEL), lambda b, t: (b, 0, t, 0)))
        qkv_shapes.append(jax.ShapeDtypeStruct((batch, dil, seq // dil, 3 * D_MODEL), BF16))
        kb = min(tb, keep)
        first = (seq - keep) // tb if keep >= tb else 0
        kv_specs.append(pl.BlockSpec(
            (None, kb, 2, N_HEADS, HEAD_DIM),
            lambda b, t, first=first, kb=kb: (b, jnp.maximum(t - first, 0) if kb == tb else 0, 0, 0, 0)))
        kv_shapes.append(jax.ShapeDtypeStruct((batch, keep, 2, N_HEADS, HEAD_DIM), F32))
    return pl.pallas_call(
        _attn_proj_kernel,
        grid=(batch, seq // tb),
        in_specs=[pl.BlockSpec((None, tb, D_MODEL), lambda b, t: (b, t, 0)),
                  pl.BlockSpec((tb, HEAD_DIM), lambda b, t: (t, 0)),
                  pl.BlockSpec((tb, HEAD_DIM), lambda b, t: (t, 0)),
                  _const_spec((1, D_MODEL)), _const_spec((D_MODEL, 10 * D_MODEL))],
        out_specs=qkv_specs + [pl.BlockSpec((None, tb, D_MODEL), lambda b, t: (b, t, 0))] + kv_specs,
        out_shape=qkv_shapes + [jax.ShapeDtypeStruct((batch, seq, D_MODEL), BF16)] + kv_shapes,
        compiler_params=_cparams("arbitrary", "arbitrary"),
        name="attn_proj",
    )(x, cos, sin, nw, win)


ATTN_UNITS = 4


def _attn_group_kernel(bps_ref, q_ref, kc_ref, kp_ref, vc_ref, vp_ref, o_ref, lse_ref):
    units = q_ref.shape[0]
    first = pl.program_id(1) * units
    blocks_per_seq = bps_ref[0]
    a = lax.broadcasted_iota(jnp.int32, (ATTN_BLOCK, ATTN_BLOCK), 0)
    c = lax.broadcasted_iota(jnp.int32, (ATTN_BLOCK, ATTN_BLOCK), 1)
    mask_cur = c <= a
    lane = lax.broadcasted_iota(jnp.int32, (ATTN_BLOCK, HEAD_DIM), 1)
    ones_col = jnp.where(lane == 0, 1.0, 0.0).astype(BF16)
    for u in range(units):
        k_prev, v_prev = (kp_ref.at[0], vp_ref.at[0]) if u == 0 else (kc_ref.at[u - 1], vc_ref.at[u - 1])
        mask_prev = (c >= a) & (lax.rem(first + u, blocks_per_seq) > 0)
        lse_tile = jnp.zeros((ATTN_BLOCK, HEAD_DIM), F32)
        for hh in range(N_HEADS):
            cs = slice(hh * HEAD_DIM, (hh + 1) * HEAD_DIM)
            qh = q_ref[u, :, cs]
            s_c = jnp.where(mask_cur, _dot_nt(qh, kc_ref[u, :, cs]), NEG_BIG)
            s_p = jnp.where(mask_prev, _dot_nt(qh, k_prev[:, cs]), NEG_BIG)
            m = jnp.max(jnp.maximum(s_c, s_p), axis=1, keepdims=True)
            ov = (_dot(jnp.exp(s_c - m).astype(BF16), jnp.concatenate([vc_ref[u, :, cs], ones_col], axis=1))
                  + _dot(jnp.exp(s_p - m).astype(BF16), jnp.concatenate([v_prev[:, cs], ones_col], axis=1)))
            den = ov[:, HEAD_DIM:HEAD_DIM + 1]
            o_ref[u, :, cs] = (ov[:, :HEAD_DIM] / den).astype(BF16)
            lse_tile = jnp.where(lane == hh, m + jnp.log(den), lse_tile)
        lse_ref[u] = lse_tile


def _attn_group(qkv):
    batch, dil, n, _ = qkv.shape
    blocks_per_seq = n // ATTN_BLOCK
    n_blocks = dil * blocks_per_seq
    units = ATTN_UNITS
    assert n_blocks % units == 0 and (blocks_per_seq % units == 0 or blocks_per_seq == 1)
    blocks = qkv.reshape(batch, n_blocks, ATTN_BLOCK, 3 * D_MODEL)

    def cur(col):
        return pl.BlockSpec((None, units, ATTN_BLOCK, D_MODEL), lambda b, i: (b, i, 0, col))

    def prev(col):
        return pl.BlockSpec((None, 1, ATTN_BLOCK, D_MODEL), lambda b, i: (b, jnp.maximum(i * units - 1, 0), 0, col))

    o, lse = pl.pallas_call(
        _attn_group_kernel,
        grid=(batch, n_blocks // units),
        in_specs=[pl.BlockSpec(memory_space=pltpu.SMEM), cur(0), cur(1), prev(1), cur(2), prev(2)],
        out_specs=[pl.BlockSpec((None, units, ATTN_BLOCK, D_MODEL), lambda b, i: (b, i, 0, 0)),
                   pl.BlockSpec((None, units, ATTN_BLOCK, HEAD_DIM), lambda b, i: (b, i, 0, 0))],
        out_shape=[jax.ShapeDtypeStruct((batch, n_blocks, ATTN_BLOCK, D_MODEL), BF16),
                   jax.ShapeDtypeStruct((batch, n_blocks, ATTN_BLOCK, HEAD_DIM), F32)],
        compiler_params=_cparams("arbitrary", "arbitrary"),
        name=f"attn_group_d{dil}",
    )(jnp.full((1,), blocks_per_seq, jnp.int32), blocks, blocks, blocks, blocks, blocks)
    return o.reshape(batch, dil, n, D_MODEL), lse.reshape(batch, dil, n, HEAD_DIM)


def _merge_heads(outs, lses):
    m = functools.reduce(jnp.maximum, lses)
    es = [jnp.exp(l - m) for l in lses]
    den = functools.reduce(jnp.add, es)
    ws = [e / den for e in es]
    cols = []
    for hh in range(N_HEADS):
        cs = slice(hh * HEAD_DIM, (hh + 1) * HEAD_DIM)
        cols.append(functools.reduce(jnp.add, [w[:, hh:hh + 1] * o[:, cs] for w, o in zip(ws, outs)]))
    return jnp.concatenate(cols, axis=1)


def _attn_out_kernel(o0_ref, o1_ref, o2_ref, l0_ref, l1_ref, l2_ref, gate_ref, x_ref, wout_ref, xo_ref):
    def natural(ref):
        dil, n, w = ref.shape
        val = ref[...].astype(F32)
        return val[0] if dil == 1 else jnp.swapaxes(val, 0, 1).reshape(dil * n, w)

    o = _merge_heads([natural(r) for r in (o0_ref, o1_ref, o2_ref)], [natural(r) for r in (l0_ref, l1_ref, l2_ref)])
    y = (o * gate_ref[...]).astype(BF16)
    xo_ref[...] = x_ref[...] + _dot(y, wout_ref[...])


def _attn_out(outs, lses, gate, x, wout, *, tb):
    batch, seq, _ = x.shape
    regrouped = lambda a: pl.BlockSpec((None, a.shape[1], tb // a.shape[1], a.shape[3]), lambda b, t: (b, 0, t, 0))
    row = pl.BlockSpec((None, tb, D_MODEL), lambda b, t: (b, t, 0))
    return pl.pallas_call(
        _attn_out_kernel,
        grid=(batch, seq // tb),
        in_specs=[regrouped(a) for a in outs] + [regrouped(a) for a in lses] + [row, row, _const_spec((D_MODEL, D_MODEL))],
        out_specs=row,
        out_shape=jax.ShapeDtypeStruct((batch, seq, D_MODEL), F32),
        compiler_params=_cparams("arbitrary", "arbitrary"),
        name="attn_out",
    )(*outs, *lses, gate, x, wout)


def _attn_sample_kernel(q0_ref, q1_ref, q2_ref, n0_ref, n1_ref, n2_ref, c0_ref, c1_ref, c2_ref,
                        gate_ref, x_ref, wout_ref, xo_ref, o_s, *, n_tok):
    nb = c0_ref.shape[0]
    pos_i = lax.broadcasted_iota(jnp.int32, (ATTN_BLOCK, N_HEADS, 1), 0)
    row_t = lax.broadcasted_iota(jnp.int32, (SAMPLE_PAD, D_MODEL), 0)
    q_refs, new_refs, cache_refs = (q0_ref, q1_ref, q2_ref), (n0_ref, n1_ref, n2_ref), (c0_ref, c1_ref, c2_ref)

    def seq_body(bi, carry):
        r0 = pl.multiple_of(bi * SAMPLE_PAD, SAMPLE_PAD)
        q_rows = [q_ref[pl.ds(r0, SAMPLE_PAD), :].astype(F32) for q_ref in q_refs]
        o_tile = jnp.zeros((SAMPLE_PAD, D_MODEL), F32)
        for t in range(n_tok):
            outs, lses = [], []
            for gi, (_, dil) in enumerate(ATTN_GROUPS):
                q_t = jnp.concatenate([q_rows[gi][t:t + 1, hh * HEAD_DIM:(hh + 1) * HEAD_DIM]
                                       for hh in range(N_HEADS)], axis=0)
                res = t % dil
                k_c = cache_refs[gi][bi, :, res, 0]
                v_c = cache_refs[gi][bi, :, res, 1]
                s_c = jnp.sum(k_c * q_t[None], axis=-1, keepdims=True)
                s_c = jnp.where(pos_i * dil + res >= t, s_c, NEG_BIG)
                new_t = [u for u in range(t + 1) if (t - u) % dil == 0]
                k_n = [new_refs[gi][r0 + u, 0] for u in new_t]
                v_n = [new_refs[gi][r0 + u, 1] for u in new_t]
                s_n = [jnp.sum(kk * q_t, axis=-1, keepdims=True) for kk in k_n]
                m = functools.reduce(jnp.maximum, s_n + [jnp.max(s_c, axis=0)])
                p_c = jnp.exp(s_c - m[None])
                p_n = [jnp.exp(s - m) for s in s_n]
                den = functools.reduce(jnp.add, p_n + [jnp.sum(p_c, axis=0)])
                acc = functools.reduce(jnp.add, [p * vv for p, vv in zip(p_n, v_n)] + [jnp.sum(p_c * v_c, axis=0)])
                outs.append(acc / den)
                lses.append(m + jnp.log(den))
            m_g = functools.reduce(jnp.maximum, lses)
            e_g = [jnp.exp(l - m_g) for l in lses]
            den_g = functools.reduce(jnp.add, e_g)
            o_t = functools.reduce(jnp.add, [e / den_g * o for e, o in zip(e_g, outs)])
            o_row = jnp.concatenate([o_t[hh:hh + 1, :] for hh in range(N_HEADS)], axis=1)
            o_tile = jnp.where(row_t == t, o_row, o_tile)
        o_s[pl.ds(r0, SAMPLE_PAD), :] = o_tile
        return carry

    lax.fori_loop(0, nb, seq_body, 0)
    y = (o_s[...] * gate_ref[...]).astype(BF16)
    xo_ref[...] = x_ref[...] + _dot(y, wout_ref[...])


def _attn_sample(qkvs, news, caches, layer, gate, x, wout, *, n_tok, nb=2):
    n_seq = caches[0].shape[1]
    rows = nb * SAMPLE_PAD
    c_views, c_specs = [], []
    for (window, dil), cache in zip(ATTN_GROUPS, caches):
        assert cache.shape[2] == window == ATTN_BLOCK * dil and (n_tok <= dil or dil == 1)
        c_views.append(cache.reshape(cache.shape[0], n_seq, ATTN_BLOCK, dil, 2, N_HEADS, HEAD_DIM))
        n_res = min(dil, n_tok)
        c_specs.append(pl.BlockSpec((None, nb, ATTN_BLOCK, n_res, 2, N_HEADS, HEAD_DIM),
                                    lambda i: (layer, i, 0, 0, 0, 0, 0)))
    row = lambda w: pl.BlockSpec((rows, w), lambda i: (i, 0))
    new_spec = pl.BlockSpec((rows, 2, N_HEADS, HEAD_DIM), lambda i: (i, 0, 0, 0))
    return pl.pallas_call(
        functools.partial(_attn_sample_kernel, n_tok=n_tok),
        grid=(n_seq // nb,),
        in_specs=[row(D_MODEL)] * 3 + [new_spec] * 3 + c_specs + [row(D_MODEL), row(D_MODEL),
                                                                  _const_spec((D_MODEL, D_MODEL))],
        out_specs=row(D_MODEL),
        out_shape=jax.ShapeDtypeStruct(x.shape, F32),
        scratch_shapes=[pltpu.VMEM((rows, D_MODEL), F32)],
        compiler_params=_cparams("arbitrary"),
        name="attn_sample",
    )(*qkvs, *news, *c_views, gate, x, wout)


ROLL_GROUPS = 8


def _cache_roll_kernel(c_ref, nxt_ref, new_ref, o_ref):
    i = pl.program_id(0)
    last = pl.num_programs(0) - 1
    g = o_ref.shape[1]
    o_ref[:, 0:g - 1] = c_ref[:, 1:g]
    o_ref[:, g - 1:g] = jnp.where(i < last, nxt_ref[...], new_ref[...])


def _cache_roll(cache, new, layer, *, n_tok):
    _, n_seq, length = cache.shape[:3]
    n_grp = length // n_tok
    g = ROLL_GROUPS
    rows = n_tok * KV_ROWS
    c_view = cache.reshape(cache.shape[0], n_seq, n_grp, rows, HEAD_DIM)
    n_view = new.reshape(n_seq, 1, rows, HEAD_DIM)
    out = pl.pallas_call(
        _cache_roll_kernel,
        grid=(n_grp // g,),
        in_specs=[pl.BlockSpec((None, n_seq, g, rows, HEAD_DIM), lambda i: (layer, 0, i, 0, 0)),
                  pl.BlockSpec((None, n_seq, 1, rows, HEAD_DIM),
                               lambda i: (layer, 0, jnp.minimum((i + 1) * g, n_grp - 1), 0, 0)),
                  pl.BlockSpec((n_seq, 1, rows, HEAD_DIM), lambda i: (0, 0, 0, 0))],
        out_specs=pl.BlockSpec((n_seq, g, rows, HEAD_DIM), lambda i: (0, i, 0, 0)),
        out_shape=jax.ShapeDtypeStruct((n_seq, n_grp, rows, HEAD_DIM), cache.dtype),
        compiler_params=_cparams("arbitrary"),
        name="cache_roll",
    )(c_view, c_view, n_view)
    return out.reshape(cache.shape[1:])


def _gelu_tanh(y):
    return 0.5 * y * (1.0 + jnp.tanh(math.sqrt(2.0 / math.pi) * (y + 0.044715 * (y * y * y))))


def _s5_kernel(x_ref, s0_ref, nw_ref, win_ref, wb_ref, are_ref, aim_ref, wc_ref, dsk_ref, wglu_ref, bglu_ref,
               wout_ref, xo_ref, sfin_ref, bu_s, st_s, *, n_seq, tb, seq_major_io):
    ti = pl.program_id(0)
    tile = 2 * S5_NSTATE // S5_KTILES
    half = tile // 2
    lanes = 1024

    @pl.when(ti == 0)
    def _():
        st_s[...] = s0_ref[...]

    if seq_major_io:
        x = jnp.swapaxes(x_ref[...], 0, 1).reshape(tb * n_seq, D_MODEL)
    else:
        x = x_ref[...]
    h = _rms(x, nw_ref[...]).astype(BF16)
    u = _dot(h, win_ref[:, 0:D_MODEL])
    gate = _silu(_dot(h, win_ref[:, D_MODEL:2 * D_MODEL]))
    ub = u.astype(BF16)
    kw = D_MODEL // S5_KTILES
    ys = []
    for kt in range(S5_KTILES):
        bu_s[:, kt * tile:(kt + 1) * tile] = _dot(ub[:, kt * kw:(kt + 1) * kw], wb_ref[kt])
        for sg in range(n_seq // 8):
            srow = slice(sg * 8, (sg + 1) * 8)
            for part in range(half // lanes):
                c_re = slice(kt * tile + part * lanes, kt * tile + (part + 1) * lanes)
                c_im = slice(kt * tile + half + part * lanes, kt * tile + half + (part + 1) * lanes)
                a_re = are_ref[:, c_re]
                a_im = aim_ref[:, c_re]
                s_re, s_im = st_s[srow, c_re], st_s[srow, c_im]
                for t in range(tb):
                    rows = slice(t * n_seq + sg * 8, t * n_seq + (sg + 1) * 8)
                    s_re, s_im = (a_re * s_re - a_im * s_im + bu_s[rows, c_re],
                                  a_re * s_im + a_im * s_re + bu_s[rows, c_im])
                    bu_s[rows, c_re] = s_re
                    bu_s[rows, c_im] = s_im
                st_s[srow, c_re] = s_re
                st_s[srow, c_im] = s_im
        ys.append(_dot(bu_s[:, kt * tile:(kt + 1) * tile].astype(BF16), wc_ref[kt]))
    y = jnp.concatenate(ys, axis=1) + dsk_ref[...] * u
    y = _gelu_tanh(y)
    y = y * jax.nn.sigmoid(_dot(y.astype(BF16), wglu_ref[...]) + bglu_ref[...])
    y = (y * gate).astype(BF16)
    xo = x + _dot(y, wout_ref[...])
    if seq_major_io:
        xo_ref[...] = jnp.swapaxes(xo.reshape(tb, n_seq, D_MODEL), 0, 1)
    else:
        xo_ref[...] = xo

    @pl.when(ti == pl.num_programs(0) - 1)
    def _():
        sfin_ref[...] = st_s[...]


def _s5_layer(x, s0, nw, win, wb, a_re, a_im, wc, dsk, wglu, bglu, wout, *, n_seq, seq, tb, seq_major_io):
    rows = tb * n_seq
    ncol = 2 * S5_NSTATE
    if seq_major_io:
        x_spec = pl.BlockSpec((n_seq, tb, D_MODEL), lambda t: (0, t, 0))
    else:
        x_spec = pl.BlockSpec((rows, D_MODEL), lambda t: (t, 0))
    return pl.pallas_call(
        functools.partial(_s5_kernel, n_seq=n_seq, tb=tb, seq_major_io=seq_major_io),
        grid=(seq // tb,),
        in_specs=[x_spec, _const_spec((n_seq, ncol)),
                  _const_spec((1, D_MODEL)), _const_spec((D_MODEL, 2 * D_MODEL)), _const_spec(wb.shape),
                  _const_spec((8, ncol)), _const_spec((8, ncol)), _const_spec(wc.shape),
                  _const_spec((1, D_MODEL)), _const_spec((D_MODEL, D_MODEL)), _const_spec((1, D_MODEL)),
                  _const_spec((D_MODEL, D_MODEL))],
        out_specs=[x_spec, pl.BlockSpec((n_seq, ncol), lambda t: (0, 0))],
        out_shape=[jax.ShapeDtypeStruct(x.shape, F32), jax.ShapeDtypeStruct((n_seq, ncol), F32)],
        scratch_shapes=[pltpu.VMEM((rows, ncol), F32), pltpu.VMEM((n_seq, ncol), F32)],
        compiler_params=_cparams("arbitrary"),
        name="s5_layer",
    )(x, s0, nw, win, wb, a_re, a_im, wc, dsk, wglu, bglu, wout)


def _s5_params(a_re, a_im, b_re, b_im, c_re, c_im, log_dt):
    lam_re = jnp.minimum(a_re, S5_MAX_RE)
    lam_im = a_im
    dt = jnp.exp(log_dt)[:, None]
    mag = jnp.exp(lam_re * dt)
    bar_re = mag * jnp.cos(lam_im * dt)
    bar_im = mag * jnp.sin(lam_im * dt)
    den = lam_re * lam_re + lam_im * lam_im
    xr = bar_re - 1.0
    coef_re = (xr * lam_re + bar_im * lam_im) / den
    coef_im = (bar_im * lam_re - xr * lam_im) / den
    bbar_re = coef_re[..., None] * b_re - coef_im[..., None] * b_im
    bbar_im = coef_re[..., None] * b_im + coef_im[..., None] * b_re
    gl = S5_GROUPS // S5_KTILES
    eye = jnp.eye(gl, dtype=F32)

    def to_cols(a):
        return a.reshape(S5_KTILES, gl * S5_STATE)

    def b_tile(bb):
        bb = bb.reshape(S5_KTILES, gl, S5_STATE, S5_GROUP_CH)
        return jnp.einsum('kgpc,gh->kgchp', bb, eye).reshape(S5_KTILES, gl * S5_GROUP_CH, gl * S5_STATE)

    def c_tile(cc):
        cc = cc.reshape(S5_KTILES, gl, S5_GROUP_CH, S5_STATE)
        return jnp.einsum('kgcp,gh->kgphc', cc, eye).reshape(S5_KTILES, gl * S5_STATE, gl * S5_GROUP_CH)

    wb = jnp.concatenate([b_tile(bbar_re), b_tile(bbar_im)], axis=2).astype(BF16)
    wc = jnp.concatenate([c_tile(c_re), -c_tile(c_im)], axis=1).astype(BF16)
    cols = lambda a: jnp.concatenate([to_cols(a), to_cols(a)], axis=1).reshape(1, -1)
    a_re_cols = jnp.broadcast_to(cols(bar_re), (8, 2 * S5_NSTATE))
    a_im_cols = jnp.broadcast_to(cols(bar_im), (8, 2 * S5_NSTATE))
    return wb, wc, a_re_cols, a_im_cols


def _s5_state_to_cols(s):
    n = s.shape[0]
    gl = S5_GROUPS // S5_KTILES
    s = s.reshape(n, S5_KTILES, gl * S5_STATE, 2)
    return jnp.moveaxis(s, 3, 2).reshape(n, 2 * S5_NSTATE)


def _s5_cols_to_state(c):
    n = c.shape[0]
    gl = S5_GROUPS // S5_KTILES
    c = c.reshape(n, S5_KTILES, 2, gl * S5_STATE)
    return jnp.moveaxis(c, 2, 3).reshape(n, S5_GROUPS, S5_STATE, 2)


def _rope_tables(pos):
    half = HEAD_DIM // 2
    inv_freq = ROPE_THETA ** (-jnp.arange(half, dtype=F32) / half)
    ang = pos[:, None] * inv_freq[None, :]
    cos, sin = jnp.cos(ang), jnp.sin(ang)
    return jnp.concatenate([cos, cos], axis=1), jnp.concatenate([-sin, sin], axis=1)


def kernel(x_prompt, x_sample, state_hgrn, cache_kv_w128, cache_kv_w512, cache_kv_w2048, state_s5,
           norm_w, final_norm_w, a_w_in, a_lb_logits, a_onorm_w, a_w_out, b_w_in, b_w_out,
           c_w_in, c_a_re, c_a_im, c_b_re, c_b_im, c_c_re, c_c_im, c_d, c_log_dt, c_w_glu, c_b_glu, c_w_out):
    batch, seq, _ = x_prompt.shape
    n_seq, n_tok, _ = x_sample.shape
    depth = norm_w.shape[0]
    caches = (cache_kv_w128, cache_kv_w512, cache_kv_w2048)
    dils = [d for _, d in ATTN_GROUPS]
    row = lambda a: a.reshape(1, -1)

    p_lb = jax.nn.softmax(a_lb_logits.astype(F32), axis=0)
    lower_bounds = jnp.cumsum(p_lb, axis=0) - p_lb[0:1]
    fnw = row(final_norm_w)

    s_rows = n_seq * SAMPLE_PAD
    xs = jnp.pad(x_sample, ((0, 0), (0, SAMPLE_PAD - n_tok), (0, 0))).reshape(s_rows, D_MODEL)
    xp = x_prompt
    pos_p = jnp.arange(seq, dtype=F32)
    pos_s = jnp.tile(jnp.pad(PAST_LEN + jnp.arange(n_tok, dtype=F32), (0, SAMPLE_PAD - n_tok)), n_seq)

    hgrn_p, hgrn_s, s5_p, s5_s = [], [], [], []
    kv_p = [[] for _ in ATTN_GROUPS]
    kv_s = [[] for _ in ATTN_GROUPS]
    for layer in range(depth):
        kind, j = layer % 3, layer // 3
        last = layer == depth - 1
        nw = row(norm_w[layer])
        if kind == 0:
            win, wout = a_w_in[j].astype(BF16), a_w_out[j].astype(BF16)
            lb, onw = row(lower_bounds[j]), row(a_onorm_w[j])
            xp, st = _hgrn_prompt(xp, nw, win, lb, onw, wout, fnw, final_norm=last)
            hgrn_p.append(st)
            xs, st = _hgrn_sample(xs, state_hgrn, j, nw, win, lb, onw, wout, fnw, n_tok=n_tok, final_norm=last)
            hgrn_s.append(st)
        elif kind == 1:
            win, wout = b_w_in[j].astype(BF16), b_w_out[j].astype(BF16)
            cos, sin = _rope_tables(pos_p)
            *qkvs, gate, kv0, kv1, kv2 = _attn_proj(xp, cos, sin, nw, win, dils=dils,
                                                    keeps=[min(w, seq) for w, _ in ATTN_GROUPS], tb=256)
            for g, kv in enumerate((kv0, kv1, kv2)):
                kv_p[g].append(kv)
            outs, lses = zip(*[_attn_group(qkv) for qkv in qkvs])
            xp = _attn_out(outs, lses, gate, xp, wout, tb=256)

            cos, sin = _rope_tables(pos_s)
            *qkvs, gate, kv0, kv1, kv2 = _attn_proj(xs.reshape(1, s_rows, D_MODEL), cos, sin, nw, win,
                                                    dils=[1] * len(dils), keeps=[s_rows] * len(dils), tb=s_rows)
            new_rows = [kv.reshape(s_rows, 2, N_HEADS, HEAD_DIM) for kv in (kv0, kv1, kv2)]
            xs = _attn_sample([q.reshape(s_rows, 3 * D_MODEL) for q in qkvs], new_rows, caches, j,
                              gate.reshape(s_rows, D_MODEL), xs, wout, n_tok=n_tok)
            news = [kv.reshape(n_seq, SAMPLE_PAD, 2, N_HEADS, HEAD_DIM)[:, :n_tok] for kv in new_rows]
            for g, (cache, new) in enumerate(zip(caches, news)):
                kv_s[g].append(_cache_roll(cache, new, j, n_tok=n_tok))
        else:
            wb, wc, a_re_cols, a_im_cols = _s5_params(c_a_re[j], c_a_im[j], c_b_re[j], c_b_im[j],
                                                      c_c_re[j], c_c_im[j], c_log_dt[j])
            wts = (nw, c_w_in[j].astype(BF16), wb, a_re_cols, a_im_cols, wc, row(c_d[j]),
                   c_w_glu[j].astype(BF16), row(c_b_glu[j]), c_w_out[j].astype(BF16))
            xp, sfin = _s5_layer(xp, jnp.zeros((batch, 2 * S5_NSTATE), F32), *wts,
                                 n_seq=batch, seq=seq, tb=32, seq_major_io=True)
            s5_p.append(_s5_cols_to_state(sfin))
            xs_tm = jnp.swapaxes(xs.reshape(n_seq, SAMPLE_PAD, D_MODEL)[:, :n_tok], 0, 1)
            xs_tm, sfin = _s5_layer(xs_tm.reshape(n_tok * n_seq, D_MODEL), _s5_state_to_cols(state_s5[j]), *wts,
                                    n_seq=n_seq, seq=n_tok, tb=n_tok, seq_major_io=False)
            s5_s.append(_s5_cols_to_state(sfin))
            xs = jnp.pad(jnp.swapaxes(xs_tm.reshape(n_tok, n_seq, D_MODEL), 0, 1),
                         ((0, 0), (0, SAMPLE_PAD - n_tok), (0, 0))).reshape(s_rows, D_MODEL)

    if depth % 3 != 1:
        raise NotImplementedError("the final norm is fused into a last HGRN2 layer")
    y_sample = xs.reshape(n_seq, SAMPLE_PAD, D_MODEL)[:, :n_tok]
    stack = lambda parts: jnp.stack(parts, axis=0)
    return (xp, y_sample, stack(hgrn_p), stack(hgrn_s),
            stack(kv_p[0]), stack(kv_s[0]), stack(kv_p[1]), stack(kv_s[1]), stack(kv_p[2]), stack(kv_s[2]),
            stack(s5_p), stack(s5_s))
```

```python
import functools
import math

import numpy as np
import jax
import jax.numpy as jnp
from jax import lax
from jax.experimental import pallas as pl
from jax.experimental.pallas import tpu as pltpu

F32 = jnp.float32
BF16 = jnp.bfloat16

D_MODEL = 1024
N_HEADS = 8
HEAD_DIM = 128
NORM_EPS = 1e-6
NEG_BIG = -1e30
PAST_LEN = 8192
ROPE_THETA = 10000.0
A_EXP_CLIP = 60.0
LOG2_E = 1.4426950408889634
ATTN_GROUPS = ((128, 1), (512, 4), (2048, 16))
ATTN_BLOCK = 128
S5_GROUPS = 64
S5_STATE = 64
S5_GROUP_CH = 16
S5_MAX_RE = -1e-4
S5_KTILES = 4
S5_NSTATE = S5_GROUPS * S5_STATE
SAMPLE_PAD = 8
KV_ROWS = 2 * N_HEADS
VMEM_LIMIT_V7X = 56 * 1024 * 1024


def _cparams(*sem):
    return pltpu.CompilerParams(dimension_semantics=sem, vmem_limit_bytes=VMEM_LIMIT_V7X)


def _const_spec(shape):
    nd = len(shape)
    return pl.BlockSpec(shape, lambda *_: (0,) * nd, pipeline_mode=pl.Buffered(1))


def _dot(a, b):
    return jnp.dot(a, b, preferred_element_type=F32)


def _dot_nt(a, b):
    return lax.dot_general(a, b, (((1,), (1,)), ((), ())), preferred_element_type=F32)


def _dot_tn(a, b):
    return lax.dot_general(a, b, (((0,), (0,)), ((), ())), preferred_element_type=F32)


def _rms(x, w):
    ms = jnp.mean(x * x, axis=-1, keepdims=True)
    return x * lax.rsqrt(ms + NORM_EPS) * w


def _silu(x):
    return x * jax.nn.sigmoid(x)


def _hgrn_gates(zf, lb):
    ez = jnp.exp(-jnp.abs(zf))
    one_ez = 1.0 + ez
    log_sig = jnp.minimum(zf, 0.0) - jnp.log(one_ez)
    log_f = log_sig + jnp.log(1.0 + lb * jnp.exp(jnp.minimum(-zf, A_EXP_CLIP)))
    r = 1.0 / one_ez
    key = (1.0 - lb) * jnp.where(zf >= 0.0, ez * r, r)
    return log_f, key


SUBLANES = 8


def _rows_from_partner(z, bit, take_upper):
    c = z.shape[0]
    step = 1 << bit
    if step < SUBLANES:
        z3 = z.reshape(c // SUBLANES, SUBLANES, z.shape[1])
        sub = lax.broadcasted_iota(jnp.int32, z3.shape, 1)
        shift = SUBLANES - step if take_upper else step
        moved = pltpu.roll(z3, shift, 1)
        keep = ((sub >> bit) & 1) == (1 if take_upper else 0)
        return jnp.where(keep, z3, moved).reshape(z.shape)
    parts = []
    for k in range(c // (2 * step)):
        src = z[2 * k * step + step:2 * (k + 1) * step] if take_upper else z[2 * k * step:2 * k * step + step]
        parts += [src, src]
    return jnp.concatenate(parts, axis=0)


def _hgrn_level_matrix(c):
    t = np.arange(c)[:, None]
    s = np.arange(c)[None, :]
    x = t ^ s
    lev = np.zeros((c, c), np.int32)
    nz = x > 0
    lev[nz] = np.floor(np.log2(x[nz])).astype(np.int32) + 1
    return np.where(s <= t, lev, -1).astype(np.int32)


def _hgrn_prompt_kernel(x_ref, nw_ref, win_ref, lb_ref, onw_ref, wout_ref, fnw_ref, tri_ref, lvl_ref,
                        xo_ref, st_ref,
                        q_s, k_s, v_s, g_s, gate_s, o_s, st_s, *, chunk, final_norm):
    ti = pl.program_id(1)
    tb = x_ref.shape[0]
    levels = int(math.log2(chunk))

    @pl.when(ti == 0)
    def _():
        st_s[...] = jnp.zeros_like(st_s)

    x = x_ref[...]
    h = _rms(x, nw_ref[...]).astype(BF16)
    q_s[...] = _silu(_dot(h, win_ref[:, 0:D_MODEL]))
    log_f, key = _hgrn_gates(_dot(h, win_ref[:, D_MODEL:2 * D_MODEL]), lb_ref[...])
    g_s[...] = log_f
    k_s[...] = key
    v_s[...] = _dot(h, win_ref[:, 2 * D_MODEL:3 * D_MODEL])
    gate_s[...] = _silu(_dot(h, win_ref[:, 3 * D_MODEL:4 * D_MODEL]))

    def head_body(ci, hh):
        rows = slice(ci * chunk, (ci + 1) * chunk)
        cs = slice(hh * HEAD_DIM, (hh + 1) * HEAD_DIM)
        qh = q_s[rows, cs]
        kh = k_s[rows, cs]
        vh = v_s[rows, cs].astype(BF16)
        gh = g_s[rows, cs]
        g_hi = gh.astype(BF16)
        g_lo = (gh - g_hi.astype(F32)).astype(BF16)
        b = _dot(tri_ref[...], jnp.concatenate([g_hi, g_lo], axis=1))
        b = b[:, :HEAD_DIM] + b[:, HEAD_DIM:]
        b_last = b[chunk - 1:chunk, :]
        st = st_s[hh]
        o = _dot_nt((qh * jnp.exp(b)).astype(BF16), st.astype(BF16))
        k_dec = (kh * jnp.exp(b_last - b)).astype(BF16)
        st_s[hh] = st * jnp.exp(b_last) + _dot_tn(vh, k_dec)
        p = _dot_nt(qh.astype(BF16), kh.astype(BF16)).astype(BF16) * lvl_ref[0]
        z = b
        for lev in range(1, levels + 1):
            bit = lev - 1
            edge = _rows_from_partner(z, bit, take_upper=False)
            e_l = jnp.exp2(jnp.abs(b - edge) * (-LOG2_E))
            sc = _dot_nt((qh * e_l).astype(BF16), (kh * e_l).astype(BF16))
            p = p + sc.astype(BF16) * lvl_ref[lev]
            if lev < levels:
                z = _rows_from_partner(z, bit, take_upper=True)
        o = o + _dot(p, vh)
        ms = jnp.mean(o * o, axis=-1, keepdims=True)
        o_s[rows, cs] = o * lax.rsqrt(ms + NORM_EPS) * onw_ref[:, cs]

    for ci in range(tb // chunk):
        for hh in range(N_HEADS):
            head_body(ci, hh)

    y = (o_s[...] * gate_s[...]).astype(BF16)
    xo = x + _dot(y, wout_ref[...])
    if final_norm:
        xo = _rms(xo, fnw_ref[...])
    xo_ref[...] = xo

    @pl.when(ti == pl.num_programs(1) - 1)
    def _():
        for hh in range(N_HEADS):
            st_ref[hh] = st_s[hh].T


def _hgrn_prompt(x, nw, win, lb, onw, wout, fnw, *, final_norm, tb=512, chunk=128):
    batch, seq, _ = x.shape
    tri = jnp.asarray(np.tril(np.ones((chunk, chunk), np.float32)), BF16)
    lvl_idx = _hgrn_level_matrix(chunk)
    lvl = jnp.asarray(np.stack([lvl_idx == lev for lev in range(int(math.log2(chunk)) + 1)]).astype(np.float32), BF16)
    row_scratch = pltpu.VMEM((tb, D_MODEL), F32)
    x_spec = pl.BlockSpec((None, tb, D_MODEL), lambda b, t: (b, t, 0))
    return pl.pallas_call(
        functools.partial(_hgrn_prompt_kernel, chunk=chunk, final_norm=final_norm),
        grid=(batch, seq // tb),
        in_specs=[x_spec, _const_spec((1, D_MODEL)), _const_spec((D_MODEL, 4 * D_MODEL)),
                  _const_spec((1, D_MODEL)), _const_spec((1, D_MODEL)), _const_spec((D_MODEL, D_MODEL)),
                  _const_spec((1, D_MODEL)), _const_spec(tri.shape), _const_spec(lvl.shape)],
        out_specs=[x_spec, pl.BlockSpec((None, N_HEADS, HEAD_DIM, HEAD_DIM), lambda b, t: (b, 0, 0, 0))],
        out_shape=[jax.ShapeDtypeStruct((batch, seq, D_MODEL), F32),
                   jax.ShapeDtypeStruct((batch, N_HEADS, HEAD_DIM, HEAD_DIM), F32)],
        scratch_shapes=[row_scratch] * 6 + [pltpu.VMEM((N_HEADS, HEAD_DIM, HEAD_DIM), F32)],
        compiler_params=_cparams("arbitrary", "arbitrary"),
        name="hgrn_prompt",
    )(x, nw, win, lb, onw, wout, fnw, tri, lvl)


def _hgrn_sample_kernel(x_ref, s0_ref, nw_ref, win_ref, lb_ref, onw_ref, wout_ref, fnw_ref,
                        xo_ref, s_ref, q_s, k_s, v_s, f_s, o_s, *, n_tok, final_norm):
    nb = s0_ref.shape[0]
    x = x_ref[...]
    h = _rms(x, nw_ref[...]).astype(BF16)
    q_s[...] = _silu(_dot(h, win_ref[:, 0:D_MODEL]))
    log_f, key = _hgrn_gates(_dot(h, win_ref[:, D_MODEL:2 * D_MODEL]), lb_ref[...])
    f_s[...] = jnp.exp(log_f)
    k_s[...] = key
    v_s[...] = _dot(h, win_ref[:, 2 * D_MODEL:3 * D_MODEL])
    gate = _silu(_dot(h, win_ref[:, 3 * D_MODEL:4 * D_MODEL]))
    row = lax.broadcasted_iota(jnp.int32, (SAMPLE_PAD, HEAD_DIM), 0)

    valid = row < n_tok

    def seq_body(bi, carry):
        rows = pl.ds(pl.multiple_of(bi * SAMPLE_PAD, SAMPLE_PAD), SAMPLE_PAD)
        for hh in range(N_HEADS):
            cs = slice(hh * HEAD_DIM, (hh + 1) * HEAD_DIM)
            qh, kh, vh = q_s[rows, cs], k_s[rows, cs], v_s[rows, cs]
            fm = jnp.where(valid, f_s[rows, cs], 1.0)
            run = [None, fm]
            for d in range(1, n_tok):
                run.append(run[d] * jnp.where(row >= d, pltpu.roll(fm, d, 0), 1.0))
            st = s0_ref[bi, hh]
            o = _dot((qh * run[n_tok]).astype(BF16), st.astype(BF16))
            o = o + jnp.sum(qh * kh, axis=-1, keepdims=True) * vh
            for d in range(1, n_tok):
                s_d = jnp.sum(qh * pltpu.roll(kh, d, 0) * run[d], axis=-1, keepdims=True)
                o = o + jnp.where(row >= d, s_d, 0.0) * pltpu.roll(vh, d, 0)
            after = jnp.ones_like(fm)
            for d in range(1, n_tok):
                after = after * pltpu.roll(fm, SAMPLE_PAD - d, 0)
            k_dec = jnp.where(valid, kh * after, 0.0)
            total = run[n_tok].T[:, n_tok - 1:n_tok]
            s_ref[bi, hh] = st * total + _dot_tn(k_dec, vh)
            ms = jnp.mean(o * o, axis=-1, keepdims=True)
            o_s[rows, cs] = o * lax.rsqrt(ms + NORM_EPS) * onw_ref[:, cs]
        return carry

    lax.fori_loop(0, nb, seq_body, 0)
    y = (o_s[...] * gate).astype(BF16)
    xo = x + _dot(y, wout_ref[...])
    if final_norm:
        xo = _rms(xo, fnw_ref[...])
    xo_ref[...] = xo


def _hgrn_sample(x, s0, layer, nw, win, lb, onw, wout, fnw, *, n_tok, final_norm, nb=8):
    n_seq = s0.shape[1]
    assert 2 * n_tok - 1 <= SAMPLE_PAD
    rows = nb * SAMPLE_PAD
    row_scratch = pltpu.VMEM((rows, D_MODEL), F32)
    return pl.pallas_call(
        functools.partial(_hgrn_sample_kernel, n_tok=n_tok, final_norm=final_norm),
        grid=(n_seq // nb,),
        in_specs=[pl.BlockSpec((rows, D_MODEL), lambda i: (i, 0)),
                  pl.BlockSpec((None, nb, N_HEADS, HEAD_DIM, HEAD_DIM), lambda i: (layer, i, 0, 0, 0)),
                  _const_spec((1, D_MODEL)), _const_spec((D_MODEL, 4 * D_MODEL)), _const_spec((1, D_MODEL)),
                  _const_spec((1, D_MODEL)), _const_spec((D_MODEL, D_MODEL)), _const_spec((1, D_MODEL))],
        out_specs=[pl.BlockSpec((rows, D_MODEL), lambda i: (i, 0)),
                   pl.BlockSpec((nb, N_HEADS, HEAD_DIM, HEAD_DIM), lambda i: (i, 0, 0, 0))],
        out_shape=[jax.ShapeDtypeStruct(x.shape, F32), jax.ShapeDtypeStruct(s0.shape[1:], F32)],
        scratch_shapes=[row_scratch] * 5,
        compiler_params=_cparams("arbitrary"),
        name="hgrn_sample",
    )(x, s0, nw, win, lb, onw, wout, fnw)


def _attn_proj_kernel(x_ref, cos_ref, sin_ref, nw_ref, win_ref,
                      g0_ref, g1_ref, g2_ref, gate_ref, kv0_ref, kv1_ref, kv2_ref):
    tb = x_ref.shape[0]
    h = _rms(x_ref[...], nw_ref[...]).astype(BF16)
    cos = cos_ref[...]
    sin = sin_ref[...]

    def heads(a):
        return [a[:, hh * HEAD_DIM:(hh + 1) * HEAD_DIM] for hh in range(N_HEADS)]

    def rope(cols):
        return [ah * cos + pltpu.roll(ah, HEAD_DIM // 2, 1) * sin for ah in cols]

    def token_tiles(cols, keep):
        return jnp.swapaxes(jnp.stack([c[tb - keep:, :] for c in cols], axis=0), 0, 1)

    for gi, (qkv_ref, kv_ref) in enumerate(((g0_ref, kv0_ref), (g1_ref, kv1_ref), (g2_ref, kv2_ref))):
        base = 3 * gi * D_MODEL
        dil = qkv_ref.shape[0]
        q_cols = rope(heads(_dot(h, win_ref[:, base:base + D_MODEL])))
        k_cols = rope(heads(_dot(h, win_ref[:, base + D_MODEL:base + 2 * D_MODEL])))
        v = _dot(h, win_ref[:, base + 2 * D_MODEL:base + 3 * D_MODEL])
        q = jnp.concatenate(q_cols, axis=1) * (HEAD_DIM ** -0.5)
        k = jnp.concatenate(k_cols, axis=1)
        for ci, val in enumerate((q, k, v)):
            if dil == 1:
                val = val.reshape(1, tb, D_MODEL)
            else:
                val = jnp.swapaxes(val.reshape(tb // dil, dil, D_MODEL), 0, 1)
            qkv_ref[:, :, ci * D_MODEL:(ci + 1) * D_MODEL] = val.astype(BF16)
        keep = kv_ref.shape[0]
        kv_ref[:, 0] = token_tiles(k_cols, keep)
        kv_ref[:, 1] = token_tiles(heads(v), keep)
    gate_ref[...] = _silu(_dot(h, win_ref[:, 9 * D_MODEL:10 * D_MODEL])).astype(BF16)


def _attn_proj(x, cos, sin, nw, win, *, dils, keeps, tb):
    batch, seq, _ = x.shape
    qkv_specs, qkv_shapes, kv_specs, kv_shapes = [], [], [], []
    for dil, keep in zip(dils, keeps):
        qkv_specs.append(pl.BlockSpec((None, dil, tb // dil, 3 * D_MODEL), lambda b, t: (b, 0, t, 0)))
        qkv_shapes.append(jax.ShapeDtypeStruct((batch, dil, seq // dil, 3 * D_MODEL), BF16))
        kb = min(tb, keep)
        first = (seq - keep) // tb if keep >= tb else 0
        kv_specs.append(pl.BlockSpec(
            (None, kb, 2, N_HEADS, HEAD_DIM),
            lambda b, t, first=first, kb=kb: (b, jnp.maximum(t - first, 0) if kb == tb else 0, 0, 0, 0)))
        kv_shapes.append(jax.ShapeDtypeStruct((batch, keep, 2, N_HEADS, HEAD_DIM), F32))
    return pl.pallas_call(
        _attn_proj_kernel,
        grid=(batch, seq // tb),
        in_specs=[pl.BlockSpec((None, tb, D_MODEL), lambda b, t: (b, t, 0)),
                  pl.BlockSpec((tb, HEAD_DIM), lambda b, t: (t, 0)),
                  pl.BlockSpec((tb, HEAD_DIM), lambda b, t: (t, 0)),
                  _const_spec((1, D_MODEL)), _const_spec((D_MODEL, 10 * D_MODEL))],
        out_specs=qkv_specs + [pl.BlockSpec((None, tb, D_MODEL), lambda b, t: (b, t, 0))] + kv_specs,
        out_shape=qkv_shapes + [jax.ShapeDtypeStruct((batch, seq, D_MODEL), BF16)] + kv_shapes,
        compiler_params=_cparams("arbitrary", "arbitrary"),
        name="attn_proj",
    )(x, cos, sin, nw, win)


ATTN_UNITS = 4


def _attn_group_kernel(bps_ref, q_ref, kc_ref, kp_ref, vc_ref, vp_ref, o_ref, lse_ref):
    units = q_ref.shape[0]
    first = pl.program_id(1) * units
    blocks_per_seq = bps_ref[0]
    a = lax.broadcasted_iota(jnp.int32, (ATTN_BLOCK, ATTN_BLOCK), 0)
    c = lax.broadcasted_iota(jnp.int32, (ATTN_BLOCK, ATTN_BLOCK), 1)
    mask_cur = c <= a
    lane = lax.broadcasted_iota(jnp.int32, (ATTN_BLOCK, HEAD_DIM), 1)
    ones_col = jnp.where(lane == 0, 1.0, 0.0).astype(BF16)
    for u in range(units):
        k_prev, v_prev = (kp_ref.at[0], vp_ref.at[0]) if u == 0 else (kc_ref.at[u - 1], vc_ref.at[u - 1])
        mask_prev = (c >= a) & (lax.rem(first + u, blocks_per_seq) > 0)
        mask = jnp.concatenate([mask_prev, mask_cur], axis=1)
        lse_tile = jnp.zeros((ATTN_BLOCK, HEAD_DIM), F32)
        for hh in range(N_HEADS):
            cs = slice(hh * HEAD_DIM, (hh + 1) * HEAD_DIM)
            keys = jnp.concatenate([k_prev[:, cs], kc_ref[u, :, cs]], axis=0)
            vals = jnp.concatenate([jnp.concatenate([v_prev[:, cs], ones_col], axis=1),
                                    jnp.concatenate([vc_ref[u, :, cs], ones_col], axis=1)], axis=0)
            s = jnp.where(mask, _dot_nt(q_ref[u, :, cs], keys), NEG_BIG)
            m = jnp.max(s, axis=1, keepdims=True)
            ov = _dot(jnp.exp(s - m).astype(BF16), vals)
            den = ov[:, HEAD_DIM:HEAD_DIM + 1]
            o_ref[u, :, cs] = (ov[:, :HEAD_DIM] / den).astype(BF16)
            lse_tile = jnp.where(lane == hh, m + jnp.log(den), lse_tile)
        lse_ref[u] = lse_tile


def _attn_group(qkv):
    batch, dil, n, _ = qkv.shape
    blocks_per_seq = n // ATTN_BLOCK
    n_blocks = dil * blocks_per_seq
    units = ATTN_UNITS
    assert n_blocks % units == 0 and (blocks_per_seq % units == 0 or blocks_per_seq == 1)
    blocks = qkv.reshape(batch, n_blocks, ATTN_BLOCK, 3 * D_MODEL)

    def cur(col):
        return pl.BlockSpec((None, units, ATTN_BLOCK, D_MODEL), lambda b, i: (b, i, 0, col))

    def prev(col):
        return pl.BlockSpec((None, 1, ATTN_BLOCK, D_MODEL), lambda b, i: (b, jnp.maximum(i * units - 1, 0), 0, col))

    o, lse = pl.pallas_call(
        _attn_group_kernel,
        grid=(batch, n_blocks // units),
        in_specs=[pl.BlockSpec(memory_space=pltpu.SMEM), cur(0), cur(1), prev(1), cur(2), prev(2)],
        out_specs=[pl.BlockSpec((None, units, ATTN_BLOCK, D_MODEL), lambda b, i: (b, i, 0, 0)),
                   pl.BlockSpec((None, units, ATTN_BLOCK, HEAD_DIM), lambda b, i: (b, i, 0, 0))],
        out_shape=[jax.ShapeDtypeStruct((batch, n_blocks, ATTN_BLOCK, D_MODEL), BF16),
                   jax.ShapeDtypeStruct((batch, n_blocks, ATTN_BLOCK, HEAD_DIM), F32)],
        compiler_params=_cparams("arbitrary", "arbitrary"),
        name=f"attn_group_d{dil}",
    )(jnp.full((1,), blocks_per_seq, jnp.int32), blocks, blocks, blocks, blocks, blocks)
    return o.reshape(batch, dil, n, D_MODEL), lse.reshape(batch, dil, n, HEAD_DIM)


def _merge_heads(outs, lses):
    m = functools.reduce(jnp.maximum, lses)
    es = [jnp.exp(l - m) for l in lses]
    den = functools.reduce(jnp.add, es)
    ws = [e / den for e in es]
    cols = []
    for hh in range(N_HEADS):
        cs = slice(hh * HEAD_DIM, (hh + 1) * HEAD_DIM)
        cols.append(functools.reduce(jnp.add, [w[:, hh:hh + 1] * o[:, cs] for w, o in zip(ws, outs)]))
    return jnp.concatenate(cols, axis=1)


def _attn_out_kernel(o0_ref, o1_ref, o2_ref, l0_ref, l1_ref, l2_ref, gate_ref, x_ref, wout_ref, xo_ref):
    def natural(ref):
        dil, n, w = ref.shape
        val = ref[...].astype(F32)
        return val[0] if dil == 1 else jnp.swapaxes(val, 0, 1).reshape(dil * n, w)

    o = _merge_heads([natural(r) for r in (o0_ref, o1_ref, o2_ref)], [natural(r) for r in (l0_ref, l1_ref, l2_ref)])
    y = (o * gate_ref[...]).astype(BF16)
    xo_ref[...] = x_ref[...] + _dot(y, wout_ref[...])


def _attn_out(outs, lses, gate, x, wout, *, tb):
    batch, seq, _ = x.shape
    regrouped = lambda a: pl.BlockSpec((None, a.shape[1], tb // a.shape[1], a.shape[3]), lambda b, t: (b, 0, t, 0))
    row = pl.BlockSpec((None, tb, D_MODEL), lambda b, t: (b, t, 0))
    return pl.pallas_call(
        _attn_out_kernel,
        grid=(batch, seq // tb),
        in_specs=[regrouped(a) for a in outs] + [regrouped(a) for a in lses] + [row, row, _const_spec((D_MODEL, D_MODEL))],
        out_specs=row,
        out_shape=jax.ShapeDtypeStruct((batch, seq, D_MODEL), F32),
        compiler_params=_cparams("arbitrary", "arbitrary"),
        name="attn_out",
    )(*outs, *lses, gate, x, wout)


def _attn_sample_kernel(q0_ref, q1_ref, q2_ref, n0_ref, n1_ref, n2_ref, c0_ref, c1_ref, c2_ref,
                        gate_ref, x_ref, wout_ref, xo_ref, o_s, *, n_tok):
    nb = c0_ref.shape[0]
    pos_i = lax.broadcasted_iota(jnp.int32, (ATTN_BLOCK, N_HEADS, 1), 0)
    row_t = lax.broadcasted_iota(jnp.int32, (SAMPLE_PAD, D_MODEL), 0)
    q_refs, new_refs, cache_refs = (q0_ref, q1_ref, q2_ref), (n0_ref, n1_ref, n2_ref), (c0_ref, c1_ref, c2_ref)

    def seq_body(bi, carry):
        r0 = pl.multiple_of(bi * SAMPLE_PAD, SAMPLE_PAD)
        q_rows = [q_ref[pl.ds(r0, SAMPLE_PAD), :].astype(F32) for q_ref in q_refs]
        o_tile = jnp.zeros((SAMPLE_PAD, D_MODEL), F32)
        for t in range(n_tok):
            outs, lses = [], []
            for gi, (_, dil) in enumerate(ATTN_GROUPS):
                q_t = jnp.concatenate([q_rows[gi][t:t + 1, hh * HEAD_DIM:(hh + 1) * HEAD_DIM]
                                       for hh in range(N_HEADS)], axis=0)
                res = t % dil
                k_c = cache_refs[gi][bi, :, res, 0]
                v_c = cache_refs[gi][bi, :, res, 1]
                s_c = jnp.sum(k_c * q_t[None], axis=-1, keepdims=True)
                s_c = jnp.where(pos_i * dil + res >= t, s_c, NEG_BIG)
                new_t = [u for u in range(t + 1) if (t - u) % dil == 0]
                k_n = [new_refs[gi][r0 + u, 0] for u in new_t]
                v_n = [new_refs[gi][r0 + u, 1] for u in new_t]
                s_n = [jnp.sum(kk * q_t, axis=-1, keepdims=True) for kk in k_n]
                m = functools.reduce(jnp.maximum, s_n + [jnp.max(s_c, axis=0)])
                p_c = jnp.exp(s_c - m[None])
                p_n = [jnp.exp(s - m) for s in s_n]
                den = functools.reduce(jnp.add, p_n + [jnp.sum(p_c, axis=0)])
                acc = functools.reduce(jnp.add, [p * vv for p, vv in zip(p_n, v_n)] + [jnp.sum(p_c * v_c, axis=0)])
                outs.append(acc / den)
                lses.append(m + jnp.log(den))
            m_g = functools.reduce(jnp.maximum, lses)
            e_g = [jnp.exp(l - m_g) for l in lses]
            den_g = functools.reduce(jnp.add, e_g)
            o_t = functools.reduce(jnp.add, [e / den_g * o for e, o in zip(e_g, outs)])
            o_row = jnp.concatenate([o_t[hh:hh + 1, :] for hh in range(N_HEADS)], axis=1)
            o_tile = jnp.where(row_t == t, o_row, o_tile)
        o_s[pl.ds(r0, SAMPLE_PAD), :] = o_tile
        return carry

    lax.fori_loop(0, nb, seq_body, 0)
    y = (o_s[...] * gate_ref[...]).astype(BF16)
    xo_ref[...] = x_ref[...] + _dot(y, wout_ref[...])


def _attn_sample(qkvs, news, caches, layer, gate, x, wout, *, n_tok, nb=2):
    n_seq = caches[0].shape[1]
    rows = nb * SAMPLE_PAD
    c_views, c_specs = [], []
    for (window, dil), cache in zip(ATTN_GROUPS, caches):
        assert cache.shape[2] == window == ATTN_BLOCK * dil and (n_tok <= dil or dil == 1)
        c_views.append(cache.reshape(cache.shape[0], n_seq, ATTN_BLOCK, dil, 2, N_HEADS, HEAD_DIM))
        n_res = min(dil, n_tok)
        c_specs.append(pl.BlockSpec((None, nb, ATTN_BLOCK, n_res, 2, N_HEADS, HEAD_DIM),
                                    lambda i: (layer, i, 0, 0, 0, 0, 0)))
    row = lambda w: pl.BlockSpec((rows, w), lambda i: (i, 0))
    new_spec = pl.BlockSpec((rows, 2, N_HEADS, HEAD_DIM), lambda i: (i, 0, 0, 0))
    return pl.pallas_call(
        functools.partial(_attn_sample_kernel, n_tok=n_tok),
        grid=(n_seq // nb,),
        in_specs=[row(D_MODEL)] * 3 + [new_spec] * 3 + c_specs + [row(D_MODEL), row(D_MODEL),
                                                                  _const_spec((D_MODEL, D_MODEL))],
        out_specs=row(D_MODEL),
        out_shape=jax.ShapeDtypeStruct(x.shape, F32),
        scratch_shapes=[pltpu.VMEM((rows, D_MODEL), F32)],
        compiler_params=_cparams("arbitrary"),
        name="attn_sample",
    )(*qkvs, *news, *c_views, gate, x, wout)


ROLL_GROUPS = 8


def _cache_roll_kernel(c_ref, nxt_ref, new_ref, o_ref):
    i = pl.program_id(0)
    last = pl.num_programs(0) - 1
    g = o_ref.shape[1]
    o_ref[:, 0:g - 1] = c_ref[:, 1:g]
    o_ref[:, g - 1:g] = jnp.where(i < last, nxt_ref[...], new_ref[...])


def _cache_roll(cache, new, layer, *, n_tok):
    _, n_seq, length = cache.shape[:3]
    n_grp = length // n_tok
    g = ROLL_GROUPS
    rows = n_tok * KV_ROWS
    c_view = cache.reshape(cache.shape[0], n_seq, n_grp, rows, HEAD_DIM)
    n_view = new.reshape(n_seq, 1, rows, HEAD_DIM)
    out = pl.pallas_call(
        _cache_roll_kernel,
        grid=(n_grp // g,),
        in_specs=[pl.BlockSpec((None, n_seq, g, rows, HEAD_DIM), lambda i: (layer, 0, i, 0, 0)),
                  pl.BlockSpec((None, n_seq, 1, rows, HEAD_DIM),
                               lambda i: (layer, 0, jnp.minimum((i + 1) * g, n_grp - 1), 0, 0)),
                  pl.BlockSpec((n_seq, 1, rows, HEAD_DIM), lambda i: (0, 0, 0, 0))],
        out_specs=pl.BlockSpec((n_seq, g, rows, HEAD_DIM), lambda i: (0, i, 0, 0)),
        out_shape=jax.ShapeDtypeStruct((n_seq, n_grp, rows, HEAD_DIM), cache.dtype),
        compiler_params=_cparams("arbitrary"),
        name="cache_roll",
    )(c_view, c_view, n_view)
    return out.reshape(cache.shape[1:])


def _gelu_tanh(y):
    return 0.5 * y * (1.0 + jnp.tanh(math.sqrt(2.0 / math.pi) * (y + 0.044715 * (y * y * y))))


def _s5_kernel(x_ref, s0_ref, nw_ref, win_ref, wb_ref, are_ref, aim_ref, wc_ref, dsk_ref, wglu_ref, bglu_ref,
               wout_ref, xo_ref, sfin_ref, bu_s, st_s, *, n_seq, tb, seq_major_io):
    ti = pl.program_id(0)
    tile = 2 * S5_NSTATE // S5_KTILES
    half = tile // 2
    lanes = 1024

    @pl.when(ti == 0)
    def _():
        st_s[...] = s0_ref[...]

    if seq_major_io:
        x = jnp.swapaxes(x_ref[...], 0, 1).reshape(tb * n_seq, D_MODEL)
    else:
        x = x_ref[...]
    h = _rms(x, nw_ref[...]).astype(BF16)
    u = _dot(h, win_ref[:, 0:D_MODEL])
    gate = _silu(_dot(h, win_ref[:, D_MODEL:2 * D_MODEL]))
    ub = u.astype(BF16)
    kw = D_MODEL // S5_KTILES
    ys = []
    for kt in range(S5_KTILES):
        bu_s[:, kt * tile:(kt + 1) * tile] = _dot(ub[:, kt * kw:(kt + 1) * kw], wb_ref[kt])
        for sg in range(n_seq // 8):
            srow = slice(sg * 8, (sg + 1) * 8)
            for part in range(half // lanes):
                c_re = slice(kt * tile + part * lanes, kt * tile + (part + 1) * lanes)
                c_im = slice(kt * tile + half + part * lanes, kt * tile + half + (part + 1) * lanes)
                a_re = are_ref[:, c_re]
                a_im = aim_ref[:, c_re]
                s_re, s_im = st_s[srow, c_re], st_s[srow, c_im]
                for t in range(tb):
                    rows = slice(t * n_seq + sg * 8, t * n_seq + (sg + 1) * 8)
                    s_re, s_im = (a_re * s_re - a_im * s_im + bu_s[rows, c_re],
                                  a_re * s_im + a_im * s_re + bu_s[rows, c_im])
                    bu_s[rows, c_re] = s_re
                    bu_s[rows, c_im] = s_im
                st_s[srow, c_re] = s_re
                st_s[srow, c_im] = s_im
        ys.append(_dot(bu_s[:, kt * tile:(kt + 1) * tile].astype(BF16), wc_ref[kt]))
    y = jnp.concatenate(ys, axis=1) + dsk_ref[...] * u
    y = _gelu_tanh(y)
    y = y * jax.nn.sigmoid(_dot(y.astype(BF16), wglu_ref[...]) + bglu_ref[...])
    y = (y * gate).astype(BF16)
    xo = x + _dot(y, wout_ref[...])
    if seq_major_io:
        xo_ref[...] = jnp.swapaxes(xo.reshape(tb, n_seq, D_MODEL), 0, 1)
    else:
        xo_ref[...] = xo

    @pl.when(ti == pl.num_programs(0) - 1)
    def _():
        sfin_ref[...] = st_s[...]


def _s5_layer(x, s0, nw, win, wb, a_re, a_im, wc, dsk, wglu, bglu, wout, *, n_seq, seq, tb, seq_major_io):
    rows = tb * n_seq
    ncol = 2 * S5_NSTATE
    if seq_major_io:
        x_spec = pl.BlockSpec((n_seq, tb, D_MODEL), lambda t: (0, t, 0))
    else:
        x_spec = pl.BlockSpec((rows, D_MODEL), lambda t: (t, 0))
    return pl.pallas_call(
        functools.partial(_s5_kernel, n_seq=n_seq, tb=tb, seq_major_io=seq_major_io),
        grid=(seq // tb,),
        in_specs=[x_spec, _const_spec((n_seq, ncol)),
                  _const_spec((1, D_MODEL)), _const_spec((D_MODEL, 2 * D_MODEL)), _const_spec(wb.shape),
                  _const_spec((8, ncol)), _const_spec((8, ncol)), _const_spec(wc.shape),
                  _const_spec((1, D_MODEL)), _const_spec((D_MODEL, D_MODEL)), _const_spec((1, D_MODEL)),
                  _const_spec((D_MODEL, D_MODEL))],
        out_specs=[x_spec, pl.BlockSpec((n_seq, ncol), lambda t: (0, 0))],
        out_shape=[jax.ShapeDtypeStruct(x.shape, F32), jax.ShapeDtypeStruct((n_seq, ncol), F32)],
        scratch_shapes=[pltpu.VMEM((rows, ncol), F32), pltpu.VMEM((n_seq, ncol), F32)],
        compiler_params=_cparams("arbitrary"),
        name="s5_layer",
    )(x, s0, nw, win, wb, a_re, a_im, wc, dsk, wglu, bglu, wout)


def _s5_params(a_re, a_im, b_re, b_im, c_re, c_im, log_dt):
    lam_re = jnp.minimum(a_re, S5_MAX_RE)
    lam_im = a_im
    dt = jnp.exp(log_dt)[:, None]
    mag = jnp.exp(lam_re * dt)
    bar_re = mag * jnp.cos(lam_im * dt)
    bar_im = mag * jnp.sin(lam_im * dt)
    den = lam_re * lam_re + lam_im * lam_im
    xr = bar_re - 1.0
    coef_re = (xr * lam_re + bar_im * lam_im) / den
    coef_im = (bar_im * lam_re - xr * lam_im) / den
    bbar_re = coef_re[..., None] * b_re - coef_im[..., None] * b_im
    bbar_im = coef_re[..., None] * b_im + coef_im[..., None] * b_re
    gl = S5_GROUPS // S5_KTILES
    eye = jnp.eye(gl, dtype=F32)

    def to_cols(a):
        return a.reshape(S5_KTILES, gl * S5_STATE)

    def b_tile(bb):
        bb = bb.reshape(S5_KTILES, gl, S5_STATE, S5_GROUP_CH)
        return jnp.einsum('kgpc,gh->kgchp', bb, eye).reshape(S5_KTILES, gl * S5_GROUP_CH, gl * S5_STATE)

    def c_tile(cc):
        cc = cc.reshape(S5_KTILES, gl, S5_GROUP_CH, S5_STATE)
        return jnp.einsum('kgcp,gh->kgphc', cc, eye).reshape(S5_KTILES, gl * S5_STATE, gl * S5_GROUP_CH)

    wb = jnp.concatenate([b_tile(bbar_re), b_tile(bbar_im)], axis=2).astype(BF16)
    wc = jnp.concatenate([c_tile(c_re), -c_tile(c_im)], axis=1).astype(BF16)
    cols = lambda a: jnp.concatenate([to_cols(a), to_cols(a)], axis=1).reshape(1, -1)
    a_re_cols = jnp.broadcast_to(cols(bar_re), (8, 2 * S5_NSTATE))
    a_im_cols = jnp.broadcast_to(cols(bar_im), (8, 2 * S5_NSTATE))
    return wb, wc, a_re_cols, a_im_cols


def _s5_state_to_cols(s):
    n = s.shape[0]
    gl = S5_GROUPS // S5_KTILES
    s = s.reshape(n, S5_KTILES, gl * S5_STATE, 2)
    return jnp.moveaxis(s, 3, 2).reshape(n, 2 * S5_NSTATE)


def _s5_cols_to_state(c):
    n = c.shape[0]
    gl = S5_GROUPS // S5_KTILES
    c = c.reshape(n, S5_KTILES, 2, gl * S5_STATE)
    return jnp.moveaxis(c, 2, 3).reshape(n, S5_GROUPS, S5_STATE, 2)


def _rope_tables(pos):
    half = HEAD_DIM // 2
    inv_freq = ROPE_THETA ** (-jnp.arange(half, dtype=F32) / half)
    ang = pos[:, None] * inv_freq[None, :]
    cos, sin = jnp.cos(ang), jnp.sin(ang)
    return jnp.concatenate([cos, cos], axis=1), jnp.concatenate([-sin, sin], axis=1)


def kernel(x_prompt, x_sample, state_hgrn, cache_kv_w128, cache_kv_w512, cache_kv_w2048, state_s5,
           norm_w, final_norm_w, a_w_in, a_lb_logits, a_onorm_w, a_w_out, b_w_in, b_w_out,
           c_w_in, c_a_re, c_a_im, c_b_re, c_b_im, c_c_re, c_c_im, c_d, c_log_dt, c_w_glu, c_b_glu, c_w_out):
    batch, seq, _ = x_prompt.shape
    n_seq, n_tok, _ = x_sample.shape
    depth = norm_w.shape[0]
    caches = (cache_kv_w128, cache_kv_w512, cache_kv_w2048)
    dils = [d for _, d in ATTN_GROUPS]
    row = lambda a: a.reshape(1, -1)

    p_lb = jax.nn.softmax(a_lb_logits.astype(F32), axis=0)
    lower_bounds = jnp.cumsum(p_lb, axis=0) - p_lb[0:1]
    fnw = row(final_norm_w)

    s_rows = n_seq * SAMPLE_PAD
    xs = jnp.pad(x_sample, ((0, 0), (0, SAMPLE_PAD - n_tok), (0, 0))).reshape(s_rows, D_MODEL)
    xp = x_prompt
    pos_p = jnp.arange(seq, dtype=F32)
    pos_s = jnp.tile(jnp.pad(PAST_LEN + jnp.arange(n_tok, dtype=F32), (0, SAMPLE_PAD - n_tok)), n_seq)

    hgrn_p, hgrn_s, s5_p, s5_s = [], [], [], []
    kv_p = [[] for _ in ATTN_GROUPS]
    kv_s = [[] for _ in ATTN_GROUPS]
    for layer in range(depth):
        kind, j = layer % 3, layer // 3
        last = layer == depth - 1
        nw = row(norm_w[layer])
        if kind == 0:
            win, wout = a_w_in[j].astype(BF16), a_w_out[j].astype(BF16)
            lb, onw = row(lower_bounds[j]), row(a_onorm_w[j])
            xp, st = _hgrn_prompt(xp, nw, win, lb, onw, wout, fnw, final_norm=last)
            hgrn_p.append(st)
            xs, st = _hgrn_sample(xs, state_hgrn, j, nw, win, lb, onw, wout, fnw, n_tok=n_tok, final_norm=last)
            hgrn_s.append(st)
        elif kind == 1:
            win, wout = b_w_in[j].astype(BF16), b_w_out[j].astype(BF16)
            cos, sin = _rope_tables(pos_p)
            *qkvs, gate, kv0, kv1, kv2 = _attn_proj(xp, cos, sin, nw, win, dils=dils,
                                                    keeps=[min(w, seq) for w, _ in ATTN_GROUPS], tb=256)
            for g, kv in enumerate((kv0, kv1, kv2)):
                kv_p[g].append(kv)
            outs, lses = zip(*[_attn_group(qkv) for qkv in qkvs])
            xp = _attn_out(outs, lses, gate, xp, wout, tb=256)

            cos, sin = _rope_tables(pos_s)
            *qkvs, gate, kv0, kv1, kv2 = _attn_proj(xs.reshape(1, s_rows, D_MODEL), cos, sin, nw, win,
                                                    dils=[1] * len(dils), keeps=[s_rows] * len(dils), tb=s_rows)
            new_rows = [kv.reshape(s_rows, 2, N_HEADS, HEAD_DIM) for kv in (kv0, kv1, kv2)]
            xs = _attn_sample([q.reshape(s_rows, 3 * D_MODEL) for q in qkvs], new_rows, caches, j,
                              gate.reshape(s_rows, D_MODEL), xs, wout, n_tok=n_tok)
            news = [kv.reshape(n_seq, SAMPLE_PAD, 2, N_HEADS, HEAD_DIM)[:, :n_tok] for kv in new_rows]
            for g, (cache, new) in enumerate(zip(caches, news)):
                kv_s[g].append(_cache_roll(cache, new, j, n_tok=n_tok))
        else:
            wb, wc, a_re_cols, a_im_cols = _s5_params(c_a_re[j], c_a_im[j], c_b_re[j], c_b_im[j],
                                                      c_c_re[j], c_c_im[j], c_log_dt[j])
            wts = (nw, c_w_in[j].astype(BF16), wb, a_re_cols, a_im_cols, wc, row(c_d[j]),
                   c_w_glu[j].astype(BF16), row(c_b_glu[j]), c_w_out[j].astype(BF16))
            xp, sfin = _s5_layer(xp, jnp.zeros((batch, 2 * S5_NSTATE), F32), *wts,
                                 n_seq=batch, seq=seq, tb=64, seq_major_io=True)
            s5_p.append(_s5_cols_to_state(sfin))
            xs_tm = jnp.swapaxes(xs.reshape(n_seq, SAMPLE_PAD, D_MODEL)[:, :n_tok], 0, 1)
            xs_tm, sfin = _s5_layer(xs_tm.reshape(n_tok * n_seq, D_MODEL), _s5_state_to_cols(state_s5[j]), *wts,
                                    n_seq=n_seq, seq=n_tok, tb=n_tok, seq_major_io=False)
            s5_s.append(_s5_cols_to_state(sfin))
            xs = jnp.pad(jnp.swapaxes(xs_tm.reshape(n_tok, n_seq, D_MODEL), 0, 1),
                         ((0, 0), (0, SAMPLE_PAD - n_tok), (0, 0))).reshape(s_rows, D_MODEL)

    if depth % 3 != 1:
        raise NotImplementedError("the final norm is fused into a last HGRN2 layer")
    y_sample = xs.reshape(n_seq, SAMPLE_PAD, D_MODEL)[:, :n_tok]
    stack = lambda parts: jnp.stack(parts, axis=0)
    return (xp, y_sample, stack(hgrn_p), stack(hgrn_s),
            stack(kv_p[0]), stack(kv_s[0]), stack(kv_p[1]), stack(kv_s[1]), stack(kv_p[2]), stack(kv_s[2]),
            stack(s5_p), stack(s5_s))
```

```python
import functools
import math

import numpy as np
import jax
import jax.numpy as jnp
from jax import lax
from jax.experimental import pallas as pl
from jax.experimental.pallas import tpu as pltpu

F32 = jnp.float32
BF16 = jnp.bfloat16

D_MODEL = 1024
N_HEADS = 8
HEAD_DIM = 128
NORM_EPS = 1e-6
NEG_BIG = -1e30
PAST_LEN = 8192
ROPE_THETA = 10000.0
A_EXP_CLIP = 60.0
LOG2_E = 1.4426950408889634
ATTN_GROUPS = ((128, 1), (512, 4), (2048, 16))
ATTN_BLOCK = 128
S5_GROUPS = 64
S5_STATE = 64
S5_GROUP_CH = 16
S5_MAX_RE = -1e-4
S5_KTILES = 4
S5_NSTATE = S5_GROUPS * S5_STATE
SAMPLE_PAD = 8
KV_ROWS = 2 * N_HEADS
VMEM_LIMIT_V7X = 56 * 1024 * 1024


def _cparams(*sem):
    return pltpu.CompilerParams(dimension_semantics=sem, vmem_limit_bytes=VMEM_LIMIT_V7X)


def _const_spec(shape):
    nd = len(shape)
    return pl.BlockSpec(shape, lambda *_: (0,) * nd, pipeline_mode=pl.Buffered(1))


def _dot(a, b):
    return jnp.dot(a, b, preferred_element_type=F32)


def _dot_nt(a, b):
    return lax.dot_general(a, b, (((1,), (1,)), ((), ())), preferred_element_type=F32)


def _dot_tn(a, b):
    return lax.dot_general(a, b, (((0,), (0,)), ((), ())), preferred_element_type=F32)


def _rms(x, w):
    ms = jnp.mean(x * x, axis=-1, keepdims=True)
    return x * lax.rsqrt(ms + NORM_EPS) * w


def _silu(x):
    return x * jax.nn.sigmoid(x)


def _hgrn_gates(zf, lb):
    ez = jnp.exp(-jnp.abs(zf))
    one_ez = 1.0 + ez
    log_sig = jnp.minimum(zf, 0.0) - jnp.log(one_ez)
    log_f = log_sig + jnp.log(1.0 + lb * jnp.exp(jnp.minimum(-zf, A_EXP_CLIP)))
    r = 1.0 / one_ez
    key = (1.0 - lb) * jnp.where(zf >= 0.0, ez * r, r)
    return log_f, key


SUBLANES = 8


def _rows_from_partner(z, bit, take_upper):
    c = z.shape[0]
    step = 1 << bit
    if step < SUBLANES:
        z3 = z.reshape(c // SUBLANES, SUBLANES, z.shape[1])
        sub = lax.broadcasted_iota(jnp.int32, z3.shape, 1)
        shift = SUBLANES - step if take_upper else step
        moved = pltpu.roll(z3, shift, 1)
        keep = ((sub >> bit) & 1) == (1 if take_upper else 0)
        return jnp.where(keep, z3, moved).reshape(z.shape)
    parts = []
    for k in range(c // (2 * step)):
        src = z[2 * k * step + step:2 * (k + 1) * step] if take_upper else z[2 * k * step:2 * k * step + step]
        parts += [src, src]
    return jnp.concatenate(parts, axis=0)


def _hgrn_level_matrix(c):
    t = np.arange(c)[:, None]
    s = np.arange(c)[None, :]
    x = t ^ s
    lev = np.zeros((c, c), np.int32)
    nz = x > 0
    lev[nz] = np.floor(np.log2(x[nz])).astype(np.int32) + 1
    return np.where(s <= t, lev, -1).astype(np.int32)


def _hgrn_prompt_kernel(x_ref, nw_ref, win_ref, lb_ref, onw_ref, wout_ref, fnw_ref, tri_ref, lvl_ref,
                        xo_ref, st_ref,
                        q_s, k_s, v_s, g_s, gate_s, o_s, st_s, *, chunk, final_norm):
    ti = pl.program_id(1)
    tb = x_ref.shape[0]
    levels = int(math.log2(chunk))

    @pl.when(ti == 0)
    def _():
        st_s[...] = jnp.zeros_like(st_s)

    x = x_ref[...]
    h = _rms(x, nw_ref[...]).astype(BF16)
    q_s[...] = _silu(_dot(h, win_ref[:, 0:D_MODEL]))
    log_f, key = _hgrn_gates(_dot(h, win_ref[:, D_MODEL:2 * D_MODEL]), lb_ref[...])
    g_s[...] = log_f
    k_s[...] = key
    v_s[...] = _dot(h, win_ref[:, 2 * D_MODEL:3 * D_MODEL])
    gate_s[...] = _silu(_dot(h, win_ref[:, 3 * D_MODEL:4 * D_MODEL]))

    def head_body(ci, hh):
        rows = slice(ci * chunk, (ci + 1) * chunk)
        cs = slice(hh * HEAD_DIM, (hh + 1) * HEAD_DIM)
        qh = q_s[rows, cs]
        kh = k_s[rows, cs]
        vh = v_s[rows, cs].astype(BF16)
        gh = g_s[rows, cs]
        g_hi = gh.astype(BF16)
        g_lo = (gh - g_hi.astype(F32)).astype(BF16)
        b = _dot(tri_ref[...], jnp.concatenate([g_hi, g_lo], axis=1))
        b = b[:, :HEAD_DIM] + b[:, HEAD_DIM:]
        b_last = b[chunk - 1:chunk, :]
        st = st_s[hh]
        o = _dot_nt((qh * jnp.exp(b)).astype(BF16), st.astype(BF16))
        k_dec = (kh * jnp.exp(b_last - b)).astype(BF16)
        st_s[hh] = st * jnp.exp(b_last) + _dot_tn(vh, k_dec)
        p = _dot_nt(qh.astype(BF16), kh.astype(BF16)).astype(BF16) * lvl_ref[0]
        z = b
        for lev in range(1, levels + 1):
            bit = lev - 1
            edge = _rows_from_partner(z, bit, take_upper=False)
            e_l = jnp.exp2(jnp.abs(b - edge) * (-LOG2_E))
            sc = _dot_nt((qh * e_l).astype(BF16), (kh * e_l).astype(BF16))
            p = p + sc.astype(BF16) * lvl_ref[lev]
            if lev < levels:
                z = _rows_from_partner(z, bit, take_upper=True)
        o = o + _dot(p, vh)
        ms = jnp.mean(o * o, axis=-1, keepdims=True)
        o_s[rows, cs] = o * lax.rsqrt(ms + NORM_EPS) * onw_ref[:, cs]

    for ci in range(tb // chunk):
        for hh in range(N_HEADS):
            head_body(ci, hh)

    y = (o_s[...] * gate_s[...]).astype(BF16)
    xo = x + _dot(y, wout_ref[...])
    if final_norm:
        xo = _rms(xo, fnw_ref[...])
    xo_ref[...] = xo

    @pl.when(ti == pl.num_programs(1) - 1)
    def _():
        for hh in range(N_HEADS):
            st_ref[hh] = st_s[hh].T


def _hgrn_prompt(x, nw, win, lb, onw, wout, fnw, *, final_norm, tb=512, chunk=128):
    batch, seq, _ = x.shape
    tri = jnp.asarray(np.tril(np.ones((chunk, chunk), np.float32)), BF16)
    lvl_idx = _hgrn_level_matrix(chunk)
    lvl = jnp.asarray(np.stack([lvl_idx == lev for lev in range(int(math.log2(chunk)) + 1)]).astype(np.float32), BF16)
    row_scratch = pltpu.VMEM((tb, D_MODEL), F32)
    x_spec = pl.BlockSpec((None, tb, D_MODEL), lambda b, t: (b, t, 0))
    return pl.pallas_call(
        functools.partial(_hgrn_prompt_kernel, chunk=chunk, final_norm=final_norm),
        grid=(batch, seq // tb),
        in_specs=[x_spec, _const_spec((1, D_MODEL)), _const_spec((D_MODEL, 4 * D_MODEL)),
                  _const_spec((1, D_MODEL)), _const_spec((1, D_MODEL)), _const_spec((D_MODEL, D_MODEL)),
                  _const_spec((1, D_MODEL)), _const_spec(tri.shape), _const_spec(lvl.shape)],
        out_specs=[x_spec, pl.BlockSpec((None, N_HEADS, HEAD_DIM, HEAD_DIM), lambda b, t: (b, 0, 0, 0))],
        out_shape=[jax.ShapeDtypeStruct((batch, seq, D_MODEL), F32),
                   jax.ShapeDtypeStruct((batch, N_HEADS, HEAD_DIM, HEAD_DIM), F32)],
        scratch_shapes=[row_scratch] * 6 + [pltpu.VMEM((N_HEADS, HEAD_DIM, HEAD_DIM), F32)],
        compiler_params=_cparams("arbitrary", "arbitrary"),
        name="hgrn_prompt",
    )(x, nw, win, lb, onw, wout, fnw, tri, lvl)


def _hgrn_sample_kernel(x_ref, s0_ref, nw_ref, win_ref, lb_ref, onw_ref, wout_ref, fnw_ref,
                        xo_ref, s_ref, q_s, k_s, v_s, f_s, o_s, *, n_tok, final_norm):
    nb = s0_ref.shape[0]
    x = x_ref[...]
    h = _rms(x, nw_ref[...]).astype(BF16)
    q_s[...] = _silu(_dot(h, win_ref[:, 0:D_MODEL]))
    log_f, key = _hgrn_gates(_dot(h, win_ref[:, D_MODEL:2 * D_MODEL]), lb_ref[...])
    f_s[...] = jnp.exp(log_f)
    k_s[...] = key
    v_s[...] = _dot(h, win_ref[:, 2 * D_MODEL:3 * D_MODEL])
    gate = _silu(_dot(h, win_ref[:, 3 * D_MODEL:4 * D_MODEL]))
    row = lax.broadcasted_iota(jnp.int32, (SAMPLE_PAD, HEAD_DIM), 0)

    valid = row < n_tok

    for bi in range(nb):
        rows = slice(bi * SAMPLE_PAD, (bi + 1) * SAMPLE_PAD)
        for hh in range(N_HEADS):
            cs = slice(hh * HEAD_DIM, (hh + 1) * HEAD_DIM)
            qh, kh, vh = q_s[rows, cs], k_s[rows, cs], v_s[rows, cs]
            fm = jnp.where(valid, f_s[rows, cs], 1.0)
            run = [None, fm]
            for d in range(1, n_tok):
                run.append(run[d] * jnp.where(row >= d, pltpu.roll(fm, d, 0), 1.0))
            st = s0_ref[bi, hh]
            o = _dot((qh * run[n_tok]).astype(BF16), st.astype(BF16))
            o = o + jnp.sum(qh * kh, axis=-1, keepdims=True) * vh
            for d in range(1, n_tok):
                s_d = jnp.sum(qh * pltpu.roll(kh, d, 0) * run[d], axis=-1, keepdims=True)
                o = o + jnp.where(row >= d, s_d, 0.0) * pltpu.roll(vh, d, 0)
            after = jnp.ones_like(fm)
            for d in range(1, n_tok):
                after = after * pltpu.roll(fm, SAMPLE_PAD - d, 0)
            k_dec = jnp.where(valid, kh * after, 0.0)
            total = run[n_tok].T[:, n_tok - 1:n_tok]
            s_ref[bi, hh] = st * total + _dot_tn(k_dec, vh)
            ms = jnp.mean(o * o, axis=-1, keepdims=True)
            o_s[rows, cs] = o * lax.rsqrt(ms + NORM_EPS) * onw_ref[:, cs]

    y = (o_s[...] * gate).astype(BF16)
    xo = x + _dot(y, wout_ref[...])
    if final_norm:
        xo = _rms(xo, fnw_ref[...])
    xo_ref[...] = xo


def _hgrn_sample(x, s0, layer, nw, win, lb, onw, wout, fnw, *, n_tok, final_norm, nb=8):
    n_seq = s0.shape[1]
    assert 2 * n_tok - 1 <= SAMPLE_PAD
    rows = nb * SAMPLE_PAD
    row_scratch = pltpu.VMEM((rows, D_MODEL), F32)
    return pl.pallas_call(
        functools.partial(_hgrn_sample_kernel, n_tok=n_tok, final_norm=final_norm),
        grid=(n_seq // nb,),
        in_specs=[pl.BlockSpec((rows, D_MODEL), lambda i: (i, 0)),
                  pl.BlockSpec((None, nb, N_HEADS, HEAD_DIM, HEAD_DIM), lambda i: (layer, i, 0, 0, 0)),
                  _const_spec((1, D_MODEL)), _const_spec((D_MODEL, 4 * D_MODEL)), _const_spec((1, D_MODEL)),
                  _const_spec((1, D_MODEL)), _const_spec((D_MODEL, D_MODEL)), _const_spec((1, D_MODEL))],
        out_specs=[pl.BlockSpec((rows, D_MODEL), lambda i: (i, 0)),
                   pl.BlockSpec((nb, N_HEADS, HEAD_DIM, HEAD_DIM), lambda i: (i, 0, 0, 0))],
        out_shape=[jax.ShapeDtypeStruct(x.shape, F32), jax.ShapeDtypeStruct(s0.shape[1:], F32)],
        scratch_shapes=[row_scratch] * 5,
        compiler_params=_cparams("arbitrary"),
        name="hgrn_sample",
    )(x, s0, nw, win, lb, onw, wout, fnw)


def _attn_proj_kernel(x_ref, cos_ref, sin_ref, nw_ref, win_ref,
                      g0_ref, g1_ref, g2_ref, gate_ref, kv0_ref, kv1_ref, kv2_ref):
    tb = x_ref.shape[0]
    h = _rms(x_ref[...], nw_ref[...]).astype(BF16)
    cos = cos_ref[...]
    sin = sin_ref[...]

    def heads(a):
        return [a[:, hh * HEAD_DIM:(hh + 1) * HEAD_DIM] for hh in range(N_HEADS)]

    def rope(cols):
        return [ah * cos + pltpu.roll(ah, HEAD_DIM // 2, 1) * sin for ah in cols]

    def token_tiles(cols, keep):
        return jnp.swapaxes(jnp.stack([c[tb - keep:, :] for c in cols], axis=0), 0, 1)

    for gi, (qkv_ref, kv_ref) in enumerate(((g0_ref, kv0_ref), (g1_ref, kv1_ref), (g2_ref, kv2_ref))):
        base = 3 * gi * D_MODEL
        dil = qkv_ref.shape[0]
        q_cols = rope(heads(_dot(h, win_ref[:, base:base + D_MODEL])))
        k_cols = rope(heads(_dot(h, win_ref[:, base + D_MODEL:base + 2 * D_MODEL])))
        v = _dot(h, win_ref[:, base + 2 * D_MODEL:base + 3 * D_MODEL])
        q = jnp.concatenate(q_cols, axis=1) * (HEAD_DIM ** -0.5)
        k = jnp.concatenate(k_cols, axis=1)
        for ci, val in enumerate((q, k, v)):
            if dil == 1:
                val = val.reshape(1, tb, D_MODEL)
            else:
                val = jnp.swapaxes(val.reshape(tb // dil, dil, D_MODEL), 0, 1)
            qkv_ref[:, :, ci * D_MODEL:(ci + 1) * D_MODEL] = val.astype(BF16)
        keep = kv_ref.shape[0]
        kv_ref[:, 0] = token_tiles(k_cols, keep)
        kv_ref[:, 1] = token_tiles(heads(v), keep)
    gate_ref[...] = _silu(_dot(h, win_ref[:, 9 * D_MODEL:10 * D_MODEL])).astype(BF16)


def _attn_proj(x, cos, sin, nw, win, *, dils, keeps, tb):
    batch, seq, _ = x.shape
    qkv_specs, qkv_shapes, kv_specs, kv_shapes = [], [], [], []
    for dil, keep in zip(dils, keeps):
        qkv_specs.append(pl.BlockSpec((None, dil, tb // dil, 3 * D_MODEL), lambda b, t: (b, 0, t, 0)))
        qkv_shapes.append(jax.ShapeDtypeStruct((batch, dil, seq // dil, 3 * D_MODEL), BF16))
        kb = min(tb, keep)
        first = (seq - keep) // tb if keep >= tb else 0
        kv_specs.append(pl.BlockSpec(
            (None, kb, 2, N_HEADS, HEAD_DIM),
            lambda b, t, first=first, kb=kb: (b, jnp.maximum(t - first, 0) if kb == tb else 0, 0, 0, 0)))
        kv_shapes.append(jax.ShapeDtypeStruct((batch, keep, 2, N_HEADS, HEAD_DIM), F32))
    return pl.pallas_call(
        _attn_proj_kernel,
        grid=(batch, seq // tb),
        in_specs=[pl.BlockSpec((None, tb, D_MODEL), lambda b, t: (b, t, 0)),
                  pl.BlockSpec((tb, HEAD_DIM), lambda b, t: (t, 0)),
                  pl.BlockSpec((tb, HEAD_DIM), lambda b, t: (t, 0)),
                  _const_spec((1, D_MODEL)), _const_spec((D_MODEL, 10 * D_MODEL))],
        out_specs=qkv_specs + [pl.BlockSpec((None, tb, D_MODEL), lambda b, t: (b, t, 0))] + kv_specs,
        out_shape=qkv_shapes + [jax.ShapeDtypeStruct((batch, seq, D_MODEL), BF16)] + kv_shapes,
        compiler_params=_cparams("arbitrary", "arbitrary"),
        name="attn_proj",
    )(x, cos, sin, nw, win)


ATTN_UNITS = 8


def _attn_group_kernel(bps_ref, q_ref, kc_ref, kp_ref, vc_ref, vp_ref, o_ref, lse_ref):
    units = q_ref.shape[0]
    first = pl.program_id(1) * units
    blocks_per_seq = bps_ref[0]
    a = lax.broadcasted_iota(jnp.int32, (ATTN_BLOCK, ATTN_BLOCK), 0)
    c = lax.broadcasted_iota(jnp.int32, (ATTN_BLOCK, ATTN_BLOCK), 1)
    mask_cur = c <= a
    lane = lax.broadcasted_iota(jnp.int32, (ATTN_BLOCK, HEAD_DIM), 1)
    ones_col = jnp.where(lane == 0, 1.0, 0.0).astype(BF16)
    for u in range(units):
        k_prev, v_prev = (kp_ref.at[0], vp_ref.at[0]) if u == 0 else (kc_ref.at[u - 1], vc_ref.at[u - 1])
        mask_prev = (c >= a) & (lax.rem(first + u, blocks_per_seq) > 0)
        mask = jnp.concatenate([mask_prev, mask_cur], axis=1)
        lse_tile = jnp.zeros((ATTN_BLOCK, HEAD_DIM), F32)
        for hh in range(N_HEADS):
            cs = slice(hh * HEAD_DIM, (hh + 1) * HEAD_DIM)
            keys = jnp.concatenate([k_prev[:, cs], kc_ref[u, :, cs]], axis=0)
            vals = jnp.concatenate([jnp.concatenate([v_prev[:, cs], ones_col], axis=1),
                                    jnp.concatenate([vc_ref[u, :, cs], ones_col], axis=1)], axis=0)
            s = jnp.where(mask, _dot_nt(q_ref[u, :, cs], keys), NEG_BIG)
            m = jnp.max(s, axis=1, keepdims=True)
            ov = _dot(jnp.exp(s - m).astype(BF16), vals)
            den = ov[:, HEAD_DIM:HEAD_DIM + 1]
            o_ref[u, :, cs] = (ov[:, :HEAD_DIM] / den).astype(BF16)
            lse_tile = jnp.where(lane == hh, m + jnp.log(den), lse_tile)
        lse_ref[u] = lse_tile


def _attn_group(qkv):
    batch, dil, n, _ = qkv.shape
    blocks_per_seq = n // ATTN_BLOCK
    n_blocks = dil * blocks_per_seq
    units = ATTN_UNITS
    assert n_blocks % units == 0
    blocks = qkv.reshape(batch, n_blocks, ATTN_BLOCK, 3 * D_MODEL)

    def cur(col):
        return pl.BlockSpec((None, units, ATTN_BLOCK, D_MODEL), lambda b, i: (b, i, 0, col))

    def prev(col):
        return pl.BlockSpec((None, 1, ATTN_BLOCK, D_MODEL), lambda b, i: (b, jnp.maximum(i * units - 1, 0), 0, col))

    o, lse = pl.pallas_call(
        _attn_group_kernel,
        grid=(batch, n_blocks // units),
        in_specs=[pl.BlockSpec(memory_space=pltpu.SMEM), cur(0), cur(1), prev(1), cur(2), prev(2)],
        out_specs=[pl.BlockSpec((None, units, ATTN_BLOCK, D_MODEL), lambda b, i: (b, i, 0, 0)),
                   pl.BlockSpec((None, units, ATTN_BLOCK, HEAD_DIM), lambda b, i: (b, i, 0, 0))],
        out_shape=[jax.ShapeDtypeStruct((batch, n_blocks, ATTN_BLOCK, D_MODEL), BF16),
                   jax.ShapeDtypeStruct((batch, n_blocks, ATTN_BLOCK, HEAD_DIM), F32)],
        compiler_params=_cparams("arbitrary", "arbitrary"),
        name=f"attn_group_d{dil}",
    )(jnp.full((1,), blocks_per_seq, jnp.int32), blocks, blocks, blocks, blocks, blocks)
    return o.reshape(batch, dil, n, D_MODEL), lse.reshape(batch, dil, n, HEAD_DIM)


def _merge_heads(outs, lses):
    m = functools.reduce(jnp.maximum, lses)
    es = [jnp.exp(l - m) for l in lses]
    den = functools.reduce(jnp.add, es)
    ws = [e / den for e in es]
    cols = []
    for hh in range(N_HEADS):
        cs = slice(hh * HEAD_DIM, (hh + 1) * HEAD_DIM)
        cols.append(functools.reduce(jnp.add, [w[:, hh:hh + 1] * o[:, cs] for w, o in zip(ws, outs)]))
    return jnp.concatenate(cols, axis=1)


def _attn_out_kernel(o0_ref, o1_ref, o2_ref, l0_ref, l1_ref, l2_ref, gate_ref, x_ref, wout_ref, xo_ref):
    def natural(ref):
        dil, n, w = ref.shape
        val = ref[...].astype(F32)
        return val[0] if dil == 1 else jnp.swapaxes(val, 0, 1).reshape(dil * n, w)

    o = _merge_heads([natural(r) for r in (o0_ref, o1_ref, o2_ref)], [natural(r) for r in (l0_ref, l1_ref, l2_ref)])
    y = (o * gate_ref[...]).astype(BF16)
    xo_ref[...] = x_ref[...] + _dot(y, wout_ref[...])


def _attn_out(outs, lses, gate, x, wout, *, tb):
    batch, seq, _ = x.shape
    regrouped = lambda a: pl.BlockSpec((None, a.shape[1], tb // a.shape[1], a.shape[3]), lambda b, t: (b, 0, t, 0))
    row = pl.BlockSpec((None, tb, D_MODEL), lambda b, t: (b, t, 0))
    return pl.pallas_call(
        _attn_out_kernel,
        grid=(batch, seq // tb),
        in_specs=[regrouped(a) for a in outs] + [regrouped(a) for a in lses] + [row, row, _const_spec((D_MODEL, D_MODEL))],
        out_specs=row,
        out_shape=jax.ShapeDtypeStruct((batch, seq, D_MODEL), F32),
        compiler_params=_cparams("arbitrary", "arbitrary"),
        name="attn_out",
    )(*outs, *lses, gate, x, wout)


def _attn_sample_kernel(q0_ref, q1_ref, q2_ref, n0_ref, n1_ref, n2_ref, c0_ref, c1_ref, c2_ref,
                        gate_ref, x_ref, wout_ref, xo_ref, o_s, *, n_tok):
    nb = c0_ref.shape[0]
    pos_i = lax.broadcasted_iota(jnp.int32, (ATTN_BLOCK, N_HEADS, 1), 0)
    row_t = lax.broadcasted_iota(jnp.int32, (SAMPLE_PAD, D_MODEL), 0)
    q_refs, new_refs, cache_refs = (q0_ref, q1_ref, q2_ref), (n0_ref, n1_ref, n2_ref), (c0_ref, c1_ref, c2_ref)

    def seq_body(bi, carry):
        r0 = pl.multiple_of(bi * SAMPLE_PAD, SAMPLE_PAD)
        q_rows = [q_ref[pl.ds(r0, SAMPLE_PAD), :].astype(F32) for q_ref in q_refs]
        o_tile = jnp.zeros((SAMPLE_PAD, D_MODEL), F32)
        for t in range(n_tok):
            outs, lses = [], []
            for gi, (_, dil) in enumerate(ATTN_GROUPS):
                q_t = jnp.concatenate([q_rows[gi][t:t + 1, hh * HEAD_DIM:(hh + 1) * HEAD_DIM]
                                       for hh in range(N_HEADS)], axis=0)
                res = t % dil
                k_c = cache_refs[gi][bi, :, res, 0]
                v_c = cache_refs[gi][bi, :, res, 1]
                s_c = jnp.sum(k_c * q_t[None], axis=-1, keepdims=True)
                s_c = jnp.where(pos_i * dil + res >= t, s_c, NEG_BIG)
                new_t = [u for u in range(t + 1) if (t - u) % dil == 0]
                k_n = [new_refs[gi][r0 + u, 0] for u in new_t]
                v_n = [new_refs[gi][r0 + u, 1] for u in new_t]
                s_n = [jnp.sum(kk * q_t, axis=-1, keepdims=True) for kk in k_n]
                m = functools.reduce(jnp.maximum, s_n + [jnp.max(s_c, axis=0)])
                p_c = jnp.exp(s_c - m[None])
                p_n = [jnp.exp(s - m) for s in s_n]
                den = functools.reduce(jnp.add, p_n + [jnp.sum(p_c, axis=0)])
                acc = functools.reduce(jnp.add, [p * vv for p, vv in zip(p_n, v_n)] + [jnp.sum(p_c * v_c, axis=0)])
                outs.append(acc / den)
                lses.append(m + jnp.log(den))
            m_g = functools.reduce(jnp.maximum, lses)
            e_g = [jnp.exp(l - m_g) for l in lses]
            den_g = functools.reduce(jnp.add, e_g)
            o_t = functools.reduce(jnp.add, [e / den_g * o for e, o in zip(e_g, outs)])
            o_row = jnp.concatenate([o_t[hh:hh + 1, :] for hh in range(N_HEADS)], axis=1)
            o_tile = jnp.where(row_t == t, o_row, o_tile)
        o_s[pl.ds(r0, SAMPLE_PAD), :] = o_tile
        return carry

    lax.fori_loop(0, nb, seq_body, 0)
    y = (o_s[...] * gate_ref[...]).astype(BF16)
    xo_ref[...] = x_ref[...] + _dot(y, wout_ref[...])


def _attn_sample(qkvs, news, caches, layer, gate, x, wout, *, n_tok, nb=2):
    n_seq = caches[0].shape[1]
    rows = nb * SAMPLE_PAD
    c_views, c_specs = [], []
    for (window, dil), cache in zip(ATTN_GROUPS, caches):
        assert cache.shape[2] == window == ATTN_BLOCK * dil and (n_tok <= dil or dil == 1)
        c_views.append(cache.reshape(cache.shape[0], n_seq, ATTN_BLOCK, dil, 2, N_HEADS, HEAD_DIM))
        n_res = min(dil, n_tok)
        c_specs.append(pl.BlockSpec((None, nb, ATTN_BLOCK, n_res, 2, N_HEADS, HEAD_DIM),
                                    lambda i: (layer, i, 0, 0, 0, 0, 0)))
    row = lambda w: pl.BlockSpec((rows, w), lambda i: (i, 0))
    new_spec = pl.BlockSpec((rows, 2, N_HEADS, HEAD_DIM), lambda i: (i, 0, 0, 0))
    return pl.pallas_call(
        functools.partial(_attn_sample_kernel, n_tok=n_tok),
        grid=(n_seq // nb,),
        in_specs=[row(D_MODEL)] * 3 + [new_spec] * 3 + c_specs + [row(D_MODEL), row(D_MODEL),
                                                                  _const_spec((D_MODEL, D_MODEL))],
        out_specs=row(D_MODEL),
        out_shape=jax.ShapeDtypeStruct(x.shape, F32),
        scratch_shapes=[pltpu.VMEM((rows, D_MODEL), F32)],
        compiler_params=_cparams("arbitrary"),
        name="attn_sample",
    )(*qkvs, *news, *c_views, gate, x, wout)


ROLL_GROUPS = 8


def _cache_roll_kernel(c_ref, nxt_ref, new_ref, o_ref):
    i = pl.program_id(0)
    last = pl.num_programs(0) - 1
    g = o_ref.shape[1]
    o_ref[:, 0:g - 1] = c_ref[:, 1:g]
    o_ref[:, g - 1:g] = jnp.where(i < last, nxt_ref[...], new_ref[...])


def _cache_roll(cache, new, layer, *, n_tok):
    _, n_seq, length = cache.shape[:3]
    n_grp = length // n_tok
    g = ROLL_GROUPS
    rows = n_tok * KV_ROWS
    c_view = cache.reshape(cache.shape[0], n_seq, n_grp, rows, HEAD_DIM)
    n_view = new.reshape(n_seq, 1, rows, HEAD_DIM)
    out = pl.pallas_call(
        _cache_roll_kernel,
        grid=(n_grp // g,),
        in_specs=[pl.BlockSpec((None, n_seq, g, rows, HEAD_DIM), lambda i: (layer, 0, i, 0, 0)),
                  pl.BlockSpec((None, n_seq, 1, rows, HEAD_DIM),
                               lambda i: (layer, 0, jnp.minimum((i + 1) * g, n_grp - 1), 0, 0)),
                  pl.BlockSpec((n_seq, 1, rows, HEAD_DIM), lambda i: (0, 0, 0, 0))],
        out_specs=pl.BlockSpec((n_seq, g, rows, HEAD_DIM), lambda i: (0, i, 0, 0)),
        out_shape=jax.ShapeDtypeStruct((n_seq, n_grp, rows, HEAD_DIM), cache.dtype),
        compiler_params=_cparams("arbitrary"),
        name="cache_roll",
    )(c_view, c_view, n_view)
    return out.reshape(cache.shape[1:])


def _gelu_tanh(y):
    return 0.5 * y * (1.0 + jnp.tanh(math.sqrt(2.0 / math.pi) * (y + 0.044715 * (y * y * y))))


def _s5_kernel(x_ref, s0_ref, nw_ref, win_ref, wb_ref, are_ref, aim_ref, wc_ref, dsk_ref, wglu_ref, bglu_ref,
               wout_ref, xo_ref, sfin_ref, bu_s, st_s, *, n_seq, tb, seq_major_io):
    ti = pl.program_id(0)
    tile = 2 * S5_NSTATE // S5_KTILES
    half = tile // 2
    lanes = 1024

    @pl.when(ti == 0)
    def _():
        st_s[...] = s0_ref[...]

    if seq_major_io:
        x = jnp.swapaxes(x_ref[...], 0, 1).reshape(tb * n_seq, D_MODEL)
    else:
        x = x_ref[...]
    h = _rms(x, nw_ref[...]).astype(BF16)
    u = _dot(h, win_ref[:, 0:D_MODEL])
    gate = _silu(_dot(h, win_ref[:, D_MODEL:2 * D_MODEL]))
    ub = u.astype(BF16)
    kw = D_MODEL // S5_KTILES
    ys = []
    for kt in range(S5_KTILES):
        bu_s[:, kt * tile:(kt + 1) * tile] = _dot(ub[:, kt * kw:(kt + 1) * kw], wb_ref[kt])
        for sg in range(n_seq // 8):
            srow = slice(sg * 8, (sg + 1) * 8)
            for part in range(half // lanes):
                c_re = slice(kt * tile + part * lanes, kt * tile + (part + 1) * lanes)
                c_im = slice(kt * tile + half + part * lanes, kt * tile + half + (part + 1) * lanes)
                a_re = are_ref[:, c_re]
                a_im = aim_ref[:, c_re]
                s_re, s_im = st_s[srow, c_re], st_s[srow, c_im]
                for t in range(tb):
                    rows = slice(t * n_seq + sg * 8, t * n_seq + (sg + 1) * 8)
                    s_re, s_im = (a_re * s_re - a_im * s_im + bu_s[rows, c_re],
                                  a_re * s_im + a_im * s_re + bu_s[rows, c_im])
                    bu_s[rows, c_re] = s_re
                    bu_s[rows, c_im] = s_im
                st_s[srow, c_re] = s_re
                st_s[srow, c_im] = s_im
        ys.append(_dot(bu_s[:, kt * tile:(kt + 1) * tile].astype(BF16), wc_ref[kt]))
    y = jnp.concatenate(ys, axis=1) + dsk_ref[...] * u
    y = _gelu_tanh(y)
    y = y * jax.nn.sigmoid(_dot(y.astype(BF16), wglu_ref[...]) + bglu_ref[...])
    y = (y * gate).astype(BF16)
    xo = x + _dot(y, wout_ref[...])
    if seq_major_io:
        xo_ref[...] = jnp.swapaxes(xo.reshape(tb, n_seq, D_MODEL), 0, 1)
    else:
        xo_ref[...] = xo

    @pl.when(ti == pl.num_programs(0) - 1)
    def _():
        sfin_ref[...] = st_s[...]


def _s5_layer(x, s0, nw, win, wb, a_re, a_im, wc, dsk, wglu, bglu, wout, *, n_seq, seq, tb, seq_major_io):
    rows = tb * n_seq
    ncol = 2 * S5_NSTATE
    if seq_major_io:
        x_spec = pl.BlockSpec((n_seq, tb, D_MODEL), lambda t: (0, t, 0))
    else:
        x_spec = pl.BlockSpec((rows, D_MODEL), lambda t: (t, 0))
    return pl.pallas_call(
        functools.partial(_s5_kernel, n_seq=n_seq, tb=tb, seq_major_io=seq_major_io),
        grid=(seq // tb,),
        in_specs=[x_spec, _const_spec((n_seq, ncol)),
                  _const_spec((1, D_MODEL)), _const_spec((D_MODEL, 2 * D_MODEL)), _const_spec(wb.shape),
                  _const_spec((8, ncol)), _const_spec((8, ncol)), _const_spec(wc.shape),
                  _const_spec((1, D_MODEL)), _const_spec((D_MODEL, D_MODEL)), _const_spec((1, D_MODEL)),
                  _const_spec((D_MODEL, D_MODEL))],
        out_specs=[x_spec, pl.BlockSpec((n_seq, ncol), lambda t: (0, 0))],
        out_shape=[jax.ShapeDtypeStruct(x.shape, F32), jax.ShapeDtypeStruct((n_seq, ncol), F32)],
        scratch_shapes=[pltpu.VMEM((rows, ncol), F32), pltpu.VMEM((n_seq, ncol), F32)],
        compiler_params=_cparams("arbitrary"),
        name="s5_layer",
    )(x, s0, nw, win, wb, a_re, a_im, wc, dsk, wglu, bglu, wout)


def _s5_params(a_re, a_im, b_re, b_im, c_re, c_im, log_dt):
    lam_re = jnp.minimum(a_re, S5_MAX_RE)
    lam_im = a_im
    dt = jnp.exp(log_dt)[:, None]
    mag = jnp.exp(lam_re * dt)
    bar_re = mag * jnp.cos(lam_im * dt)
    bar_im = mag * jnp.sin(lam_im * dt)
    den = lam_re * lam_re + lam_im * lam_im
    xr = bar_re - 1.0
    coef_re = (xr * lam_re + bar_im * lam_im) / den
    coef_im = (bar_im * lam_re - xr * lam_im) / den
    bbar_re = coef_re[..., None] * b_re - coef_im[..., None] * b_im
    bbar_im = coef_re[..., None] * b_im + coef_im[..., None] * b_re
    gl = S5_GROUPS // S5_KTILES
    eye = jnp.eye(gl, dtype=F32)

    def to_cols(a):
        return a.reshape(S5_KTILES, gl * S5_STATE)

    def b_tile(bb):
        bb = bb.reshape(S5_KTILES, gl, S5_STATE, S5_GROUP_CH)
        return jnp.einsum('kgpc,gh->kgchp', bb, eye).reshape(S5_KTILES, gl * S5_GROUP_CH, gl * S5_STATE)

    def c_tile(cc):
        cc = cc.reshape(S5_KTILES, gl, S5_GROUP_CH, S5_STATE)
        return jnp.einsum('kgcp,gh->kgphc', cc, eye).reshape(S5_KTILES, gl * S5_STATE, gl * S5_GROUP_CH)

    wb = jnp.concatenate([b_tile(bbar_re), b_tile(bbar_im)], axis=2).astype(BF16)
    wc = jnp.concatenate([c_tile(c_re), -c_tile(c_im)], axis=1).astype(BF16)
    cols = lambda a: jnp.concatenate([to_cols(a), to_cols(a)], axis=1).reshape(1, -1)
    a_re_cols = jnp.broadcast_to(cols(bar_re), (8, 2 * S5_NSTATE))
    a_im_cols = jnp.broadcast_to(cols(bar_im), (8, 2 * S5_NSTATE))
    return wb, wc, a_re_cols, a_im_cols


def _s5_state_to_cols(s):
    n = s.shape[0]
    gl = S5_GROUPS // S5_KTILES
    s = s.reshape(n, S5_KTILES, gl * S5_STATE, 2)
    return jnp.moveaxis(s, 3, 2).reshape(n, 2 * S5_NSTATE)


def _s5_cols_to_state(c):
    n = c.shape[0]
    gl = S5_GROUPS // S5_KTILES
    c = c.reshape(n, S5_KTILES, 2, gl * S5_STATE)
    return jnp.moveaxis(c, 2, 3).reshape(n, S5_GROUPS, S5_STATE, 2)


def _rope_tables(pos):
    half = HEAD_DIM // 2
    inv_freq = ROPE_THETA ** (-jnp.arange(half, dtype=F32) / half)
    ang = pos[:, None] * inv_freq[None, :]
    cos, sin = jnp.cos(ang), jnp.sin(ang)
    return jnp.concatenate([cos, cos], axis=1), jnp.concatenate([-sin, sin], axis=1)


def kernel(x_prompt, x_sample, state_hgrn, cache_kv_w128, cache_kv_w512, cache_kv_w2048, state_s5,
           norm_w, final_norm_w, a_w_in, a_lb_logits, a_onorm_w, a_w_out, b_w_in, b_w_out,
           c_w_in, c_a_re, c_a_im, c_b_re, c_b_im, c_c_re, c_c_im, c_d, c_log_dt, c_w_glu, c_b_glu, c_w_out):
    batch, seq, _ = x_prompt.shape
    n_seq, n_tok, _ = x_sample.shape
    depth = norm_w.shape[0]
    caches = (cache_kv_w128, cache_kv_w512, cache_kv_w2048)
    dils = [d for _, d in ATTN_GROUPS]
    row = lambda a: a.reshape(1, -1)

    p_lb = jax.nn.softmax(a_lb_logits.astype(F32), axis=0)
    lower_bounds = jnp.cumsum(p_lb, axis=0) - p_lb[0:1]
    fnw = row(final_norm_w)

    s_rows = n_seq * SAMPLE_PAD
    xs = jnp.pad(x_sample, ((0, 0), (0, SAMPLE_PAD - n_tok), (0, 0))).reshape(s_rows, D_MODEL)
    xp = x_prompt
    pos_p = jnp.arange(seq, dtype=F32)
    pos_s = jnp.tile(jnp.pad(PAST_LEN + jnp.arange(n_tok, dtype=F32), (0, SAMPLE_PAD - n_tok)), n_seq)

    hgrn_p, hgrn_s, s5_p, s5_s = [], [], [], []
    kv_p = [[] for _ in ATTN_GROUPS]
    kv_s = [[] for _ in ATTN_GROUPS]
    for layer in range(depth):
        kind, j = layer % 3, layer // 3
        last = layer == depth - 1
        nw = row(norm_w[layer])
        if kind == 0:
            win, wout = a_w_in[j].astype(BF16), a_w_out[j].astype(BF16)
            lb, onw = row(lower_bounds[j]), row(a_onorm_w[j])
            xp, st = _hgrn_prompt(xp, nw, win, lb, onw, wout, fnw, final_norm=last)
            hgrn_p.append(st)
            xs, st = _hgrn_sample(xs, state_hgrn, j, nw, win, lb, onw, wout, fnw, n_tok=n_tok, final_norm=last)
            hgrn_s.append(st)
        elif kind == 1:
            win, wout = b_w_in[j].astype(BF16), b_w_out[j].astype(BF16)
            cos, sin = _rope_tables(pos_p)
            *qkvs, gate, kv0, kv1, kv2 = _attn_proj(xp, cos, sin, nw, win, dils=dils,
                                                    keeps=[min(w, seq) for w, _ in ATTN_GROUPS], tb=256)
            for g, kv in enumerate((kv0, kv1, kv2)):
                kv_p[g].append(kv)
            outs, lses = zip(*[_attn_group(qkv) for qkv in qkvs])
            xp = _attn_out(outs, lses, gate, xp, wout, tb=512)

            cos, sin = _rope_tables(pos_s)
            *qkvs, gate, kv0, kv1, kv2 = _attn_proj(xs.reshape(1, s_rows, D_MODEL), cos, sin, nw, win,
                                                    dils=[1] * len(dils), keeps=[s_rows] * len(dils), tb=s_rows)
            new_rows = [kv.reshape(s_rows, 2, N_HEADS, HEAD_DIM) for kv in (kv0, kv1, kv2)]
            xs = _attn_sample([q.reshape(s_rows, 3 * D_MODEL) for q in qkvs], new_rows, caches, j,
                              gate.reshape(s_rows, D_MODEL), xs, wout, n_tok=n_tok)
            news = [kv.reshape(n_seq, SAMPLE_PAD, 2, N_HEADS, HEAD_DIM)[:, :n_tok] for kv in new_rows]
            for g, (cache, new) in enumerate(zip(caches, news)):
                kv_s[g].append(_cache_roll(cache, new, j, n_tok=n_tok))
        else:
            wb, wc, a_re_cols, a_im_cols = _s5_params(c_a_re[j], c_a_im[j], c_b_re[j], c_b_im[j],
                                                      c_c_re[j], c_c_im[j], c_log_dt[j])
            wts = (nw, c_w_in[j].astype(BF16), wb, a_re_cols, a_im_cols, wc, row(c_d[j]),
                   c_w_glu[j].astype(BF16), row(c_b_glu[j]), c_w_out[j].astype(BF16))
            xp, sfin = _s5_layer(xp, jnp.zeros((batch, 2 * S5_NSTATE), F32), *wts,
                                 n_seq=batch, seq=seq, tb=64, seq_major_io=True)
            s5_p.append(_s5_cols_to_state(sfin))
            xs_tm = jnp.swapaxes(xs.reshape(n_seq, SAMPLE_PAD, D_MODEL)[:, :n_tok], 0, 1)
            xs_tm, sfin = _s5_layer(xs_tm.reshape(n_tok * n_seq, D_MODEL), _s5_state_to_cols(state_s5[j]), *wts,
                                    n_seq=n_seq, seq=n_tok, tb=n_tok, seq_major_io=False)
            s5_s.append(_s5_cols_to_state(sfin))
            xs = jnp.pad(jnp.swapaxes(xs_tm.reshape(n_tok, n_seq, D_MODEL), 0, 1),
                         ((0, 0), (0, SAMPLE_PAD - n_tok), (0, 0))).reshape(s_rows, D_MODEL)

    if depth % 3 != 1:
        raise NotImplementedError("the final norm is fused into a last HGRN2 layer")
    y_sample = xs.reshape(n_seq, SAMPLE_PAD, D_MODEL)[:, :n_tok]
    stack = lambda parts: jnp.stack(parts, axis=0)
    return (xp, y_sample, stack(hgrn_p), stack(hgrn_s),
            stack(kv_p[0]), stack(kv_s[0]), stack(kv_p[1]), stack(kv_s[1]), stack(kv_p[2]), stack(kv_s[2]),
            stack(s5_p), stack(s5_s))
```
